```python
import math
import jax, jax.numpy as jnp
from jax import lax
import numpy as np

D_MODEL = 1024
BATCH = 8
SEQ = 2048
DEPTH = 4
DEC_BATCH = 128
DEC_SEQ = 4
PAST_LEN = 16384
PAGE_SIZE = 128

N_MIXERS = 2
BRANCH = D_MODEL
HEAD_DIM = 128
N_HEADS = BRANCH // HEAD_DIM
HEAD_DK = HEAD_DIM
HEAD_DV = HEAD_DIM
CONV_W = 3
HGRN_CHUNK = 64
N_CONV_LAYERS = (DEPTH + 1) // 2
N_HGRN_LAYERS = DEPTH // 2
RMS_EPS = 1e-6
LOG_F_FLOOR = -20.0

kernel_name = "hybrid_shortconv_hgrn2_decode_step"


def rms_norm(x, g):
    xf = x.astype(jnp.float32)
    y = xf * lax.rsqrt(jnp.mean(xf * xf, axis=-1, keepdims=True) + RMS_EPS)
    return (y * g.astype(jnp.float32)).astype(x.dtype)


def conv_branch(h, w_in, w_conv, w_out, buf):
    T = h.shape[1]
    proj = h @ w_in
    v, bg, cg, z = jnp.split(proj, 4, axis=-1)
    u = cg * v
    u_ext = jnp.concatenate([buf.astype(u.dtype), u], axis=1)
    conv = w_conv[0] * u_ext[:, 0:T]
    for j in range(1, CONV_W):
        conv = conv + w_conv[j] * u_ext[:, j:j + T]
    new_buf = u_ext[:, -(CONV_W - 1):]
    y = (bg * conv * jax.nn.silu(z)) @ w_out
    return y, new_buf


def hgrn2_chunked(q, k, v, log_f, s0):
    b, T = q.shape[:2]
    C = math.gcd(T, HGRN_CHUNK)
    nc = T // C

    def to_chunks(a):
        return a.reshape(b, nc, C, a.shape[2], a.shape[3]).transpose(1, 0, 3, 2, 4)

    mask = jnp.tril(jnp.ones((C, C), dtype=bool))[None, None, :, :, None]

    def step(S, inp):
        qc, kc, vc, gc = inp
        bcum = jnp.cumsum(gc, axis=2)
        rel = jnp.where(mask, bcum[:, :, :, None, :] - bcum[:, :, None, :, :], -jnp.inf)
        A = jnp.einsum('bhtk,bhtsk,bhsk->bhts', qc, jnp.exp(rel), kc)
        o = jnp.einsum('bhts,bhsv->bhtv', A, vc) + jnp.einsum('bhtk,bhkv->bhtv', qc * jnp.exp(bcum), S)
        b_last = bcum[:, :, -1:, :]
        S = jnp.exp(b_last[:, :, 0, :])[..., None] * S + jnp.einsum(
            'bhsk,bhsv->bhkv', kc * jnp.exp(b_last - bcum), vc)
        return S, o

    S, o = lax.scan(step, s0, (to_chunks(q), to_chunks(k), to_chunks(v), to_chunks(log_f)))
    o = o.transpose(1, 0, 3, 2, 4).reshape(b, T, q.shape[2], v.shape[-1])
    return o, S


def hgrn_branch(h, w_in, lb, g_onorm, w_out, s0):
    b, T = h.shape[:2]
    proj = h @ w_in
    q, fpre, inp, z = jnp.split(proj, 4, axis=-1)
    shp = (b, T, N_HEADS, HEAD_DK)
    q = jax.nn.silu(q.astype(jnp.float32)).reshape(shp) * (HEAD_DK ** -0.5)
    fpre = fpre.astype(jnp.float32).reshape(shp)
    vv = inp.astype(jnp.float32).reshape(b, T, N_HEADS, HEAD_DV)
    lb = lb.reshape(N_HEADS, HEAD_DK)
    log_f = jnp.logaddexp(jnp.log(lb), jnp.log1p(-lb) + jax.nn.log_sigmoid(fpre))
    log_f = jnp.maximum(log_f, LOG_F_FLOOR)
    k = -jnp.expm1(log_f)
    o, S = hgrn2_chunked(q, k, vv, log_f, s0)
    o = rms_norm(o, g_onorm).reshape(b, T, BRANCH).astype(h.dtype)
    y = (o * jax.nn.silu(z)) @ w_out
    return y, S


def setup_inputs(seed: int = 0) -> dict:
    key = jax.random.key(seed)
    ks = jax.random.split(key, 11)
    E = BRANCH
    return {
        "x_prompt": jax.random.normal(ks[0], (BATCH, SEQ, D_MODEL), jnp.float32),
        "x_sample": jax.random.normal(ks[1], (DEC_BATCH, DEC_SEQ, D_MODEL), jnp.float32),
        "state_conv": jax.random.normal(ks[2], (N_CONV_LAYERS, DEC_BATCH, CONV_W - 1, E), jnp.float32),
        "state_hgrn": 0.5 * jax.random.normal(ks[3], (N_HGRN_LAYERS, DEC_BATCH, N_HEADS, HEAD_DK, HEAD_DV), jnp.float32),
        "norm_pre": 1.0 + 0.05 * jax.random.normal(ks[4], (DEPTH, D_MODEL), jnp.float32),
        "w_in": jax.random.normal(ks[5], (DEPTH, D_MODEL, 4 * E), jnp.float32) * D_MODEL ** -0.5,
        "conv_w": jax.random.normal(ks[6], (N_CONV_LAYERS, CONV_W, E), jnp.float32) * CONV_W ** -0.5,
        "hgrn_lb_logits": 0.5 * jax.random.normal(ks[7], (N_HGRN_LAYERS, E), jnp.float32),
        "hgrn_onorm": 1.0 + 0.05 * jax.random.normal(ks[8], (N_HGRN_LAYERS, HEAD_DV), jnp.float32),
        "w_out": jax.random.normal(ks[9], (DEPTH, E, D_MODEL), jnp.float32) * E ** -0.5,
        "norm_post": 1.0 + 0.05 * jax.random.normal(ks[10], (DEPTH, D_MODEL), jnp.float32),
    }


def reference(x_prompt, x_sample, state_conv, state_hgrn, norm_pre, w_in, conv_w,
              hgrn_lb_logits, hgrn_onorm, w_out, norm_post):
    lb_all = jnp.cumsum(jax.nn.softmax(hgrn_lb_logits.astype(jnp.float32), axis=0), axis=0)
    lb_all = lb_all - lb_all[0:1]

    xp, xs = x_prompt, x_sample
    conv_p, conv_s, hgrn_p, hgrn_s = [], [], [], []
    for i in range(DEPTH):
        j = i // N_MIXERS
        hp = rms_norm(xp, norm_pre[i])
        hs = rms_norm(xs, norm_pre[i])
        if i % N_MIXERS == 0:
            buf0 = jnp.zeros((xp.shape[0], CONV_W - 1, BRANCH), xp.dtype)
            yp, bp = conv_branch(hp, w_in[i], conv_w[j], w_out[i], buf0)
            ys, bs = conv_branch(hs, w_in[i], conv_w[j], w_out[i], state_conv[j])
            conv_p.append(bp.astype(xp.dtype))
            conv_s.append(bs.astype(state_conv.dtype))
        else:
            s0 = jnp.zeros((xp.shape[0], N_HEADS, HEAD_DK, HEAD_DV), jnp.float32)
            yp, Sp = hgrn_branch(hp, w_in[i], lb_all[j], hgrn_onorm[j], w_out[i], s0)
            ys, Ss = hgrn_branch(hs, w_in[i], lb_all[j], hgrn_onorm[j], w_out[i],
                                 state_hgrn[j].astype(jnp.float32))
            hgrn_p.append(Sp.astype(xp.dtype))
            hgrn_s.append(Ss.astype(state_hgrn.dtype))
        xp = xp + rms_norm(yp, norm_post[i])
        xs = xs + rms_norm(ys, norm_post[i])

    return (xp, xs, jnp.stack(conv_p), jnp.stack(conv_s), jnp.stack(hgrn_p), jnp.stack(hgrn_s))
```

```python
import functools
import math

import jax
import jax.numpy as jnp
from jax import lax
from jax.experimental import pallas as pl
from jax.experimental.pallas import tpu as pltpu

F32 = jnp.float32
BF16 = jnp.bfloat16

RMS_EPS = 1e-6
LOG_F_FLOOR = -20.0
HEAD_DIM = 128
CONV_W = 3
HGRN_CHUNK = 64
BASE_BLOCK = 8
PROMPT_TILE = 256
SAMPLE_SEQS_PER_STEP = 8
VMEM_LIMIT_BYTES = 56 * 1024 * 1024


def _rms(x, g):
    ms = jnp.mean(x * x, axis=-1, keepdims=True)
    return x * lax.rsqrt(ms + RMS_EPS) * g


def _silu(x):
    return x * (1.0 / (1.0 + jnp.exp(-x)))


def _dot(a, b):
    return jnp.dot(a, b, preferred_element_type=F32)


def _dot_nt(a, b):
    return lax.dot_general(a, b, (((1,), (1,)), ((), ())), preferred_element_type=F32)


def _dot_tn(a, b):
    return lax.dot_general(a, b, (((0,), (0,)), ((), ())), preferred_element_type=F32)


def _lower_bound(logits, j):
    m = jnp.max(logits, axis=0, keepdims=True)
    e = jnp.exp(logits - m)
    p = e / jnp.sum(e, axis=0, keepdims=True)
    if j == 0:
        return jnp.zeros_like(p[0:1])
    return jnp.sum(p[1:j + 1], axis=0, keepdims=True)


def _forget_gate(fpre, lb):
    e = jnp.exp(-jnp.abs(fpre))
    r = 1.0 / (1.0 + e)
    er = e * r
    pos = fpre >= 0
    sig = jnp.where(pos, r, er)
    nsig = jnp.where(pos, er, r)
    f = lb + (1.0 - lb) * sig
    log_f = jnp.maximum(jnp.log(f), LOG_F_FLOOR)
    k = jnp.minimum((1.0 - lb) * nsig, 1.0 - math.exp(LOG_F_FLOOR))
    return log_f, k


def _cumsum_rows(x, group):
    pos = lax.broadcasted_iota(jnp.int32, x.shape, 0) & (group - 1)
    s = 1
    while s < group:
        x = x + jnp.where(pos >= s, pltpu.roll(x, s, axis=0), 0.0)
        s *= 2
    return x


def _group_row(x, group, idx):
    n = x.shape[0]
    parts = []
    for g in range(n // group):
        r = g * group + idx
        parts.append(jnp.broadcast_to(x[r:r + 1, :], (group, x.shape[1])))
    return parts[0] if len(parts) == 1 else jnp.concatenate(parts, axis=0)


def _group_allsum(x, group):
    n = x.shape[0]
    pos = lax.broadcasted_iota(jnp.int32, x.shape, 0) & (group - 1)
    s = 1
    while s < group:
        partner = jnp.where((pos & s) == 0, pltpu.roll(x, n - s, axis=0), pltpu.roll(x, s, axis=0))
        x = x + partner
        s *= 2
    return x


def _level_matrix(n):
    t = lax.broadcasted_iota(jnp.int32, (n, n), 0)
    s = lax.broadcasted_iota(jnp.int32, (n, n), 1)
    sh = BASE_BLOCK.bit_length() - 1
    lv = jnp.where(((t >> sh) == (s >> sh)) & (t >= s), 1, 0)
    c, i = 2 * BASE_BLOCK, 2
    while c <= n:
        half, sh = c // 2, sh + 1
        own = ((t >> sh) == (s >> sh)) & ((t & half) != 0) & ((s & half) == 0)
        lv = jnp.where(own, i, lv)
        c, i = 2 * c, i + 1
    return lv


def _hgrn_chunk(q, k, v, lf, st, lv):
    c = q.shape[0]
    b = _cumsum_rows(lf, c)
    b_last = b[c - 1:c, :]
    qe = q * jnp.exp(b)
    kd = k * jnp.exp(b_last - b)
    decay = jnp.exp(b_last)

    x = b - _group_row(b, BASE_BLOCK, BASE_BLOCK // 2 - 1)
    a = _dot_nt((q * jnp.exp(x)).astype(BF16), (k * jnp.exp(-x)).astype(BF16))
    a = jnp.where(lv == 1, a, 0.0)
    blk, i = 2 * BASE_BLOCK, 2
    while blk <= c:
        e = jnp.exp(-jnp.abs(b - _group_row(b, blk, blk // 2 - 1)))
        a = jnp.where(lv == i, _dot_nt((q * e).astype(BF16), (k * e).astype(BF16)), a)
        blk, i = 2 * blk, i + 1

    vb = v.astype(BF16)
    o = _dot(a.astype(BF16), vb) + _dot_nt(qe.astype(BF16), st.astype(BF16))
    st_new = st * decay + _dot_tn(vb, kd.astype(BF16))
    return o, st_new


def _head_norm_gate(o, gate, gon):
    parts = []
    for hd in range(o.shape[1] // HEAD_DIM):
        oh = o[:, hd * HEAD_DIM:(hd + 1) * HEAD_DIM]
        parts.append(_rms(oh, gon))
    return jnp.concatenate(parts, axis=1) * gate


def _conv_mix(h, win_ref, cw_ref, prev1, prev2, pos):
    e = win_ref.shape[1] // 4
    v = _dot(h, win_ref[:, 0:e])
    cg = _dot(h, win_ref[:, 2 * e:3 * e])
    u = cg * v
    u1 = jnp.where(pos >= 1, pltpu.roll(u, 1, axis=0), prev1)
    u2 = jnp.where(pos >= 2, pltpu.roll(u, 2, axis=0), prev2)
    conv = cw_ref[0:1, :] * u2 + cw_ref[1:2, :] * u1 + cw_ref[2:3, :] * u
    bg = _dot(h, win_ref[:, e:2 * e])
    z = _dot(h, win_ref[:, 3 * e:4 * e])
    return bg * conv * _silu(z), u


def _conv_prompt_kernel(x_ref, gpre_ref, win_ref, cw_ref, wout_ref, gpost_ref,
                        y_ref, buf_ref, tail_ref):
    @pl.when(pl.program_id(1) == 0)
    def _():
        tail_ref[...] = jnp.zeros_like(tail_ref)

    x = x_ref[0]
    tm = x.shape[0]
    h = _rms(x, gpre_ref[...]).astype(BF16)
    pos = lax.broadcasted_iota(jnp.int32, (tm, win_ref.shape[1] // 4), 0)
    tail = tail_ref[...]
    prev1 = tail[7:8, :]
    prev2 = jnp.where(pos == 0, tail[6:7, :], tail[7:8, :])
    g, u = _conv_mix(h, win_ref, cw_ref, prev1, prev2, pos)
    y = _dot(g.astype(BF16), wout_ref[...])
    y_ref[0] = x + _rms(y, gpost_ref[...])
    tail_ref[...] = u[tm - 8:tm, :]
    buf_ref[0] = u[tm - (CONV_W - 1):tm, :]


def _conv_sample_kernel(seq, x_ref, p1_ref, p2_ref, gpre_ref, win_ref, cw_ref, wout_ref, gpost_ref,
                        y_ref, u_ref):
    x = x_ref[...]
    h = _rms(x, gpre_ref[...]).astype(BF16)
    pos = lax.broadcasted_iota(jnp.int32, p1_ref.shape, 0) & (seq - 1)
    g, u = _conv_mix(h, win_ref, cw_ref, p1_ref[...], p2_ref[...], pos)
    y = _dot(g.astype(BF16), wout_ref[...])
    y_ref[...] = x + _rms(y, gpost_ref[...])
    u_ref[...] = u


def _hgrn_prompt_kernel(j, x_ref, gpre_ref, win_ref, lbl_ref, gon_ref, wout_ref, gpost_ref,
                        y_ref, sout_ref, st_s, q_s, k_s, lf_s, v_s, o_s):
    t_idx = pl.program_id(1)
    e = win_ref.shape[1] // 4
    heads = e // HEAD_DIM

    @pl.when(t_idx == 0)
    def _():
        st_s[...] = jnp.zeros_like(st_s)

    x = x_ref[0]
    tm = x.shape[0]
    h = _rms(x, gpre_ref[...]).astype(BF16)
    lb = _lower_bound(lbl_ref[...], j)
    q_s[...] = _silu(_dot(h, win_ref[:, 0:e])) * (HEAD_DIM ** -0.5)
    lf, k = _forget_gate(_dot(h, win_ref[:, e:2 * e]), lb)
    lf_s[...] = lf
    k_s[...] = k
    v_s[...] = _dot(h, win_ref[:, 2 * e:3 * e])

    lv = _level_matrix(HGRN_CHUNK)

    def chunk_body(c, carry):
        rows = pl.ds(pl.multiple_of(c * HGRN_CHUNK, HGRN_CHUNK), HGRN_CHUNK)
        for hd in range(heads):
            cols = slice(hd * HEAD_DIM, (hd + 1) * HEAD_DIM)
            o, st_new = _hgrn_chunk(q_s[rows, cols], k_s[rows, cols], v_s[rows, cols],
                                    lf_s[rows, cols], st_s[hd], lv)
            o_s[rows, cols] = o
            st_s[hd] = st_new
        return carry

    lax.fori_loop(0, tm // HGRN_CHUNK, chunk_body, 0)

    gate = _silu(_dot(h, win_ref[:, 3 * e:4 * e]))
    g = _head_norm_gate(o_s[...], gate, gon_ref[...])
    y = _dot(g.astype(BF16), wout_ref[...])
    y_ref[0] = x + _rms(y, gpost_ref[...])

    @pl.when(t_idx == pl.num_programs(1) - 1)
    def _():
        for hd in range(heads):
            sout_ref[0, hd] = st_s[hd].T


def _hgrn_sample_pre_kernel(j, seq, x_ref, gpre_ref, win_ref, lbl_ref,
                            qe_ref, kd_ref, decp_ref, v_ref, gate_ref, oi_ref):
    e = win_ref.shape[1] // 4
    heads = e // HEAD_DIM
    x = x_ref[...]
    n = x.shape[0]
    h = _rms(x, gpre_ref[...]).astype(BF16)
    lb = _lower_bound(lbl_ref[...], j)
    gate_ref[...] = _silu(_dot(h, win_ref[:, 3 * e:4 * e]))

    t = lax.broadcasted_iota(jnp.int32, (n, n), 0)
    s = lax.broadcasted_iota(jnp.int32, (n, n), 1)
    sh = seq.bit_length() - 1
    same_seq_causal = ((t >> sh) == (s >> sh)) & (t >= s)
    row = lax.broadcasted_iota(jnp.int32, (n, HEAD_DIM), 0)
    pos = row & (seq - 1)

    for hd in range(heads):
        cols = slice(hd * HEAD_DIM, (hd + 1) * HEAD_DIM)
        q = _silu(_dot(h, win_ref[:, cols])) * (HEAD_DIM ** -0.5)
        lf, k = _forget_gate(_dot(h, win_ref[:, e + hd * HEAD_DIM:e + (hd + 1) * HEAD_DIM]), lb[:, cols])
        v = _dot(h, win_ref[:, 2 * e + hd * HEAD_DIM:2 * e + (hd + 1) * HEAD_DIM])
        b = _cumsum_rows(lf, seq)
        b_last = _group_allsum(lf, seq)
        x_mid = b - _group_allsum(jnp.where(pos < seq // 2, lf, 0.0), seq)
        a = _dot_nt((q * jnp.exp(x_mid)).astype(BF16), (k * jnp.exp(-x_mid)).astype(BF16))
        a = jnp.where(same_seq_causal, a, 0.0)
        vb = v.astype(BF16)
        oi_ref[:, cols] = _dot(a.astype(BF16), vb)
        qe_ref[:, cols] = q * jnp.exp(b)
        kd_ref[:, cols] = k * jnp.exp(b_last - b)
        v_ref[:, cols] = v
        dec = jnp.exp(b_last)
        dsw = jnp.where((row & 7) < 4, pltpu.roll(dec, n - 4, axis=0), pltpu.roll(dec, 4, axis=0))
        hi = dsw.astype(BF16).astype(F32)
        mid = (dsw - hi).astype(BF16).astype(F32)
        lo = (dsw - hi - mid).astype(BF16).astype(F32)
        decp_ref[:, cols] = jnp.where(pos == 0, hi, jnp.where(pos == 1, mid, jnp.where(pos == 2, lo, 0.0)))


def _hgrn_sample_rec_kernel(seq, qe_ref, kd_ref, decp_ref, v_ref, oi_ref, s0_ref, o_ref, s1_ref):
    nseq = s0_ref.shape[0]
    heads = s0_ref.shape[1]
    row = lax.broadcasted_iota(jnp.int32, (8, HEAD_DIM), 0)
    pad = jnp.zeros((HEAD_DIM - 8, HEAD_DIM), F32)
    pad2 = jnp.zeros((HEAD_DIM - 8, 2 * HEAD_DIM), F32)

    def pair_body(p, carry):
        rows = pl.ds(pl.multiple_of(p * 8, 8), 8)
        for hd in range(heads):
            cols = slice(hd * HEAD_DIM, (hd + 1) * HEAD_DIM)
            qe8 = qe_ref[rows, cols].astype(BF16)
            kd8 = kd_ref[rows, cols]
            dp8 = decp_ref[rows, cols]
            v8 = v_ref[rows, cols]
            o8 = oi_ref[rows, cols]
            for half in range(2):
                mine = (row >= 4) if half else (row < 4)
                s0 = s0_ref[2 * p + half, hd]
                o_inter = _dot(qe8, s0.astype(BF16))
                o8 = o8 + jnp.where(mine, o_inter, 0.0)
                lhs = jnp.concatenate([jnp.where(mine, kd8, dp8), pad], axis=0).astype(BF16)
                ones = jnp.where(mine, 0.0, 1.0)
                rhs = jnp.concatenate(
                    [jnp.concatenate([jnp.where(mine, v8, 0.0), ones], axis=1), pad2], axis=0).astype(BF16)
                upd = _dot_tn(lhs, rhs)
                s1_ref[2 * p + half, hd] = upd[:, HEAD_DIM:] * s0 + upd[:, :HEAD_DIM]
            o_ref[rows, cols] = o8
        return carry

    lax.fori_loop(0, nseq // 2, pair_body, 0)


def _hgrn_sample_post_kernel(x_ref, o_ref, gate_ref, gon_ref, wout_ref, gpost_ref, y_ref):
    x = x_ref[...]
    g = _head_norm_gate(o_ref[...], gate_ref[...], gon_ref[...])
    y = _dot(g.astype(BF16), wout_ref[...])
    y_ref[...] = x + _rms(y, gpost_ref[...])


def _const_spec(shape):
    nd = len(shape)
    return pl.BlockSpec(shape, lambda *_: (0,) * nd)


def _params(sem):
    return pltpu.CompilerParams(dimension_semantics=sem, vmem_limit_bytes=VMEM_LIMIT_BYTES)


def _conv_prompt(x, gpre, win, cw, wout, gpost):
    bsz, t, d = x.shape
    e = wout.shape[0]
    tm = PROMPT_TILE
    assert t % tm == 0 and tm % 8 == 0
    return pl.pallas_call(
        _conv_prompt_kernel,
        grid=(bsz, t // tm),
        in_specs=[pl.BlockSpec((1, tm, d), lambda b, i: (b, i, 0)),
                  _const_spec(gpre.shape), _const_spec(win.shape), _const_spec(cw.shape),
                  _const_spec(wout.shape), _const_spec(gpost.shape)],
        out_specs=[pl.BlockSpec((1, tm, d), lambda b, i: (b, i, 0)),
                   pl.BlockSpec((1, CONV_W - 1, e), lambda b, i: (b, 0, 0))],
        out_shape=[jax.ShapeDtypeStruct(x.shape, F32),
                   jax.ShapeDtypeStruct((bsz, CONV_W - 1, e), F32)],
        scratch_shapes=[pltpu.VMEM((8, e), F32)],
        compiler_params=_params(("arbitrary", "arbitrary")),
        name="conv_prompt",
    )(x, gpre, win, cw, wout, gpost)


def _conv_sample(x, buf, gpre, win, cw, wout, gpost):
    nb, seq, d = x.shape
    e = wout.shape[0]
    assert seq & (seq - 1) == 0 and seq >= CONV_W - 1
    n = nb * seq
    xf = x.reshape(n, d)
    zeros = lambda r: jnp.zeros((nb, r, e), F32)
    p1 = jnp.concatenate([buf[:, 1:2], zeros(seq - 1)], axis=1).reshape(n, e)
    p2 = jnp.concatenate([buf[:, 0:2], zeros(seq - 2)], axis=1).reshape(n, e)
    args = (xf, p1, p2, gpre, win, cw, wout, gpost)
    y, u = pl.pallas_call(
        functools.partial(_conv_sample_kernel, seq),
        grid=(1,),
        in_specs=[_const_spec(a.shape) for a in args],
        out_specs=[_const_spec((n, d)), _const_spec((n, e))],
        out_shape=[jax.ShapeDtypeStruct((n, d), F32), jax.ShapeDtypeStruct((n, e), F32)],
        compiler_params=_params(("arbitrary",)),
        name="conv_sample",
    )(*args)
    return y.reshape(nb, seq, d), u.reshape(nb, seq, e)[:, seq - (CONV_W - 1):]


def _hgrn_prompt(j, x, gpre, win, lb_logits, gon, wout, gpost):
    bsz, t, d = x.shape
    e = wout.shape[0]
    heads = e // HEAD_DIM
    tm = PROMPT_TILE
    assert t % tm == 0 and tm % HGRN_CHUNK == 0
    return pl.pallas_call(
        functools.partial(_hgrn_prompt_kernel, j),
        grid=(bsz, t // tm),
        in_specs=[pl.BlockSpec((1, tm, d), lambda b, i: (b, i, 0)),
                  _const_spec(gpre.shape), _const_spec(win.shape), _const_spec(lb_logits.shape),
                  _const_spec(gon.shape), _const_spec(wout.shape), _const_spec(gpost.shape)],
        out_specs=[pl.BlockSpec((1, tm, d), lambda b, i: (b, i, 0)),
                   pl.BlockSpec((1, heads, HEAD_DIM, HEAD_DIM), lambda b, i: (b, 0, 0, 0))],
        out_shape=[jax.ShapeDtypeStruct(x.shape, F32),
                   jax.ShapeDtypeStruct((bsz, heads, HEAD_DIM, HEAD_DIM), F32)],
        scratch_shapes=[pltpu.VMEM((heads, HEAD_DIM, HEAD_DIM), F32)]
                       + [pltpu.VMEM((tm, e), F32) for _ in range(5)],
        compiler_params=_params(("arbitrary", "arbitrary")),
        name="hgrn_prompt",
    )(x, gpre, win, lb_logits, gon, wout, gpost)


def _hgrn_sample(j, x, s0, gpre, win, lb_logits, gon, wout, gpost):
    nb, seq, d = x.shape
    e = wout.shape[0]
    heads = e // HEAD_DIM
    n = nb * seq
    g = SAMPLE_SEQS_PER_STEP
    assert seq == 4 and nb % g == 0 and g % 2 == 0
    xf = x.reshape(n, d)
    act = jax.ShapeDtypeStruct((n, e), F32)
    pre_args = (xf, gpre, win, lb_logits)
    qe, kd, decp, v, gate, oi = pl.pallas_call(
        functools.partial(_hgrn_sample_pre_kernel, j, seq),
        grid=(1,),
        in_specs=[_const_spec(a.shape) for a in pre_args],
        out_specs=[_const_spec((n, e))] * 6,
        out_shape=[act] * 6,
        compiler_params=_params(("arbitrary",)),
        name="hgrn_sample_pre",
    )(*pre_args)

    row_spec = pl.BlockSpec((g * seq, e), lambda i: (i, 0))
    st_spec = pl.BlockSpec((g, heads, HEAD_DIM, HEAD_DIM), lambda i: (i, 0, 0, 0))
    o, s1 = pl.pallas_call(
        functools.partial(_hgrn_sample_rec_kernel, seq),
        grid=(nb // g,),
        in_specs=[row_spec] * 5 + [st_spec],
        out_specs=[row_spec, st_spec],
        out_shape=[act, jax.ShapeDtypeStruct(s0.shape, F32)],
        compiler_params=_params(("arbitrary",)),
        name="hgrn_sample_rec",
    )(qe, kd, decp, v, oi, s0)

    post_args = (xf, o, gate, gon, wout, gpost)
    y = pl.pallas_call(
        _hgrn_sample_post_kernel,
        grid=(1,),
        in_specs=[_const_spec(a.shape) for a in post_args],
        out_specs=_const_spec((n, d)),
        out_shape=jax.ShapeDtypeStruct((n, d), F32),
        compiler_params=_params(("arbitrary",)),
        name="hgrn_sample_post",
    )(*post_args)
    return y.reshape(nb, seq, d), s1


def kernel(x_prompt, x_sample, state_conv, state_hgrn, norm_pre, w_in, conv_w, hgrn_lb_logits,
           hgrn_onorm, w_out, norm_post):
    depth = w_in.shape[0]
    win_b = w_in.astype(BF16)
    wout_b = w_out.astype(BF16)
    lb_logits = hgrn_lb_logits.astype(F32)
    xp, xs = x_prompt, x_sample
    conv_p, conv_s, hgrn_p, hgrn_s = [], [], [], []
    for i in range(depth):
        j = i // 2
        gpre, gpost = norm_pre[i:i + 1], norm_post[i:i + 1]
        if i % 2 == 0:
            xp, bp = _conv_prompt(xp, gpre, win_b[i], conv_w[j], wout_b[i], gpost)
            xs, bs = _conv_sample(xs, state_conv[j], gpre, win_b[i], conv_w[j], wout_b[i], gpost)
            conv_p.append(bp)
            conv_s.append(bs)
        else:
            gon = hgrn_onorm[j:j + 1]
            xp, sp = _hgrn_prompt(j, xp, gpre, win_b[i], lb_logits, gon, wout_b[i], gpost)
            xs, ss = _hgrn_sample(j, xs, state_hgrn[j], gpre, win_b[i], lb_logits, gon, wout_b[i], gpost)
            hgrn_p.append(sp)
            hgrn_s.append(ss)
    return (xp, xs, jnp.stack(conv_p), jnp.stack(conv_s), jnp.stack(hgrn_p), jnp.stack(hgrn_s))
```

```python
import functools
import math

import jax
import jax.numpy as jnp
from jax import lax
from jax.experimental import pallas as pl
from jax.experimental.pallas import tpu as pltpu

F32 = jnp.float32
BF16 = jnp.bfloat16

RMS_EPS = 1e-6
LOG_F_FLOOR = -20.0
LOG2_E = math.log2(math.e)
HEAD_DIM = 128
CONV_W = 3
HGRN_CHUNK = 128
BASE_BLOCK = 8
PROMPT_TILE = 256
SAMPLE_SEQS_PER_STEP = 8
VMEM_LIMIT_BYTES = 56 * 1024 * 1024


def _rms(x, g):
    ms = jnp.mean(x * x, axis=-1, keepdims=True)
    return x * lax.rsqrt(ms + RMS_EPS) * g


def _silu(x):
    return x * (1.0 / (1.0 + jnp.exp(-x)))


def _dot(a, b):
    return jnp.dot(a, b, preferred_element_type=F32)


def _dot_nt(a, b):
    return lax.dot_general(a, b, (((1,), (1,)), ((), ())), preferred_element_type=F32)


def _dot_tn(a, b):
    return lax.dot_general(a, b, (((0,), (0,)), ((), ())), preferred_element_type=F32)


def _lower_bound(logits, j):
    m = jnp.max(logits, axis=0, keepdims=True)
    e = jnp.exp(logits - m)
    p = e / jnp.sum(e, axis=0, keepdims=True)
    if j == 0:
        return jnp.zeros_like(p[0:1])
    return jnp.sum(p[1:j + 1], axis=0, keepdims=True)


def _forget_gate(fpre, lb):
    e = jnp.exp(-jnp.abs(fpre))
    r = 1.0 / (1.0 + e)
    er = e * r
    pos = fpre >= 0
    sig = jnp.where(pos, r, er)
    nsig = jnp.where(pos, er, r)
    f = lb + (1.0 - lb) * sig
    log2_f = jnp.maximum(jnp.log(f) * LOG2_E, LOG_F_FLOOR * LOG2_E)
    k = jnp.minimum((1.0 - lb) * nsig, 1.0 - math.exp(LOG_F_FLOOR))
    return log2_f, k


def _cumsum_rows(x, group):
    pos = lax.broadcasted_iota(jnp.int32, x.shape, 0) & (group - 1)
    s = 1
    while s < group:
        x = x + jnp.where(pos >= s, pltpu.roll(x, s, axis=0), 0.0)
        s *= 2
    return x


def _group_allsum(x, group):
    n = x.shape[0]
    pos = lax.broadcasted_iota(jnp.int32, x.shape, 0) & (group - 1)
    s = 1
    while s < group:
        partner = jnp.where((pos & s) == 0, pltpu.roll(x, n - s, axis=0), pltpu.roll(x, s, axis=0))
        x = x + partner
        s *= 2
    return x


def _level_matrix(n):
    t = lax.broadcasted_iota(jnp.int32, (n, n), 0)
    s = lax.broadcasted_iota(jnp.int32, (n, n), 1)
    sh = BASE_BLOCK.bit_length() - 1
    lv = jnp.where(((t >> sh) == (s >> sh)) & (t >= s), 1, 0)
    c, i = 2 * BASE_BLOCK, 2
    while c <= n:
        half, sh = c // 2, sh + 1
        own = ((t >> sh) == (s >> sh)) & ((t & half) != 0) & ((s & half) == 0)
        lv = jnp.where(own, i, lv)
        c, i = 2 * c, i + 1
    return lv


def _level_sizes(chunk):
    sizes, c = [], 2 * BASE_BLOCK
    while c <= chunk:
        sizes.append(c)
        c *= 2
    return sizes


def _chunk_operands(q, k, lf2):
    c = q.shape[0]
    local = _cumsum_rows(lf2, BASE_BLOCK)
    b_parts, x_parts, carry = [], [], None
    for g in range(c // BASE_BLOCK):
        blk = local[g * BASE_BLOCK:(g + 1) * BASE_BLOCK]
        x_parts.append(blk - blk[BASE_BLOCK // 2 - 1:BASE_BLOCK // 2])
        if carry is not None:
            blk = blk + carry
        b_parts.append(blk)
        carry = blk[BASE_BLOCK - 1:BASE_BLOCK]
    b = jnp.concatenate(b_parts, axis=0)
    x = jnp.concatenate(x_parts, axis=0)
    b_last = carry
    qe = (q * jnp.exp2(b)).astype(BF16)
    kd = (k * jnp.exp2(b_last - b)).astype(BF16)
    q8 = (q * jnp.exp2(x)).astype(BF16)
    k8 = (k * jnp.exp2(-x)).astype(BF16)
    w = []
    for size in _level_sizes(c):
        half = size // 2
        parts = []
        for s0 in range(0, c, size):
            r = b[s0 + half - 1:s0 + half]
            lo, hi = slice(s0, s0 + half), slice(s0 + half, s0 + size)
            parts.append(k[lo] * jnp.exp2(r - b[lo]))
            parts.append(q[hi] * jnp.exp2(b[hi] - r))
        w.append(jnp.concatenate(parts, axis=0).astype(BF16))
    return qe, kd, q8, k8, w, jnp.exp2(b_last)


def _head_norm_gate(o, gate, gon):
    parts = []
    for hd in range(o.shape[1] // HEAD_DIM):
        oh = o[:, hd * HEAD_DIM:(hd + 1) * HEAD_DIM]
        parts.append(_rms(oh, gon))
    return jnp.concatenate(parts, axis=1) * gate


def _conv_mix(h, win_ref, cw_ref, prev1, prev2, pos):
    e = win_ref.shape[1] // 4
    v = _dot(h, win_ref[:, 0:e])
    cg = _dot(h, win_ref[:, 2 * e:3 * e])
    u = cg * v
    u1 = jnp.where(pos >= 1, pltpu.roll(u, 1, axis=0), prev1)
    u2 = jnp.where(pos >= 2, pltpu.roll(u, 2, axis=0), prev2)
    conv = cw_ref[0:1, :] * u2 + cw_ref[1:2, :] * u1 + cw_ref[2:3, :] * u
    bg = _dot(h, win_ref[:, e:2 * e])
    z = _dot(h, win_ref[:, 3 * e:4 * e])
    return bg * conv * _silu(z), u


def _conv_prompt_kernel(x_ref, gpre_ref, win_ref, cw_ref, wout_ref, gpost_ref,
                        y_ref, buf_ref, tail_ref):
    @pl.when(pl.program_id(1) == 0)
    def _():
        tail_ref[...] = jnp.zeros_like(tail_ref)

    x = x_ref[0]
    tm = x.shape[0]
    h = _rms(x, gpre_ref[...]).astype(BF16)
    pos = lax.broadcasted_iota(jnp.int32, (tm, win_ref.shape[1] // 4), 0)
    tail = tail_ref[...]
    prev1 = tail[7:8, :]
    prev2 = jnp.where(pos == 0, tail[6:7, :], tail[7:8, :])
    g, u = _conv_mix(h, win_ref, cw_ref, prev1, prev2, pos)
    y = _dot(g.astype(BF16), wout_ref[...])
    y_ref[0] = x + _rms(y, gpost_ref[...])
    tail_ref[...] = u[tm - 8:tm, :]
    buf_ref[0] = u[tm - (CONV_W - 1):tm, :]


def _conv_sample_kernel(seq, x_ref, p1_ref, p2_ref, gpre_ref, win_ref, cw_ref, wout_ref, gpost_ref,
                        y_ref, u_ref):
    x = x_ref[...]
    h = _rms(x, gpre_ref[...]).astype(BF16)
    pos = lax.broadcasted_iota(jnp.int32, p1_ref.shape, 0) & (seq - 1)
    g, u = _conv_mix(h, win_ref, cw_ref, p1_ref[...], p2_ref[...], pos)
    y = _dot(g.astype(BF16), wout_ref[...])
    y_ref[...] = x + _rms(y, gpost_ref[...])
    u_ref[...] = u


def _hgrn_prompt_kernel(j, x_ref, gpre_ref, win_ref, lbl_ref, gon_ref, wout_ref, gpost_ref,
                        y_ref, sout_ref, st_s, dec_s, qe_s, kd_s, q8_s, k8_s, v_s, o_s, *w_s):
    t_idx = pl.program_id(1)
    e = win_ref.shape[1] // 4
    heads = e // HEAD_DIM

    @pl.when(t_idx == 0)
    def _():
        st_s[...] = jnp.zeros_like(st_s)

    x = x_ref[0]
    tm = x.shape[0]
    h = _rms(x, gpre_ref[...]).astype(BF16)
    lb = _lower_bound(lbl_ref[...], j)
    q = _silu(_dot(h, win_ref[:, 0:e])) * (HEAD_DIM ** -0.5)
    lf2, k = _forget_gate(_dot(h, win_ref[:, e:2 * e]), lb)
    v_s[...] = _dot(h, win_ref[:, 2 * e:3 * e]).astype(BF16)
    for c in range(tm // HGRN_CHUNK):
        rows = slice(c * HGRN_CHUNK, (c + 1) * HGRN_CHUNK)
        qe, kd, q8, k8, w, decay = _chunk_operands(q[rows], k[rows], lf2[rows])
        qe_s[rows, :] = qe
        kd_s[rows, :] = kd
        q8_s[rows, :] = q8
        k8_s[rows, :] = k8
        for w_ref, w_val in zip(w_s, w):
            w_ref[rows, :] = w_val
        dec_s[c] = jnp.broadcast_to(decay, dec_s.shape[1:])

    lv = _level_matrix(HGRN_CHUNK)

    def chunk_body(c, carry):
        rows = pl.ds(pl.multiple_of(c * HGRN_CHUNK, HGRN_CHUNK), HGRN_CHUNK)
        for hd in range(heads):
            cols = slice(hd * HEAD_DIM, (hd + 1) * HEAD_DIM)
            a = jnp.where(lv == 1, _dot_nt(q8_s[rows, cols], k8_s[rows, cols]), 0.0)
            for i, w_ref in enumerate(w_s):
                wv = w_ref[rows, cols]
                a = jnp.where(lv == i + 2, _dot_nt(wv, wv), a)
            vb = v_s[rows, cols]
            st = st_s[hd]
            o_s[rows, cols] = _dot(a.astype(BF16), vb) + _dot_nt(qe_s[rows, cols], st.astype(BF16))
            st3 = st.reshape(HEAD_DIM // 8, 8, HEAD_DIM) * dec_s[c, :, cols]
            st_s[hd] = st3.reshape(HEAD_DIM, HEAD_DIM) + _dot_tn(vb, kd_s[rows, cols])
        return carry

    lax.fori_loop(0, tm // HGRN_CHUNK, chunk_body, 0)

    gate = _silu(_dot(h, win_ref[:, 3 * e:4 * e]))
    g = _head_norm_gate(o_s[...], gate, gon_ref[...])
    y = _dot(g.astype(BF16), wout_ref[...])
    y_ref[0] = x + _rms(y, gpost_ref[...])

    @pl.when(t_idx == pl.num_programs(1) - 1)
    def _():
        for hd in range(heads):
            sout_ref[0, hd] = st_s[hd].T


def _hgrn_sample_pre_kernel(j, seq, x_ref, gpre_ref, win_ref, lbl_ref,
                            qe_ref, kd_ref, decp_ref, v_ref, gate_ref, oi_ref):
    e = win_ref.shape[1] // 4
    heads = e // HEAD_DIM
    x = x_ref[...]
    n = x.shape[0]
    h = _rms(x, gpre_ref[...]).astype(BF16)
    lb = _lower_bound(lbl_ref[...], j)
    gate_ref[...] = _silu(_dot(h, win_ref[:, 3 * e:4 * e]))

    t = lax.broadcasted_iota(jnp.int32, (n, n), 0)
    s = lax.broadcasted_iota(jnp.int32, (n, n), 1)
    sh = seq.bit_length() - 1
    same_seq_causal = ((t >> sh) == (s >> sh)) & (t >= s)
    row = lax.broadcasted_iota(jnp.int32, (n, HEAD_DIM), 0)
    pos = row & (seq - 1)

    for hd in range(heads):
        cols = slice(hd * HEAD_DIM, (hd + 1) * HEAD_DIM)
        q = _silu(_dot(h, win_ref[:, cols])) * (HEAD_DIM ** -0.5)
        lf2, k = _forget_gate(_dot(h, win_ref[:, e + hd * HEAD_DIM:e + (hd + 1) * HEAD_DIM]), lb[:, cols])
        v = _dot(h, win_ref[:, 2 * e + hd * HEAD_DIM:2 * e + (hd + 1) * HEAD_DIM])
        b = _cumsum_rows(lf2, seq)
        b_last = _group_allsum(lf2, seq)
        x_mid = b - _group_allsum(jnp.where(pos < seq // 2, lf2, 0.0), seq)
        a = _dot_nt((q * jnp.exp2(x_mid)).astype(BF16), (k * jnp.exp2(-x_mid)).astype(BF16))
        a = jnp.where(same_seq_causal, a, 0.0)
        vb = v.astype(BF16)
        oi_ref[:, cols] = _dot(a.astype(BF16), vb)
        qe_ref[:, cols] = q * jnp.exp2(b)
        kd_ref[:, cols] = k * jnp.exp2(b_last - b)
        v_ref[:, cols] = v
        dec = jnp.exp2(b_last)
        dsw = jnp.where((row & 7) < 4, pltpu.roll(dec, n - 4, axis=0), pltpu.roll(dec, 4, axis=0))
        hi = dsw.astype(BF16).astype(F32)
        mid = (dsw - hi).astype(BF16).astype(F32)
        lo = (dsw - hi - mid).astype(BF16).astype(F32)
        decp_ref[:, cols] = jnp.where(pos == 0, hi, jnp.where(pos == 1, mid, jnp.where(pos == 2, lo, 0.0)))


def _hgrn_sample_rec_kernel(qe_ref, kd_ref, decp_ref, v_ref, oi_ref, s0_ref, *rest):
    o_ref, s1_ref = rest[-2:]
    nseq = s0_ref.shape[0]
    heads = s0_ref.shape[1]
    row = lax.broadcasted_iota(jnp.int32, (8, HEAD_DIM), 0)
    pad = jnp.zeros((HEAD_DIM - 8, HEAD_DIM), F32)
    pad2 = jnp.zeros((HEAD_DIM - 8, 2 * HEAD_DIM), F32)

    def pair_body(p, carry):
        rows = pl.ds(pl.multiple_of(p * 8, 8), 8)
        for hd in range(heads):
            cols = slice(hd * HEAD_DIM, (hd + 1) * HEAD_DIM)
            qe8 = qe_ref[rows, cols].astype(BF16)
            kd8 = kd_ref[rows, cols]
            dp8 = decp_ref[rows, cols]
            v8 = v_ref[rows, cols]
            o8 = oi_ref[rows, cols]
            for half in range(2):
                mine = (row >= 4) if half else (row < 4)
                s0 = s0_ref[2 * p + half, hd]
                o_inter = _dot(qe8, s0.astype(BF16))
                o8 = o8 + jnp.where(mine, o_inter, 0.0)
                lhs = jnp.concatenate([jnp.where(mine, kd8, dp8), pad], axis=0).astype(BF16)
                ones = jnp.where(mine, 0.0, 1.0)
                rhs = jnp.concatenate(
                    [jnp.concatenate([jnp.where(mine, v8, 0.0), ones], axis=1), pad2], axis=0).astype(BF16)
                upd = _dot_tn(lhs, rhs)
                s1_ref[2 * p + half, hd] = upd[:, HEAD_DIM:] * s0 + upd[:, :HEAD_DIM]
            o_ref[rows, cols] = o8
        return carry

    lax.fori_loop(0, nseq // 2, pair_body, 0)


def _hgrn_sample_post_kernel(x_ref, o_ref, gate_ref, gon_ref, wout_ref, gpost_ref, y_ref):
    x = x_ref[...]
    g = _head_norm_gate(o_ref[...], gate_ref[...], gon_ref[...])
    y = _dot(g.astype(BF16), wout_ref[...])
    y_ref[...] = x + _rms(y, gpost_ref[...])


def _const_spec(shape):
    nd = len(shape)
    return pl.BlockSpec(shape, lambda *_: (0,) * nd)


def _layer_spec(arr, layer):
    nd = arr.ndim - 1
    return pl.BlockSpec((None,) + arr.shape[1:], lambda *_: (layer,) + (0,) * nd)


def _params(sem):
    return pltpu.CompilerParams(dimension_semantics=sem, vmem_limit_bytes=VMEM_LIMIT_BYTES)


def _conv_prompt(i, j, x, gpre, win, cw, wout, gpost):
    bsz, t, d = x.shape
    e = wout.shape[1]
    tm = PROMPT_TILE
    assert t % tm == 0 and tm % 8 == 0
    return pl.pallas_call(
        _conv_prompt_kernel,
        grid=(bsz, t // tm),
        in_specs=[pl.BlockSpec((1, tm, d), lambda b, s: (b, s, 0)),
                  _layer_spec(gpre, i), _layer_spec(win, i), _layer_spec(cw, j),
                  _layer_spec(wout, i), _layer_spec(gpost, i)],
        out_specs=[pl.BlockSpec((1, tm, d), lambda b, s: (b, s, 0)),
                   pl.BlockSpec((1, CONV_W - 1, e), lambda b, s: (b, 0, 0))],
        out_shape=[jax.ShapeDtypeStruct(x.shape, F32),
                   jax.ShapeDtypeStruct((bsz, CONV_W - 1, e), F32)],
        scratch_shapes=[pltpu.VMEM((8, e), F32)],
        compiler_params=_params(("arbitrary", "arbitrary")),
        name="conv_prompt",
    )(x, gpre, win, cw, wout, gpost)


def _conv_sample(i, j, x, buf, gpre, win, cw, wout, gpost):
    nb, seq, d = x.shape
    e = wout.shape[1]
    assert seq & (seq - 1) == 0 and seq >= CONV_W - 1
    n = nb * seq
    xf = x.reshape(n, d)
    zeros = lambda r: jnp.zeros((nb, r, e), F32)
    p1 = jnp.concatenate([buf[:, 1:2], zeros(seq - 1)], axis=1).reshape(n, e)
    p2 = jnp.concatenate([buf[:, 0:2], zeros(seq - 2)], axis=1).reshape(n, e)
    y, u = pl.pallas_call(
        functools.partial(_conv_sample_kernel, seq),
        grid=(1,),
        in_specs=[_const_spec(xf.shape), _const_spec(p1.shape), _const_spec(p2.shape),
                  _layer_spec(gpre, i), _layer_spec(win, i), _layer_spec(cw, j),
                  _layer_spec(wout, i), _layer_spec(gpost, i)],
        out_specs=[_const_spec((n, d)), _const_spec((n, e))],
        out_shape=[jax.ShapeDtypeStruct((n, d), F32), jax.ShapeDtypeStruct((n, e), F32)],
        compiler_params=_params(("arbitrary",)),
        name="conv_sample",
    )(xf, p1, p2, gpre, win, cw, wout, gpost)
    return y.reshape(nb, seq, d), u.reshape(nb, seq, e)[:, seq - (CONV_W - 1):]


def _hgrn_prompt(i, j, x, gpre, win, lb_logits, gon, wout, gpost):
    bsz, t, d = x.shape
    e = wout.shape[1]
    heads = e // HEAD_DIM
    tm = PROMPT_TILE
    assert t % tm == 0 and tm % HGRN_CHUNK == 0
    n_bf16 = 5 + len(_level_sizes(HGRN_CHUNK))
    scratch = ([pltpu.VMEM((heads, HEAD_DIM, HEAD_DIM), F32),
                pltpu.VMEM((tm // HGRN_CHUNK, 8, e), F32)]
               + [pltpu.VMEM((tm, e), BF16) for _ in range(5)]
               + [pltpu.VMEM((tm, e), F32)]
               + [pltpu.VMEM((tm, e), BF16) for _ in range(n_bf16 - 5)])
    return pl.pallas_call(
        functools.partial(_hgrn_prompt_kernel, j),
        grid=(bsz, t // tm),
        in_specs=[pl.BlockSpec((1, tm, d), lambda b, s: (b, s, 0)),
                  _layer_spec(gpre, i), _layer_spec(win, i), _const_spec(lb_logits.shape),
                  _layer_spec(gon, j), _layer_spec(wout, i), _layer_spec(gpost, i)],
        out_specs=[pl.BlockSpec((1, tm, d), lambda b, s: (b, s, 0)),
                   pl.BlockSpec((1, heads, HEAD_DIM, HEAD_DIM), lambda b, s: (b, 0, 0, 0))],
        out_shape=[jax.ShapeDtypeStruct(x.shape, F32),
                   jax.ShapeDtypeStruct((bsz, heads, HEAD_DIM, HEAD_DIM), F32)],
        scratch_shapes=scratch,
        compiler_params=_params(("arbitrary", "arbitrary")),
        name="hgrn_prompt",
    )(x, gpre, win, lb_logits, gon, wout, gpost)


def _hgrn_sample(i, j, x, state, states_out, gpre, win, lb_logits, gon, wout, gpost):
    nb, seq, d = x.shape
    e = wout.shape[1]
    heads = e // HEAD_DIM
    n = nb * seq
    g = SAMPLE_SEQS_PER_STEP
    assert seq == 4 and nb % g == 0 and g % 2 == 0
    xf = x.reshape(n, d)
    act = jax.ShapeDtypeStruct((n, e), F32)
    qe, kd, decp, v, gate, oi = pl.pallas_call(
        functools.partial(_hgrn_sample_pre_kernel, j, seq),
        grid=(1,),
        in_specs=[_const_spec(xf.shape), _layer_spec(gpre, i), _layer_spec(win, i),
                  _const_spec(lb_logits.shape)],
        out_specs=[_const_spec((n, e))] * 6,
        out_shape=[act] * 6,
        compiler_params=_params(("arbitrary",)),
        name="hgrn_sample_pre",
    )(xf, gpre, win, lb_logits)

    row_spec = pl.BlockSpec((g * seq, e), lambda s: (s, 0))
    st_spec = pl.BlockSpec((None, g, heads, HEAD_DIM, HEAD_DIM), lambda s: (j, s, 0, 0, 0))
    rec_in = [qe, kd, decp, v, oi, state]
    rec_specs = [row_spec] * 5 + [st_spec]
    aliases = {}
    if states_out is not None:
        rec_in.append(states_out)
        rec_specs.append(pl.BlockSpec(memory_space=pl.ANY))
        aliases = {len(rec_in) - 1: 1}
    o, states_out = pl.pallas_call(
        _hgrn_sample_rec_kernel,
        grid=(nb // g,),
        in_specs=rec_specs,
        out_specs=[row_spec, st_spec],
        out_shape=[act, jax.ShapeDtypeStruct(state.shape, F32)],
        input_output_aliases=aliases,
        compiler_params=_params(("arbitrary",)),
        name="hgrn_sample_rec",
    )(*rec_in)

    y = pl.pallas_call(
        _hgrn_sample_post_kernel,
        grid=(1,),
        in_specs=[_const_spec(xf.shape), _const_spec((n, e)), _const_spec((n, e)),
                  _layer_spec(gon, j), _layer_spec(wout, i), _layer_spec(gpost, i)],
        out_specs=_const_spec((n, d)),
        out_shape=jax.ShapeDtypeStruct((n, d), F32),
        compiler_params=_params(("arbitrary",)),
        name="hgrn_sample_post",
    )(xf, o, gate, gon, wout, gpost)
    return y.reshape(nb, seq, d), states_out


def kernel(x_prompt, x_sample, state_conv, state_hgrn, norm_pre, w_in, conv_w, hgrn_lb_logits,
           hgrn_onorm, w_out, norm_post):
    depth = w_in.shape[0]
    win_b = w_in.astype(BF16)
    wout_b = w_out.astype(BF16)
    gpre = norm_pre.astype(F32)[:, None, :]
    gpost = norm_post.astype(F32)[:, None, :]
    gon = hgrn_onorm.astype(F32)[:, None, :]
    lb_logits = hgrn_lb_logits.astype(F32)
    state_hgrn = state_hgrn.astype(F32)
    xp, xs = x_prompt, x_sample
    conv_p, conv_s, hgrn_p, hgrn_s = [], [], [], None
    for i in range(depth):
        j = i // 2
        if i % 2 == 0:
            xp, bp = _conv_prompt(i, j, xp, gpre, win_b, conv_w, wout_b, gpost)
            xs, bs = _conv_sample(i, j, xs, state_conv[j], gpre, win_b, conv_w, wout_b, gpost)
            conv_p.append(bp)
            conv_s.append(bs)
        else:
            xp, sp = _hgrn_prompt(i, j, xp, gpre, win_b, lb_logits, gon, wout_b, gpost)
            xs, hgrn_s = _hgrn_sample(i, j, xs, state_hgrn, hgrn_s, gpre, win_b, lb_logits, gon,
                                      wout_b, gpost)
            hgrn_p.append(sp)
    return (xp, xs, jnp.stack(conv_p), jnp.stack(conv_s), jnp.stack(hgrn_p), hgrn_s)
```

```python
import functools
import math

import jax
import jax.numpy as jnp
from jax import lax
from jax.experimental import pallas as pl
from jax.experimental.pallas import tpu as pltpu

F32 = jnp.float32
BF16 = jnp.bfloat16

RMS_EPS = 1e-6
LOG_F_FLOOR = -20.0
LOG2_E = math.log2(math.e)
HEAD_DIM = 128
MXU_WIDTH = 256
CONV_W = 3
HGRN_CHUNK = 128
BASE_BLOCK = 8
PROMPT_TILE = 256
SAMPLE_SEQS_PER_STEP = 8
VMEM_LIMIT_BYTES = 56 * 1024 * 1024


def _rms(x, g):
    ms = jnp.mean(x * x, axis=-1, keepdims=True)
    return x * lax.rsqrt(ms + RMS_EPS) * g


def _silu(x):
    return x * (1.0 / (1.0 + jnp.exp(-x)))


def _dot(a, b):
    return jnp.dot(a, b, preferred_element_type=F32)


def _dot_nt(a, b):
    return lax.dot_general(a, b, (((1,), (1,)), ((), ())), preferred_element_type=F32)


def _dot_tn(a, b):
    return lax.dot_general(a, b, (((0,), (0,)), ((), ())), preferred_element_type=F32)


def _lower_bound(logits, j):
    m = jnp.max(logits, axis=0, keepdims=True)
    e = jnp.exp(logits - m)
    p = e / jnp.sum(e, axis=0, keepdims=True)
    if j == 0:
        return jnp.zeros_like(p[0:1])
    return jnp.sum(p[1:j + 1], axis=0, keepdims=True)


def _forget_gate(fpre, lb):
    e = jnp.exp(-jnp.abs(fpre))
    r = 1.0 / (1.0 + e)
    er = e * r
    pos = fpre >= 0
    sig = jnp.where(pos, r, er)
    nsig = jnp.where(pos, er, r)
    f = lb + (1.0 - lb) * sig
    log2_f = jnp.maximum(jnp.log(f) * LOG2_E, LOG_F_FLOOR * LOG2_E)
    k = jnp.minimum((1.0 - lb) * nsig, 1.0 - math.exp(LOG_F_FLOOR))
    return log2_f, k


def _cumsum_rows(x, group):
    pos = lax.broadcasted_iota(jnp.int32, x.shape, 0) & (group - 1)
    s = 1
    while s < group:
        x = x + jnp.where(pos >= s, pltpu.roll(x, s, axis=0), 0.0)
        s *= 2
    return x


def _group_allsum(x, group):
    n = x.shape[0]
    pos = lax.broadcasted_iota(jnp.int32, x.shape, 0) & (group - 1)
    s = 1
    while s < group:
        partner = jnp.where((pos & s) == 0, pltpu.roll(x, n - s, axis=0), pltpu.roll(x, s, axis=0))
        x = x + partner
        s *= 2
    return x


def _level_matrix(n):
    t = lax.broadcasted_iota(jnp.int32, (n, n), 0)
    s = lax.broadcasted_iota(jnp.int32, (n, n), 1)
    sh = BASE_BLOCK.bit_length() - 1
    lv = jnp.where(((t >> sh) == (s >> sh)) & (t >= s), 1, 0)
    c, i = 2 * BASE_BLOCK, 2
    while c <= n:
        half, sh = c // 2, sh + 1
        own = ((t >> sh) == (s >> sh)) & ((t & half) != 0) & ((s & half) == 0)
        lv = jnp.where(own, i, lv)
        c, i = 2 * c, i + 1
    return lv


def _level_sizes(chunk):
    sizes, c = [], 2 * BASE_BLOCK
    while c <= chunk:
        sizes.append(c)
        c *= 2
    return sizes


def _chunk_operands(q, k, lf2):
    c = q.shape[0]
    local = _cumsum_rows(lf2, BASE_BLOCK)
    b_parts, x_parts, carry = [], [], None
    for g in range(c // BASE_BLOCK):
        blk = local[g * BASE_BLOCK:(g + 1) * BASE_BLOCK]
        x_parts.append(blk - blk[BASE_BLOCK // 2 - 1:BASE_BLOCK // 2])
        if carry is not None:
            blk = blk + carry
        b_parts.append(blk)
        carry = blk[BASE_BLOCK - 1:BASE_BLOCK]
    b = jnp.concatenate(b_parts, axis=0)
    x = jnp.concatenate(x_parts, axis=0)
    b_last = carry
    qe = (q * jnp.exp2(b)).astype(BF16)
    kd = (k * jnp.exp2(b_last - b)).astype(BF16)
    q8 = (q * jnp.exp2(x)).astype(BF16)
    k8 = (k * jnp.exp2(-x)).astype(BF16)
    w = []
    for size in _level_sizes(c):
        half = size // 2
        parts = []
        for s0 in range(0, c, size):
            r = b[s0 + half - 1:s0 + half]
            lo, hi = slice(s0, s0 + half), slice(s0 + half, s0 + size)
            parts.append(k[lo] * jnp.exp2(r - b[lo]))
            parts.append(q[hi] * jnp.exp2(b[hi] - r))
        w.append(jnp.concatenate(parts, axis=0).astype(BF16))
    return qe, kd, q8, k8, w, jnp.exp2(b_last)


def _head_norm_gate(o, gate, gon):
    parts = []
    for hd in range(o.shape[1] // HEAD_DIM):
        oh = o[:, hd * HEAD_DIM:(hd + 1) * HEAD_DIM]
        parts.append(_rms(oh, gon))
    return jnp.concatenate(parts, axis=1) * gate


def _conv_mix(h, win_ref, cw_ref, prev1, prev2, pos):
    e = win_ref.shape[1] // 4
    v = _dot(h, win_ref[:, 0:e])
    cg = _dot(h, win_ref[:, 2 * e:3 * e])
    u = cg * v
    u1 = jnp.where(pos >= 1, pltpu.roll(u, 1, axis=0), prev1)
    u2 = jnp.where(pos >= 2, pltpu.roll(u, 2, axis=0), prev2)
    conv = cw_ref[0:1, :] * u2 + cw_ref[1:2, :] * u1 + cw_ref[2:3, :] * u
    bg = _dot(h, win_ref[:, e:2 * e])
    z = _dot(h, win_ref[:, 3 * e:4 * e])
    return bg * conv * _silu(z), u


def _conv_prompt_kernel(x_ref, gpre_ref, win_ref, cw_ref, wout_ref, gpost_ref,
                        y_ref, buf_ref, tail_ref):
    @pl.when(pl.program_id(1) == 0)
    def _():
        tail_ref[...] = jnp.zeros_like(tail_ref)

    x = x_ref[0]
    tm = x.shape[0]
    h = _rms(x, gpre_ref[...]).astype(BF16)
    pos = lax.broadcasted_iota(jnp.int32, (tm, win_ref.shape[1] // 4), 0)
    tail = tail_ref[...]
    prev1 = tail[7:8, :]
    prev2 = jnp.where(pos == 0, tail[6:7, :], tail[7:8, :])
    g, u = _conv_mix(h, win_ref, cw_ref, prev1, prev2, pos)
    y = _dot(g.astype(BF16), wout_ref[...])
    y_ref[0] = x + _rms(y, gpost_ref[...])
    tail_ref[...] = u[tm - 8:tm, :]
    buf_ref[0] = u[tm - (CONV_W - 1):tm, :]


def _conv_sample_kernel(seq, x_ref, p1_ref, p2_ref, gpre_ref, win_ref, cw_ref, wout_ref, gpost_ref,
                        y_ref, u_ref):
    x = x_ref[...]
    h = _rms(x, gpre_ref[...]).astype(BF16)
    pos = lax.broadcasted_iota(jnp.int32, p1_ref.shape, 0) & (seq - 1)
    g, u = _conv_mix(h, win_ref, cw_ref, p1_ref[...], p2_ref[...], pos)
    y = _dot(g.astype(BF16), wout_ref[...])
    y_ref[...] = x + _rms(y, gpost_ref[...])
    u_ref[...] = u


def _hgrn_prompt_kernel(j, nt, x_ref, gpre_ref, win_ref, lbl_ref, gon_ref, wout_ref, gpost_ref,
                        y_ref, sout_ref, st_s, dec_s, x_s, gate_s, qe_s, kd_s, q8_s, k8_s, v_s, *w_s):
    g = pl.program_id(0)
    e = win_ref.shape[1] // 4
    heads = e // HEAD_DIM
    tm = x_ref.shape[1]
    prev = g - 1

    @pl.when(g == 0)
    def _():
        for ref in (dec_s, x_s, gate_s, qe_s, kd_s, q8_s, k8_s, v_s) + tuple(w_s):
            ref[...] = jnp.zeros_like(ref)

    @pl.when(jnp.logical_or(g == 0, lax.rem(prev, nt) == 0))
    def _():
        st_s[...] = jnp.zeros_like(st_s)

    lv = _level_matrix(HGRN_CHUNK)
    gon = gon_ref[...]
    chunks = tm // HGRN_CHUNK

    def finish_chunk_head(c, hd):
        rows = slice(c * HGRN_CHUNK, (c + 1) * HGRN_CHUNK)
        cols = slice(hd * HEAD_DIM, (hd + 1) * HEAD_DIM)
        a = jnp.where(lv == 1, _dot_nt(q8_s[rows, cols], k8_s[rows, cols]), 0.0)
        for i, w_ref in enumerate(w_s):
            wv = w_ref[rows, cols]
            a = jnp.where(lv == i + 2, _dot_nt(wv, wv), a)
        vb = v_s[rows, cols]
        st = st_s[hd]
        o = _dot(a.astype(BF16), vb) + _dot_nt(qe_s[rows, cols], st.astype(BF16))
        st3 = st.reshape(HEAD_DIM // 8, 8, HEAD_DIM) * dec_s[c, :, cols]
        st_s[hd] = st3.reshape(HEAD_DIM, HEAD_DIM) + _dot_tn(vb, kd_s[rows, cols])
        return (_rms(o, gon) * gate_s[rows, cols]).astype(BF16)

    x = x_ref[0]
    h = _rms(x, gpre_ref[...]).astype(BF16)
    lb = _lower_bound(lbl_ref[...], j)
    out_blocks = [[None] * heads for _ in range(chunks)]
    for cg in range(e // MXU_WIDTH):
        gc = slice(cg * MXU_WIDTH, (cg + 1) * MXU_WIDTH)
        q_raw, f_raw, v_raw, z_raw = (
            _dot(h, win_ref[:, part * e + cg * MXU_WIDTH:part * e + (cg + 1) * MXU_WIDTH])
            for part in range(4))
        for hd in range(cg * MXU_WIDTH // HEAD_DIM, (cg + 1) * MXU_WIDTH // HEAD_DIM):
            for c in range(chunks):
                out_blocks[c][hd] = finish_chunk_head(c, hd)
        q = _silu(q_raw) * (HEAD_DIM ** -0.5)
        lf2, k = _forget_gate(f_raw, lb[:, gc])
        v_s[:, gc] = v_raw.astype(BF16)
        gate_s[:, gc] = _silu(z_raw)
        for c in range(chunks):
            rows = slice(c * HGRN_CHUNK, (c + 1) * HGRN_CHUNK)
            qe, kd, q8, k8, w, decay = _chunk_operands(q[rows], k[rows], lf2[rows])
            qe_s[rows, gc] = qe
            kd_s[rows, gc] = kd
            q8_s[rows, gc] = q8
            k8_s[rows, gc] = k8
            for w_ref, w_val in zip(w_s, w):
                w_ref[rows, gc] = w_val
            dec_s[c, :, gc] = jnp.broadcast_to(decay, (8, MXU_WIDTH))

    gated = jnp.concatenate([jnp.concatenate(blocks, axis=1) for blocks in out_blocks], axis=0)
    y = _dot(gated, wout_ref[...])
    y_ref[0] = x_s[...] + _rms(y, gpost_ref[...])
    x_s[...] = x

    @pl.when(jnp.logical_and(g > 0, lax.rem(prev, nt) == nt - 1))
    def _():
        for hd in range(heads):
            sout_ref[0, hd] = st_s[hd].T


def _hgrn_sample_pre_kernel(j, seq, x_ref, gpre_ref, win_ref, lbl_ref,
                            qe_ref, kd_ref, decp_ref, v_ref, gate_ref, oi_ref):
    e = win_ref.shape[1] // 4
    heads = e // HEAD_DIM
    x = x_ref[...]
    n = x.shape[0]
    h = _rms(x, gpre_ref[...]).astype(BF16)
    lb = _lower_bound(lbl_ref[...], j)
    gate_ref[...] = _silu(_dot(h, win_ref[:, 3 * e:4 * e]))

    t = lax.broadcasted_iota(jnp.int32, (n, n), 0)
    s = lax.broadcasted_iota(jnp.int32, (n, n), 1)
    sh = seq.bit_length() - 1
    same_seq_causal = ((t >> sh) == (s >> sh)) & (t >= s)
    row = lax.broadcasted_iota(jnp.int32, (n, HEAD_DIM), 0)
    pos = row & (seq - 1)

    for hd in range(heads):
        cols = slice(hd * HEAD_DIM, (hd + 1) * HEAD_DIM)
        q = _silu(_dot(h, win_ref[:, cols])) * (HEAD_DIM ** -0.5)
        lf2, k = _forget_gate(_dot(h, win_ref[:, e + hd * HEAD_DIM:e + (hd + 1) * HEAD_DIM]), lb[:, cols])
        v = _dot(h, win_ref[:, 2 * e + hd * HEAD_DIM:2 * e + (hd + 1) * HEAD_DIM])
        b = _cumsum_rows(lf2, seq)
        b_last = _group_allsum(lf2, seq)
        x_mid = b - _group_allsum(jnp.where(pos < seq // 2, lf2, 0.0), seq)
        a = _dot_nt((q * jnp.exp2(x_mid)).astype(BF16), (k * jnp.exp2(-x_mid)).astype(BF16))
        a = jnp.where(same_seq_causal, a, 0.0)
        vb = v.astype(BF16)
        oi_ref[:, cols] = _dot(a.astype(BF16), vb)
        qe_ref[:, cols] = q * jnp.exp2(b)
        kd_ref[:, cols] = k * jnp.exp2(b_last - b)
        v_ref[:, cols] = v
        dec = jnp.exp2(b_last)
        dsw = jnp.where((row & 7) < 4, pltpu.roll(dec, n - 4, axis=0), pltpu.roll(dec, 4, axis=0))
        hi = dsw.astype(BF16).astype(F32)
        mid = (dsw - hi).astype(BF16).astype(F32)
        lo = (dsw - hi - mid).astype(BF16).astype(F32)
        decp_ref[:, cols] = jnp.where(pos == 0, hi, jnp.where(pos == 1, mid, jnp.where(pos == 2, lo, 0.0)))


def _hgrn_sample_rec_kernel(qe_ref, kd_ref, decp_ref, v_ref, oi_ref, s0_ref, *rest):
    o_ref, s1_ref = rest[-2:]
    nseq = s0_ref.shape[0]
    heads = s0_ref.shape[1]
    row = lax.broadcasted_iota(jnp.int32, (8, HEAD_DIM), 0)
    pad = jnp.zeros((HEAD_DIM - 8, HEAD_DIM), F32)
    pad2 = jnp.zeros((HEAD_DIM - 8, 2 * HEAD_DIM), F32)

    def pair_body(p, carry):
        rows = pl.ds(pl.multiple_of(p * 8, 8), 8)
        for hd in range(heads):
            cols = slice(hd * HEAD_DIM, (hd + 1) * HEAD_DIM)
            qe8 = qe_ref[rows, cols].astype(BF16)
            kd8 = kd_ref[rows, cols]
            dp8 = decp_ref[rows, cols]
            v8 = v_ref[rows, cols]
            o8 = oi_ref[rows, cols]
            for half in range(2):
                mine = (row >= 4) if half else (row < 4)
                s0 = s0_ref[2 * p + half, hd]
                o_inter = _dot(qe8, s0.astype(BF16))
                o8 = o8 + jnp.where(mine, o_inter, 0.0)
                lhs = jnp.concatenate([jnp.where(mine, kd8, dp8), pad], axis=0).astype(BF16)
                ones = jnp.where(mine, 0.0, 1.0)
                rhs = jnp.concatenate(
                    [jnp.concatenate([jnp.where(mine, v8, 0.0), ones], axis=1), pad2], axis=0).astype(BF16)
                upd = _dot_tn(lhs, rhs)
                s1_ref[2 * p + half, hd] = upd[:, HEAD_DIM:] * s0 + upd[:, :HEAD_DIM]
            o_ref[rows, cols] = o8
        return carry

    lax.fori_loop(0, nseq // 2, pair_body, 0)


def _hgrn_sample_post_kernel(x_ref, o_ref, gate_ref, gon_ref, wout_ref, gpost_ref, y_ref):
    x = x_ref[...]
    g = _head_norm_gate(o_ref[...], gate_ref[...], gon_ref[...])
    y = _dot(g.astype(BF16), wout_ref[...])
    y_ref[...] = x + _rms(y, gpost_ref[...])


def _const_spec(shape):
    nd = len(shape)
    return pl.BlockSpec(shape, lambda *_: (0,) * nd)


def _layer_spec(arr, layer):
    nd = arr.ndim - 1
    return pl.BlockSpec((None,) + arr.shape[1:], lambda *_: (layer,) + (0,) * nd)


def _params(sem):
    return pltpu.CompilerParams(dimension_semantics=sem, vmem_limit_bytes=VMEM_LIMIT_BYTES)


def _conv_prompt(i, j, x, gpre, win, cw, wout, gpost):
    bsz, t, d = x.shape
    e = wout.shape[1]
    tm = PROMPT_TILE
    assert t % tm == 0 and tm % 8 == 0
    return pl.pallas_call(
        _conv_prompt_kernel,
        grid=(bsz, t // tm),
        in_specs=[pl.BlockSpec((1, tm, d), lambda b, s: (b, s, 0)),
                  _layer_spec(gpre, i), _layer_spec(win, i), _layer_spec(cw, j),
                  _layer_spec(wout, i), _layer_spec(gpost, i)],
        out_specs=[pl.BlockSpec((1, tm, d), lambda b, s: (b, s, 0)),
                   pl.BlockSpec((1, CONV_W - 1, e), lambda b, s: (b, 0, 0))],
        out_shape=[jax.ShapeDtypeStruct(x.shape, F32),
                   jax.ShapeDtypeStruct((bsz, CONV_W - 1, e), F32)],
        scratch_shapes=[pltpu.VMEM((8, e), F32)],
        compiler_params=_params(("arbitrary", "arbitrary")),
        name="conv_prompt",
    )(x, gpre, win, cw, wout, gpost)


def _conv_sample(i, j, x, buf, gpre, win, cw, wout, gpost):
    nb, seq, d = x.shape
    e = wout.shape[1]
    assert seq & (seq - 1) == 0 and seq >= CONV_W - 1
    n = nb * seq
    xf = x.reshape(n, d)
    zeros = lambda r: jnp.zeros((nb, r, e), F32)
    p1 = jnp.concatenate([buf[:, 1:2], zeros(seq - 1)], axis=1).reshape(n, e)
    p2 = jnp.concatenate([buf[:, 0:2], zeros(seq - 2)], axis=1).reshape(n, e)
    y, u = pl.pallas_call(
        functools.partial(_conv_sample_kernel, seq),
        grid=(1,),
        in_specs=[_const_spec(xf.shape), _const_spec(p1.shape), _const_spec(p2.shape),
                  _layer_spec(gpre, i), _layer_spec(win, i), _layer_spec(cw, j),
                  _layer_spec(wout, i), _layer_spec(gpost, i)],
        out_specs=[_const_spec((n, d)), _const_spec((n, e))],
        out_shape=[jax.ShapeDtypeStruct((n, d), F32), jax.ShapeDtypeStruct((n, e), F32)],
        compiler_params=_params(("arbitrary",)),
        name="conv_sample",
    )(xf, p1, p2, gpre, win, cw, wout, gpost)
    return y.reshape(nb, seq, d), u.reshape(nb, seq, e)[:, seq - (CONV_W - 1):]


def _hgrn_prompt(i, j, x, gpre, win, lb_logits, gon, wout, gpost):
    bsz, t, d = x.shape
    e = wout.shape[1]
    heads = e // HEAD_DIM
    tm = PROMPT_TILE
    assert t % tm == 0 and tm % HGRN_CHUNK == 0
    nt = t // tm
    n_tiles = bsz * nt
    n_bf16 = 5 + len(_level_sizes(HGRN_CHUNK))
    scratch = ([pltpu.VMEM((heads, HEAD_DIM, HEAD_DIM), F32),
                pltpu.VMEM((tm // HGRN_CHUNK, 8, e), F32),
                pltpu.VMEM((tm, d), F32),
                pltpu.VMEM((tm, e), F32)]
               + [pltpu.VMEM((tm, e), BF16) for _ in range(n_bf16)])

    def staged(g):
        tile = jnp.minimum(g, n_tiles - 1)
        return tile // nt, tile % nt

    def finished(g):
        tile = jnp.maximum(g - 1, 0)
        return tile // nt, tile % nt

    return pl.pallas_call(
        functools.partial(_hgrn_prompt_kernel, j, nt),
        grid=(n_tiles + 1,),
        in_specs=[pl.BlockSpec((1, tm, d), lambda g: staged(g) + (0,)),
                  _layer_spec(gpre, i), _layer_spec(win, i), _const_spec(lb_logits.shape),
                  _layer_spec(gon, j), _layer_spec(wout, i), _layer_spec(gpost, i)],
        out_specs=[pl.BlockSpec((1, tm, d), lambda g: finished(g) + (0,)),
                   pl.BlockSpec((1, heads, HEAD_DIM, HEAD_DIM), lambda g: (finished(g)[0], 0, 0, 0))],
        out_shape=[jax.ShapeDtypeStruct(x.shape, F32),
                   jax.ShapeDtypeStruct((bsz, heads, HEAD_DIM, HEAD_DIM), F32)],
        scratch_shapes=scratch,
        compiler_params=_params(("arbitrary",)),
        name="hgrn_prompt",
    )(x, gpre, win, lb_logits, gon, wout, gpost)


def _hgrn_sample(i, j, x, state, states_out, gpre, win, lb_logits, gon, wout, gpost):
    nb, seq, d = x.shape
    e = wout.shape[1]
    heads = e // HEAD_DIM
    n = nb * seq
    g = SAMPLE_SEQS_PER_STEP
    assert seq == 4 and nb % g == 0 and g % 2 == 0
    xf = x.reshape(n, d)
    act = jax.ShapeDtypeStruct((n, e), F32)
    qe, kd, decp, v, gate, oi = pl.pallas_call(
        functools.partial(_hgrn_sample_pre_kernel, j, seq),
        grid=(1,),
        in_specs=[_const_spec(xf.shape), _layer_spec(gpre, i), _layer_spec(win, i),
                  _const_spec(lb_logits.shape)],
        out_specs=[_const_spec((n, e))] * 6,
        out_shape=[act] * 6,
        compiler_params=_params(("arbitrary",)),
        name="hgrn_sample_pre",
    )(xf, gpre, win, lb_logits)

    row_spec = pl.BlockSpec((g * seq, e), lambda s: (s, 0))
    st_spec = pl.BlockSpec((None, g, heads, HEAD_DIM, HEAD_DIM), lambda s: (j, s, 0, 0, 0))
    rec_in = [qe, kd, decp, v, oi, state]
    rec_specs = [row_spec] * 5 + [st_spec]
    aliases = {}
    if states_out is not None:
        rec_in.append(states_out)
        rec_specs.append(pl.BlockSpec(memory_space=pl.ANY))
        aliases = {len(rec_in) - 1: 1}
    o, states_out = pl.pallas_call(
        _hgrn_sample_rec_kernel,
        grid=(nb // g,),
        in_specs=rec_specs,
        out_specs=[row_spec, st_spec],
        out_shape=[act, jax.ShapeDtypeStruct(state.shape, F32)],
        input_output_aliases=aliases,
        compiler_params=_params(("arbitrary",)),
        name="hgrn_sample_rec",
    )(*rec_in)

    y = pl.pallas_call(
        _hgrn_sample_post_kernel,
        grid=(1,),
        in_specs=[_const_spec(xf.shape), _const_spec((n, e)), _const_spec((n, e)),
                  _layer_spec(gon, j), _layer_spec(wout, i), _layer_spec(gpost, i)],
        out_specs=_const_spec((n, d)),
        out_shape=jax.ShapeDtypeStruct((n, d), F32),
        compiler_params=_params(("arbitrary",)),
        name="hgrn_sample_post",
    )(xf, o, gate, gon, wout, gpost)
    return y.reshape(nb, seq, d), states_out


def kernel(x_prompt, x_sample, state_conv, state_hgrn, norm_pre, w_in, conv_w, hgrn_lb_logits,
           hgrn_onorm, w_out, norm_post):
    depth = w_in.shape[0]
    win_b = w_in.astype(BF16)
    wout_b = w_out.astype(BF16)
    gpre = norm_pre.astype(F32)[:, None, :]
    gpost = norm_post.astype(F32)[:, None, :]
    gon = hgrn_onorm.astype(F32)[:, None, :]
    lb_logits = hgrn_lb_logits.astype(F32)
    state_hgrn = state_hgrn.astype(F32)
    xp, xs = x_prompt, x_sample
    conv_p, conv_s, hgrn_p, hgrn_s = [], [], [], None
    for i in range(depth):
        j = i // 2
        if i % 2 == 0:
            xp, bp = _conv_prompt(i, j, xp, gpre, win_b, conv_w, wout_b, gpost)
            xs, bs = _conv_sample(i, j, xs, state_conv[j], gpre, win_b, conv_w, wout_b, gpost)
            conv_p.append(bp)
            conv_s.append(bs)
        else:
            xp, sp = _hgrn_prompt(i, j, xp, gpre, win_b, lb_logits, gon, wout_b, gpost)
            xs, hgrn_s = _hgrn_sample(i, j, xs, state_hgrn, hgrn_s, gpre, win_b, lb_logits, gon,
                                      wout_b, gpost)
            hgrn_p.append(sp)
    return (xp, xs, jnp.stack(conv_p), jnp.stack(conv_s), jnp.stack(hgrn_p), hgrn_s)
```

```python
import functools
import math

import jax
import jax.numpy as jnp
from jax import lax
from jax.experimental import pallas as pl
from jax.experimental.pallas import tpu as pltpu

F32 = jnp.float32
BF16 = jnp.bfloat16

RMS_EPS = 1e-6
LOG_F_FLOOR = -20.0
LOG2_E = math.log2(math.e)
HEAD_DIM = 128
MXU_WIDTH = 256
CONV_W = 3
HGRN_CHUNK = 128
BASE_BLOCK = 8
PROMPT_TILE = 256
SAMPLE_SEQS_PER_STEP = 8
VMEM_LIMIT_BYTES = 56 * 1024 * 1024


def _rms(x, g):
    ms = jnp.mean(x * x, axis=-1, keepdims=True)
    return x * lax.rsqrt(ms + RMS_EPS) * g


def _silu(x):
    return x * (1.0 / (1.0 + jnp.exp2(x * (-LOG2_E))))


def _dot(a, b):
    return jnp.dot(a, b, preferred_element_type=F32)


def _dot_nt(a, b):
    return lax.dot_general(a, b, (((1,), (1,)), ((), ())), preferred_element_type=F32)


def _dot_tn(a, b):
    return lax.dot_general(a, b, (((0,), (0,)), ((), ())), preferred_element_type=F32)


def _lower_bound(logits, j):
    m = jnp.max(logits, axis=0, keepdims=True)
    e = jnp.exp(logits - m)
    p = e / jnp.sum(e, axis=0, keepdims=True)
    if j == 0:
        return jnp.zeros_like(p[0:1])
    return jnp.sum(p[1:j + 1], axis=0, keepdims=True)


def _forget_gate(fpre, lb):
    e = jnp.exp2(jnp.abs(fpre) * (-LOG2_E))
    r = 1.0 / (1.0 + e)
    er = e * r
    pos = fpre >= 0
    sig = jnp.where(pos, r, er)
    nsig = jnp.where(pos, er, r)
    f = lb + (1.0 - lb) * sig
    log2_f = jnp.maximum(jnp.log(f) * LOG2_E, LOG_F_FLOOR * LOG2_E)
    k = jnp.minimum((1.0 - lb) * nsig, 1.0 - math.exp(LOG_F_FLOOR))
    return log2_f, k


def _cumsum_rows(x, group):
    pos = lax.broadcasted_iota(jnp.int32, x.shape, 0) & (group - 1)
    s = 1
    while s < group:
        x = x + jnp.where(pos >= s, pltpu.roll(x, s, axis=0), 0.0)
        s *= 2
    return x


def _group_allsum(x, group):
    n = x.shape[0]
    pos = lax.broadcasted_iota(jnp.int32, x.shape, 0) & (group - 1)
    s = 1
    while s < group:
        partner = jnp.where((pos & s) == 0, pltpu.roll(x, n - s, axis=0), pltpu.roll(x, s, axis=0))
        x = x + partner
        s *= 2
    return x


def _level_matrix(n):
    t = lax.broadcasted_iota(jnp.int32, (n, n), 0)
    s = lax.broadcasted_iota(jnp.int32, (n, n), 1)
    sh = BASE_BLOCK.bit_length() - 1
    lv = jnp.where(((t >> sh) == (s >> sh)) & (t >= s), 1, 0)
    c, i = 2 * BASE_BLOCK, 2
    while c <= n:
        half, sh = c // 2, sh + 1
        own = ((t >> sh) == (s >> sh)) & ((t & half) != 0) & ((s & half) == 0)
        lv = jnp.where(own, i, lv)
        c, i = 2 * c, i + 1
    return lv


def _level_sizes(chunk):
    sizes, c = [], 2 * BASE_BLOCK
    while c <= chunk:
        sizes.append(c)
        c *= 2
    return sizes


def _chunk_operands(q, k, lf2):
    c = q.shape[0]
    local = _cumsum_rows(lf2, BASE_BLOCK)
    b_parts, x_parts, carry = [], [], None
    for g in range(c // BASE_BLOCK):
        blk = local[g * BASE_BLOCK:(g + 1) * BASE_BLOCK]
        x_parts.append(blk - blk[BASE_BLOCK // 2 - 1:BASE_BLOCK // 2])
        if carry is not None:
            blk = blk + carry
        b_parts.append(blk)
        carry = blk[BASE_BLOCK - 1:BASE_BLOCK]
    b = jnp.concatenate(b_parts, axis=0)
    x = jnp.concatenate(x_parts, axis=0)
    b_last = carry
    qe = (q * jnp.exp2(b)).astype(BF16)
    kd = (k * jnp.exp2(b_last - b)).astype(BF16)
    q8 = (q * jnp.exp2(x)).astype(BF16)
    k8 = (k * jnp.exp2(-x)).astype(BF16)
    w = []
    for size in _level_sizes(c):
        half = size // 2
        parts = []
        for s0 in range(0, c, size):
            r = b[s0 + half - 1:s0 + half]
            lo, hi = slice(s0, s0 + half), slice(s0 + half, s0 + size)
            parts.append(k[lo] * jnp.exp2(r - b[lo]))
            parts.append(q[hi] * jnp.exp2(b[hi] - r))
        w.append(jnp.concatenate(parts, axis=0).astype(BF16))
    return qe, kd, q8, k8, w, jnp.exp2(b_last)


def _head_norm_gate(o, gate, gon):
    parts = []
    for hd in range(o.shape[1] // HEAD_DIM):
        oh = o[:, hd * HEAD_DIM:(hd + 1) * HEAD_DIM]
        parts.append(_rms(oh, gon))
    return jnp.concatenate(parts, axis=1) * gate


def _conv_mix(h, win_ref, cw_ref, prev1, prev2, pos):
    e = win_ref.shape[1] // 4
    v = _dot(h, win_ref[:, 0:e])
    cg = _dot(h, win_ref[:, 2 * e:3 * e])
    u = cg * v
    u1 = jnp.where(pos >= 1, pltpu.roll(u, 1, axis=0), prev1)
    u2 = jnp.where(pos >= 2, pltpu.roll(u, 2, axis=0), prev2)
    conv = cw_ref[0:1, :] * u2 + cw_ref[1:2, :] * u1 + cw_ref[2:3, :] * u
    bg = _dot(h, win_ref[:, e:2 * e])
    z = _dot(h, win_ref[:, 3 * e:4 * e])
    return bg * conv * _silu(z), u


def _conv_prompt_kernel(x_ref, gpre_ref, win_ref, cw_ref, wout_ref, gpost_ref,
                        y_ref, buf_ref, tail_ref):
    @pl.when(pl.program_id(1) == 0)
    def _():
        tail_ref[...] = jnp.zeros_like(tail_ref)

    x = x_ref[0]
    tm = x.shape[0]
    h = _rms(x, gpre_ref[...]).astype(BF16)
    pos = lax.broadcasted_iota(jnp.int32, (tm, win_ref.shape[1] // 4), 0)
    tail = tail_ref[...]
    prev1 = tail[7:8, :]
    prev2 = jnp.where(pos == 0, tail[6:7, :], tail[7:8, :])
    g, u = _conv_mix(h, win_ref, cw_ref, prev1, prev2, pos)
    y = _dot(g.astype(BF16), wout_ref[...])
    y_ref[0] = x + _rms(y, gpost_ref[...])
    tail_ref[...] = u[tm - 8:tm, :]
    buf_ref[0] = u[tm - (CONV_W - 1):tm, :]


def _conv_sample_kernel(seq, x_ref, p1_ref, p2_ref, gpre_ref, win_ref, cw_ref, wout_ref, gpost_ref,
                        y_ref, u_ref):
    x = x_ref[...]
    h = _rms(x, gpre_ref[...]).astype(BF16)
    pos = lax.broadcasted_iota(jnp.int32, p1_ref.shape, 0) & (seq - 1)
    g, u = _conv_mix(h, win_ref, cw_ref, p1_ref[...], p2_ref[...], pos)
    y = _dot(g.astype(BF16), wout_ref[...])
    y_ref[...] = x + _rms(y, gpost_ref[...])
    u_ref[...] = u


def _hgrn_prompt_kernel(j, nt, xp_ref, xr_ref, gpre_ref, win_ref, lbl_ref, gon_ref, wout_ref, gpost_ref,
                        y_ref, sout_ref, st_s, dec_s, raw_s, gate_s, qe_s, kd_s, q8_s, k8_s, v_s, *w_s):
    g = pl.program_id(0)
    e = win_ref.shape[1] // 4
    heads = e // HEAD_DIM
    tm = xp_ref.shape[1]
    fin = g - 2

    @pl.when(g == 0)
    def _():
        for ref in (dec_s, raw_s, gate_s, qe_s, kd_s, q8_s, k8_s, v_s) + tuple(w_s):
            ref[...] = jnp.zeros_like(ref)

    @pl.when(jnp.logical_or(g < 2, lax.rem(fin, nt) == 0))
    def _():
        st_s[...] = jnp.zeros_like(st_s)

    lv = _level_matrix(HGRN_CHUNK)
    owned = [lv == i + 1 for i in range(len(w_s) + 1)]
    gon = gon_ref[...]
    chunks = tm // HGRN_CHUNK

    def finish_chunk_head(c, hd):
        rows = slice(c * HGRN_CHUNK, (c + 1) * HGRN_CHUNK)
        cols = slice(hd * HEAD_DIM, (hd + 1) * HEAD_DIM)
        a = jnp.where(owned[0], _dot_nt(q8_s[rows, cols], k8_s[rows, cols]), 0.0)
        for i, w_ref in enumerate(w_s):
            wv = w_ref[rows, cols]
            a = jnp.where(owned[i + 1], _dot_nt(wv, wv), a)
        vb = v_s[rows, cols]
        st = st_s[hd]
        o = _dot(a.astype(BF16), vb) + _dot_nt(qe_s[rows, cols], st.astype(BF16))
        st3 = st.reshape(HEAD_DIM // 8, 8, HEAD_DIM) * dec_s[c, :, cols]
        st_s[hd] = st3.reshape(HEAD_DIM, HEAD_DIM) + _dot_tn(vb, kd_s[rows, cols])
        return (_rms(o, gon) * gate_s[rows, cols]).astype(BF16)

    h = _rms(xp_ref[0], gpre_ref[...]).astype(BF16)
    lb = _lower_bound(lbl_ref[...], j)
    out_blocks = [[None] * heads for _ in range(chunks)]
    for cg in range(e // MXU_WIDTH):
        gc = slice(cg * MXU_WIDTH, (cg + 1) * MXU_WIDTH)
        part_cols = [slice(part * e + cg * MXU_WIDTH, part * e + (cg + 1) * MXU_WIDTH) for part in range(4)]
        for hd in range(cg * MXU_WIDTH // HEAD_DIM, (cg + 1) * MXU_WIDTH // HEAD_DIM):
            for c in range(chunks):
                out_blocks[c][hd] = finish_chunk_head(c, hd)
        q = _silu(raw_s[:, part_cols[0]]) * (HEAD_DIM ** -0.5)
        lf2, k = _forget_gate(raw_s[:, part_cols[1]], lb[:, gc])
        v_s[:, gc] = raw_s[:, part_cols[2]].astype(BF16)
        gate_s[:, gc] = _silu(raw_s[:, part_cols[3]])
        for c in range(chunks):
            rows = slice(c * HGRN_CHUNK, (c + 1) * HGRN_CHUNK)
            qe, kd, q8, k8, w, decay = _chunk_operands(q[rows], k[rows], lf2[rows])
            qe_s[rows, gc] = qe
            kd_s[rows, gc] = kd
            q8_s[rows, gc] = q8
            k8_s[rows, gc] = k8
            for w_ref, w_val in zip(w_s, w):
                w_ref[rows, gc] = w_val
            dec_s[c, :, gc] = jnp.broadcast_to(decay, (8, MXU_WIDTH))
        for cols in part_cols:
            raw_s[:, cols] = _dot(h, win_ref[:, cols])

    gated = jnp.concatenate([jnp.concatenate(blocks, axis=1) for blocks in out_blocks], axis=0)
    y = _dot(gated, wout_ref[...])
    y_ref[0] = xr_ref[0] + _rms(y, gpost_ref[...])

    @pl.when(jnp.logical_and(g >= 2, lax.rem(fin, nt) == nt - 1))
    def _():
        for hd in range(heads):
            sout_ref[0, hd] = st_s[hd].T


def _hgrn_sample_pre_kernel(j, seq, x_ref, gpre_ref, win_ref, lbl_ref,
                            qe_ref, kd_ref, decp_ref, v_ref, gate_ref, oi_ref):
    e = win_ref.shape[1] // 4
    heads = e // HEAD_DIM
    x = x_ref[...]
    n = x.shape[0]
    h = _rms(x, gpre_ref[...]).astype(BF16)
    lb = _lower_bound(lbl_ref[...], j)
    gate_ref[...] = _silu(_dot(h, win_ref[:, 3 * e:4 * e]))

    t = lax.broadcasted_iota(jnp.int32, (n, n), 0)
    s = lax.broadcasted_iota(jnp.int32, (n, n), 1)
    sh = seq.bit_length() - 1
    same_seq_causal = ((t >> sh) == (s >> sh)) & (t >= s)
    row = lax.broadcasted_iota(jnp.int32, (n, HEAD_DIM), 0)
    pos = row & (seq - 1)

    for hd in range(heads):
        cols = slice(hd * HEAD_DIM, (hd + 1) * HEAD_DIM)
        q = _silu(_dot(h, win_ref[:, cols])) * (HEAD_DIM ** -0.5)
        lf2, k = _forget_gate(_dot(h, win_ref[:, e + hd * HEAD_DIM:e + (hd + 1) * HEAD_DIM]), lb[:, cols])
        v = _dot(h, win_ref[:, 2 * e + hd * HEAD_DIM:2 * e + (hd + 1) * HEAD_DIM])
        b = _cumsum_rows(lf2, seq)
        b_last = _group_allsum(lf2, seq)
        x_mid = b - _group_allsum(jnp.where(pos < seq // 2, lf2, 0.0), seq)
        a = _dot_nt((q * jnp.exp2(x_mid)).astype(BF16), (k * jnp.exp2(-x_mid)).astype(BF16))
        a = jnp.where(same_seq_causal, a, 0.0)
        vb = v.astype(BF16)
        oi_ref[:, cols] = _dot(a.astype(BF16), vb)
        qe_ref[:, cols] = q * jnp.exp2(b)
        kd_ref[:, cols] = k * jnp.exp2(b_last - b)
        v_ref[:, cols] = v
        dec = jnp.exp2(b_last)
        dsw = jnp.where((row & 7) < 4, pltpu.roll(dec, n - 4, axis=0), pltpu.roll(dec, 4, axis=0))
        hi = dsw.astype(BF16).astype(F32)
        mid = (dsw - hi).astype(BF16).astype(F32)
        lo = (dsw - hi - mid).astype(BF16).astype(F32)
        decp_ref[:, cols] = jnp.where(pos == 0, hi, jnp.where(pos == 1, mid, jnp.where(pos == 2, lo, 0.0)))


def _hgrn_sample_rec_kernel(qe_ref, kd_ref, decp_ref, v_ref, oi_ref, s0_ref, *rest):
    o_ref, s1_ref = rest[-2:]
    nseq = s0_ref.shape[0]
    heads = s0_ref.shape[1]
    row = lax.broadcasted_iota(jnp.int32, (8, HEAD_DIM), 0)
    pad = jnp.zeros((HEAD_DIM - 8, HEAD_DIM), F32)
    pad2 = jnp.zeros((HEAD_DIM - 8, 2 * HEAD_DIM), F32)

    def pair_body(p, carry):
        rows = pl.ds(pl.multiple_of(p * 8, 8), 8)
        for hd in range(heads):
            cols = slice(hd * HEAD_DIM, (hd + 1) * HEAD_DIM)
            qe8 = qe_ref[rows, cols].astype(BF16)
            kd8 = kd_ref[rows, cols]
            dp8 = decp_ref[rows, cols]
            v8 = v_ref[rows, cols]
            o8 = oi_ref[rows, cols]
            for half in range(2):
                mine = (row >= 4) if half else (row < 4)
                s0 = s0_ref[2 * p + half, hd]
                o_inter = _dot(qe8, s0.astype(BF16))
                o8 = o8 + jnp.where(mine, o_inter, 0.0)
                lhs = jnp.concatenate([jnp.where(mine, kd8, dp8), pad], axis=0).astype(BF16)
                ones = jnp.where(mine, 0.0, 1.0)
                rhs = jnp.concatenate(
                    [jnp.concatenate([jnp.where(mine, v8, 0.0), ones], axis=1), pad2], axis=0).astype(BF16)
                upd = _dot_tn(lhs, rhs)
                s1_ref[2 * p + half, hd] = upd[:, HEAD_DIM:] * s0 + upd[:, :HEAD_DIM]
            o_ref[rows, cols] = o8
        return carry

    lax.fori_loop(0, nseq // 2, pair_body, 0)


def _hgrn_sample_post_kernel(x_ref, o_ref, gate_ref, gon_ref, wout_ref, gpost_ref, y_ref):
    x = x_ref[...]
    g = _head_norm_gate(o_ref[...], gate_ref[...], gon_ref[...])
    y = _dot(g.astype(BF16), wout_ref[...])
    y_ref[...] = x + _rms(y, gpost_ref[...])


def _const_spec(shape):
    nd = len(shape)
    return pl.BlockSpec(shape, lambda *_: (0,) * nd)


def _layer_spec(arr, layer):
    nd = arr.ndim - 1
    return pl.BlockSpec((None,) + arr.shape[1:], lambda *_: (layer,) + (0,) * nd)


def _params(sem):
    return pltpu.CompilerParams(dimension_semantics=sem, vmem_limit_bytes=VMEM_LIMIT_BYTES)


def _conv_prompt(i, j, x, gpre, win, cw, wout, gpost):
    bsz, t, d = x.shape
    e = wout.shape[1]
    tm = PROMPT_TILE
    assert t % tm == 0 and tm % 8 == 0
    return pl.pallas_call(
        _conv_prompt_kernel,
        grid=(bsz, t // tm),
        in_specs=[pl.BlockSpec((1, tm, d), lambda b, s: (b, s, 0)),
                  _layer_spec(gpre, i), _layer_spec(win, i), _layer_spec(cw, j),
                  _layer_spec(wout, i), _layer_spec(gpost, i)],
        out_specs=[pl.BlockSpec((1, tm, d), lambda b, s: (b, s, 0)),
                   pl.BlockSpec((1, CONV_W - 1, e), lambda b, s: (b, 0, 0))],
        out_shape=[jax.ShapeDtypeStruct(x.shape, F32),
                   jax.ShapeDtypeStruct((bsz, CONV_W - 1, e), F32)],
        scratch_shapes=[pltpu.VMEM((8, e), F32)],
        compiler_params=_params(("arbitrary", "arbitrary")),
        name="conv_prompt",
    )(x, gpre, win, cw, wout, gpost)


def _conv_sample(i, j, x, buf, gpre, win, cw, wout, gpost):
    nb, seq, d = x.shape
    e = wout.shape[1]
    assert seq & (seq - 1) == 0 and seq >= CONV_W - 1
    n = nb * seq
    xf = x.reshape(n, d)
    zeros = lambda r: jnp.zeros((nb, r, e), F32)
    p1 = jnp.concatenate([buf[:, 1:2], zeros(seq - 1)], axis=1).reshape(n, e)
    p2 = jnp.concatenate([buf[:, 0:2], zeros(seq - 2)], axis=1).reshape(n, e)
    y, u = pl.pallas_call(
        functools.partial(_conv_sample_kernel, seq),
        grid=(1,),
        in_specs=[_const_spec(xf.shape), _const_spec(p1.shape), _const_spec(p2.shape),
                  _layer_spec(gpre, i), _layer_spec(win, i), _layer_spec(cw, j),
                  _layer_spec(wout, i), _layer_spec(gpost, i)],
        out_specs=[_const_spec((n, d)), _const_spec((n, e))],
        out_shape=[jax.ShapeDtypeStruct((n, d), F32), jax.ShapeDtypeStruct((n, e), F32)],
        compiler_params=_params(("arbitrary",)),
        name="conv_sample",
    )(xf, p1, p2, gpre, win, cw, wout, gpost)
    return y.reshape(nb, seq, d), u.reshape(nb, seq, e)[:, seq - (CONV_W - 1):]


def _hgrn_prompt(i, j, x, gpre, win, lb_logits, gon, wout, gpost):
    bsz, t, d = x.shape
    e = wout.shape[1]
    heads = e // HEAD_DIM
    tm = PROMPT_TILE
    assert t % tm == 0 and tm % HGRN_CHUNK == 0
    nt = t // tm
    n_tiles = bsz * nt
    n_bf16 = 5 + len(_level_sizes(HGRN_CHUNK))
    scratch = ([pltpu.VMEM((heads, HEAD_DIM, HEAD_DIM), F32),
                pltpu.VMEM((tm // HGRN_CHUNK, 8, e), F32),
                pltpu.VMEM((tm, 4 * e), F32),
                pltpu.VMEM((tm, e), F32)]
               + [pltpu.VMEM((tm, e), BF16) for _ in range(n_bf16)])

    def projected(g):
        tile = jnp.minimum(g, n_tiles - 1)
        return tile // nt, tile % nt

    def finished(g):
        tile = jnp.maximum(g - 2, 0)
        return tile // nt, tile % nt

    return pl.pallas_call(
        functools.partial(_hgrn_prompt_kernel, j, nt),
        grid=(n_tiles + 2,),
        in_specs=[pl.BlockSpec((1, tm, d), lambda g: projected(g) + (0,)),
                  pl.BlockSpec((1, tm, d), lambda g: finished(g) + (0,)),
                  _layer_spec(gpre, i), _layer_spec(win, i), _const_spec(lb_logits.shape),
                  _layer_spec(gon, j), _layer_spec(wout, i), _layer_spec(gpost, i)],
        out_specs=[pl.BlockSpec((1, tm, d), lambda g: finished(g) + (0,)),
                   pl.BlockSpec((1, heads, HEAD_DIM, HEAD_DIM), lambda g: (finished(g)[0], 0, 0, 0))],
        out_shape=[jax.ShapeDtypeStruct(x.shape, F32),
                   jax.ShapeDtypeStruct((bsz, heads, HEAD_DIM, HEAD_DIM), F32)],
        scratch_shapes=scratch,
        compiler_params=_params(("arbitrary",)),
        name="hgrn_prompt",
    )(x, x, gpre, win, lb_logits, gon, wout, gpost)


def _hgrn_sample(i, j, x, state, states_out, gpre, win, lb_logits, gon, wout, gpost):
    nb, seq, d = x.shape
    e = wout.shape[1]
    heads = e // HEAD_DIM
    n = nb * seq
    g = SAMPLE_SEQS_PER_STEP
    assert seq == 4 and nb % g == 0 and g % 2 == 0
    xf = x.reshape(n, d)
    act = jax.ShapeDtypeStruct((n, e), F32)
    qe, kd, decp, v, gate, oi = pl.pallas_call(
        functools.partial(_hgrn_sample_pre_kernel, j, seq),
        grid=(1,),
        in_specs=[_const_spec(xf.shape), _layer_spec(gpre, i), _layer_spec(win, i),
                  _const_spec(lb_logits.shape)],
        out_specs=[_const_spec((n, e))] * 6,
        out_shape=[act] * 6,
        compiler_params=_params(("arbitrary",)),
        name="hgrn_sample_pre",
    )(xf, gpre, win, lb_logits)

    row_spec = pl.BlockSpec((g * seq, e), lambda s: (s, 0))
    st_spec = pl.BlockSpec((None, g, heads, HEAD_DIM, HEAD_DIM), lambda s: (j, s, 0, 0, 0))
    rec_in = [qe, kd, decp, v, oi, state]
    rec_specs = [row_spec] * 5 + [st_spec]
    aliases = {}
    if states_out is not None:
        rec_in.append(states_out)
        rec_specs.append(pl.BlockSpec(memory_space=pl.ANY))
        aliases = {len(rec_in) - 1: 1}
    o, states_out = pl.pallas_call(
        _hgrn_sample_rec_kernel,
        grid=(nb // g,),
        in_specs=rec_specs,
        out_specs=[row_spec, st_spec],
        out_shape=[act, jax.ShapeDtypeStruct(state.shape, F32)],
        input_output_aliases=aliases,
        compiler_params=_params(("arbitrary",)),
        name="hgrn_sample_rec",
    )(*rec_in)

    y = pl.pallas_call(
        _hgrn_sample_post_kernel,
        grid=(1,),
        in_specs=[_const_spec(xf.shape), _const_spec((n, e)), _const_spec((n, e)),
                  _layer_spec(gon, j), _layer_spec(wout, i), _layer_spec(gpost, i)],
        out_specs=_const_spec((n, d)),
        out_shape=jax.ShapeDtypeStruct((n, d), F32),
        compiler_params=_params(("arbitrary",)),
        name="hgrn_sample_post",
    )(xf, o, gate, gon, wout, gpost)
    return y.reshape(nb, seq, d), states_out


def kernel(x_prompt, x_sample, state_conv, state_hgrn, norm_pre, w_in, conv_w, hgrn_lb_logits,
           hgrn_onorm, w_out, norm_post):
    depth = w_in.shape[0]
    win_b = w_in.astype(BF16)
    wout_b = w_out.astype(BF16)
    gpre = norm_pre.astype(F32)[:, None, :]
    gpost = norm_post.astype(F32)[:, None, :]
    gon = hgrn_onorm.astype(F32)[:, None, :]
    lb_logits = hgrn_lb_logits.astype(F32)
    state_hgrn = state_hgrn.astype(F32)
    xp, xs = x_prompt, x_sample
    conv_p, conv_s, hgrn_p, hgrn_s = [], [], [], None
    for i in range(depth):
        j = i // 2
        if i % 2 == 0:
            xp, bp = _conv_prompt(i, j, xp, gpre, win_b, conv_w, wout_b, gpost)
            xs, bs = _conv_sample(i, j, xs, state_conv[j], gpre, win_b, conv_w, wout_b, gpost)
            conv_p.append(bp)
            conv_s.append(bs)
        else:
            xp, sp = _hgrn_prompt(i, j, xp, gpre, win_b, lb_logits, gon, wout_b, gpost)
            xs, hgrn_s = _hgrn_sample(i, j, xs, state_hgrn, hgrn_s, gpre, win_b, lb_logits, gon,
                                      wout_b, gpost)
            hgrn_p.append(sp)
    return (xp, xs, jnp.stack(conv_p), jnp.stack(conv_s), jnp.stack(hgrn_p), hgrn_s)
```

```python
import functools
import math

import jax
import jax.numpy as jnp
from jax import lax
from jax.experimental import pallas as pl
from jax.experimental.pallas import tpu as pltpu

F32 = jnp.float32
BF16 = jnp.bfloat16

RMS_EPS = 1e-6
LOG_F_FLOOR = -20.0
LOG2_E = math.log2(math.e)
HEAD_DIM = 128
MXU_WIDTH = 256
CONV_W = 3
HGRN_CHUNK = 128
BASE_BLOCK = 8
SCORE_LOOKAHEAD = 2
PROMPT_TILE = 256
SAMPLE_SEQS_PER_STEP = 8
VMEM_LIMIT_BYTES = 56 * 1024 * 1024


def _rms(x, g):
    ms = jnp.mean(x * x, axis=-1, keepdims=True)
    return x * lax.rsqrt(ms + RMS_EPS) * g


def _silu(x):
    return x * (1.0 / (1.0 + jnp.exp2(x * (-LOG2_E))))


def _dot(a, b):
    return jnp.dot(a, b, preferred_element_type=F32)


def _dot_nt(a, b):
    return lax.dot_general(a, b, (((1,), (1,)), ((), ())), preferred_element_type=F32)


def _dot_tn(a, b):
    return lax.dot_general(a, b, (((0,), (0,)), ((), ())), preferred_element_type=F32)


def _lower_bound(logits, j):
    m = jnp.max(logits, axis=0, keepdims=True)
    e = jnp.exp(logits - m)
    p = e / jnp.sum(e, axis=0, keepdims=True)
    if j == 0:
        return jnp.zeros_like(p[0:1])
    return jnp.sum(p[1:j + 1], axis=0, keepdims=True)


def _forget_gate(fpre, lb):
    e = jnp.exp2(jnp.abs(fpre) * (-LOG2_E))
    r = 1.0 / (1.0 + e)
    er = e * r
    pos = fpre >= 0
    sig = jnp.where(pos, r, er)
    nsig = jnp.where(pos, er, r)
    f = lb + (1.0 - lb) * sig
    log2_f = jnp.maximum(jnp.log(f) * LOG2_E, LOG_F_FLOOR * LOG2_E)
    k = jnp.minimum((1.0 - lb) * nsig, 1.0 - math.exp(LOG_F_FLOOR))
    return log2_f, k


def _cumsum_rows(x, group):
    pos = lax.broadcasted_iota(jnp.int32, x.shape, 0) & (group - 1)
    s = 1
    while s < group:
        x = x + jnp.where(pos >= s, pltpu.roll(x, s, axis=0), 0.0)
        s *= 2
    return x


def _group_allsum(x, group):
    n = x.shape[0]
    pos = lax.broadcasted_iota(jnp.int32, x.shape, 0) & (group - 1)
    s = 1
    while s < group:
        partner = jnp.where((pos & s) == 0, pltpu.roll(x, n - s, axis=0), pltpu.roll(x, s, axis=0))
        x = x + partner
        s *= 2
    return x


def _level_matrix(n):
    t = lax.broadcasted_iota(jnp.int32, (n, n), 0)
    s = lax.broadcasted_iota(jnp.int32, (n, n), 1)
    sh = BASE_BLOCK.bit_length() - 1
    lv = jnp.where(((t >> sh) == (s >> sh)) & (t >= s), 1, 0)
    c, i = 2 * BASE_BLOCK, 2
    while c <= n:
        half, sh = c // 2, sh + 1
        own = ((t >> sh) == (s >> sh)) & ((t & half) != 0) & ((s & half) == 0)
        lv = jnp.where(own, i, lv)
        c, i = 2 * c, i + 1
    return lv


def _level_sizes(chunk):
    sizes, c = [], 2 * BASE_BLOCK
    while c <= chunk:
        sizes.append(c)
        c *= 2
    return sizes


def _chunk_operands(q, k, lf2):
    c = q.shape[0]
    local = _cumsum_rows(lf2, BASE_BLOCK)
    b_parts, x_parts, carry = [], [], None
    for g in range(c // BASE_BLOCK):
        blk = local[g * BASE_BLOCK:(g + 1) * BASE_BLOCK]
        x_parts.append(blk - blk[BASE_BLOCK // 2 - 1:BASE_BLOCK // 2])
        if carry is not None:
            blk = blk + carry
        b_parts.append(blk)
        carry = blk[BASE_BLOCK - 1:BASE_BLOCK]
    b = jnp.concatenate(b_parts, axis=0)
    x = jnp.concatenate(x_parts, axis=0)
    b_last = carry
    qe = (q * jnp.exp2(b)).astype(BF16)
    kd = (k * jnp.exp2(b_last - b)).astype(BF16)
    q8 = (q * jnp.exp2(x)).astype(BF16)
    k8 = (k * jnp.exp2(-x)).astype(BF16)
    w = []
    for size in _level_sizes(c):
        half = size // 2
        parts = []
        for s0 in range(0, c, size):
            r = b[s0 + half - 1:s0 + half]
            lo, hi = slice(s0, s0 + half), slice(s0 + half, s0 + size)
            parts.append(k[lo] * jnp.exp2(r - b[lo]))
            parts.append(q[hi] * jnp.exp2(b[hi] - r))
        w.append(jnp.concatenate(parts, axis=0).astype(BF16))
    return qe, kd, q8, k8, w, jnp.exp2(b_last)


def _head_norm_gate(o, gate, gon):
    parts = []
    for hd in range(o.shape[1] // HEAD_DIM):
        oh = o[:, hd * HEAD_DIM:(hd + 1) * HEAD_DIM]
        parts.append(_rms(oh, gon))
    return jnp.concatenate(parts, axis=1) * gate


def _conv_mix(h, win_ref, cw_ref, prev1, prev2, pos):
    e = win_ref.shape[1] // 4
    v = _dot(h, win_ref[:, 0:e])
    cg = _dot(h, win_ref[:, 2 * e:3 * e])
    u = cg * v
    u1 = jnp.where(pos >= 1, pltpu.roll(u, 1, axis=0), prev1)
    u2 = jnp.where(pos >= 2, pltpu.roll(u, 2, axis=0), prev2)
    conv = cw_ref[0:1, :] * u2 + cw_ref[1:2, :] * u1 + cw_ref[2:3, :] * u
    bg = _dot(h, win_ref[:, e:2 * e])
    z = _dot(h, win_ref[:, 3 * e:4 * e])
    return bg * conv * _silu(z), u


def _conv_prompt_kernel(x_ref, gpre_ref, win_ref, cw_ref, wout_ref, gpost_ref,
                        y_ref, buf_ref, tail_ref):
    @pl.when(pl.program_id(1) == 0)
    def _():
        tail_ref[...] = jnp.zeros_like(tail_ref)

    x = x_ref[0]
    tm = x.shape[0]
    h = _rms(x, gpre_ref[...]).astype(BF16)
    pos = lax.broadcasted_iota(jnp.int32, (tm, win_ref.shape[1] // 4), 0)
    tail = tail_ref[...]
    prev1 = tail[7:8, :]
    prev2 = jnp.where(pos == 0, tail[6:7, :], tail[7:8, :])
    g, u = _conv_mix(h, win_ref, cw_ref, prev1, prev2, pos)
    y = _dot(g.astype(BF16), wout_ref[...])
    y_ref[0] = x + _rms(y, gpost_ref[...])
    tail_ref[...] = u[tm - 8:tm, :]
    buf_ref[0] = u[tm - (CONV_W - 1):tm, :]


def _conv_sample_kernel(seq, x_ref, p1_ref, p2_ref, gpre_ref, win_ref, cw_ref, wout_ref, gpost_ref,
                        y_ref, u_ref):
    x = x_ref[...]
    h = _rms(x, gpre_ref[...]).astype(BF16)
    pos = lax.broadcasted_iota(jnp.int32, p1_ref.shape, 0) & (seq - 1)
    g, u = _conv_mix(h, win_ref, cw_ref, p1_ref[...], p2_ref[...], pos)
    y = _dot(g.astype(BF16), wout_ref[...])
    y_ref[...] = x + _rms(y, gpost_ref[...])
    u_ref[...] = u


def _hgrn_prompt_kernel(j, nt, xp_ref, xr_ref, gpre_ref, win_ref, lbl_ref, gon_ref, wout_ref, gpost_ref,
                        y_ref, sout_ref, st_s, dec_s, raw_s, gate_s, qe_s, kd_s, q8_s, k8_s, v_s, *w_s):
    g = pl.program_id(0)
    e = win_ref.shape[1] // 4
    heads = e // HEAD_DIM
    tm = xp_ref.shape[1]
    fin = g - 2

    @pl.when(g == 0)
    def _():
        for ref in (dec_s, raw_s, gate_s, qe_s, kd_s, q8_s, k8_s, v_s) + tuple(w_s):
            ref[...] = jnp.zeros_like(ref)

    @pl.when(jnp.logical_or(g < 2, lax.rem(fin, nt) == 0))
    def _():
        st_s[...] = jnp.zeros_like(st_s)

    lv = _level_matrix(HGRN_CHUNK)
    owned = [lv == i + 1 for i in range(len(w_s) + 1)]
    gon = gon_ref[...]
    chunks = tm // HGRN_CHUNK

    def chunk_scores(c, hd):
        rows = slice(c * HGRN_CHUNK, (c + 1) * HGRN_CHUNK)
        cols = slice(hd * HEAD_DIM, (hd + 1) * HEAD_DIM)
        a = jnp.where(owned[0], _dot_nt(q8_s[rows, cols], k8_s[rows, cols]), 0.0)
        for i, w_ref in enumerate(w_s):
            wv = w_ref[rows, cols]
            a = jnp.where(owned[i + 1], _dot_nt(wv, wv), a)
        return a.astype(BF16)

    def chunk_output(c, hd, a):
        rows = slice(c * HGRN_CHUNK, (c + 1) * HGRN_CHUNK)
        cols = slice(hd * HEAD_DIM, (hd + 1) * HEAD_DIM)
        vb = v_s[rows, cols]
        st = st_s[hd]
        o = _dot(a, vb) + _dot_nt(qe_s[rows, cols], st.astype(BF16))
        st3 = st.reshape(HEAD_DIM // 8, 8, HEAD_DIM) * dec_s[c, :, cols]
        st_s[hd] = st3.reshape(HEAD_DIM, HEAD_DIM) + _dot_tn(vb, kd_s[rows, cols])
        return (_rms(o, gon) * gate_s[rows, cols]).astype(BF16)

    h = _rms(xp_ref[0], gpre_ref[...]).astype(BF16)
    lb = _lower_bound(lbl_ref[...], j)
    out_blocks = [[None] * heads for _ in range(chunks)]
    groups = e // MXU_WIDTH

    def part_cols(cg):
        return [slice(part * e + cg * MXU_WIDTH, part * e + (cg + 1) * MXU_WIDTH) for part in range(4)]


    def stage(cg, c, q_raw, f_raw, v_raw, z_raw):
        gc = slice(cg * MXU_WIDTH, (cg + 1) * MXU_WIDTH)
        rows = slice(c * HGRN_CHUNK, (c + 1) * HGRN_CHUNK)
        q = _silu(q_raw) * (HEAD_DIM ** -0.5)
        lf2, k = _forget_gate(f_raw, lb[:, gc])
        v_s[rows, gc] = v_raw.astype(BF16)
        gate_s[rows, gc] = _silu(z_raw)
        qe, kd, q8, k8, w, decay = _chunk_operands(q, k, lf2)
        qe_s[rows, gc] = qe
        kd_s[rows, gc] = kd
        q8_s[rows, gc] = q8
        k8_s[rows, gc] = k8
        for w_ref, w_val in zip(w_s, w):
            w_ref[rows, gc] = w_val
        dec_s[c, :, gc] = jnp.broadcast_to(decay, (8, MXU_WIDTH))

    def project(cols):
        raw_s[:, cols] = _dot(h, win_ref[:, cols])

    def output():
        gated = jnp.concatenate([jnp.concatenate(blocks, axis=1) for blocks in out_blocks], axis=0)
        y = _dot(gated, wout_ref[...])
        y_ref[0] = xr_ref[0] + _rms(y, gpost_ref[...])

    heads_per_group = MXU_WIDTH // HEAD_DIM
    raws = {}
    for cg in range(groups):
        for c in range(chunks):
            rows = slice(c * HGRN_CHUNK, (c + 1) * HGRN_CHUNK)
            raws[cg, c] = [raw_s[rows, cols] for cols in part_cols(cg)]
    projections = [cols for cg in range(groups) for cols in part_cols(cg)]
    units = [(cg, c, hd) for cg in range(groups) for c in range(chunks)
             for hd in range(cg * heads_per_group, (cg + 1) * heads_per_group)]
    scores = {}
    for n in range(len(units) + SCORE_LOOKAHEAD):
        if n < len(units):
            _, c, hd = units[n]
            scores[c, hd] = chunk_scores(c, hd)
        if projections:
            project(projections.pop(0))
        if n >= SCORE_LOOKAHEAD:
            done = n - SCORE_LOOKAHEAD
            cg, c, hd = units[done]
            out_blocks[c][hd] = chunk_output(c, hd, scores.pop((c, hd)))
            if (done + 1) % heads_per_group == 0:
                stage(cg, c, *raws.pop((cg, c)))
    for cols in projections:
        project(cols)
    output()

    @pl.when(jnp.logical_and(g >= 2, lax.rem(fin, nt) == nt - 1))
    def _():
        for hd in range(heads):
            sout_ref[0, hd] = st_s[hd].T


def _hgrn_sample_pre_kernel(j, seq, x_ref, gpre_ref, win_ref, lbl_ref,
                            qe_ref, kd_ref, decp_ref, v_ref, gate_ref, oi_ref):
    e = win_ref.shape[1] // 4
    heads = e // HEAD_DIM
    x = x_ref[...]
    n = x.shape[0]
    h = _rms(x, gpre_ref[...]).astype(BF16)
    lb = _lower_bound(lbl_ref[...], j)
    gate_ref[...] = _silu(_dot(h, win_ref[:, 3 * e:4 * e]))

    t = lax.broadcasted_iota(jnp.int32, (n, n), 0)
    s = lax.broadcasted_iota(jnp.int32, (n, n), 1)
    sh = seq.bit_length() - 1
    same_seq_causal = ((t >> sh) == (s >> sh)) & (t >= s)
    row = lax.broadcasted_iota(jnp.int32, (n, HEAD_DIM), 0)
    pos = row & (seq - 1)

    for hd in range(heads):
        cols = slice(hd * HEAD_DIM, (hd + 1) * HEAD_DIM)
        q = _silu(_dot(h, win_ref[:, cols])) * (HEAD_DIM ** -0.5)
        lf2, k = _forget_gate(_dot(h, win_ref[:, e + hd * HEAD_DIM:e + (hd + 1) * HEAD_DIM]), lb[:, cols])
        v = _dot(h, win_ref[:, 2 * e + hd * HEAD_DIM:2 * e + (hd + 1) * HEAD_DIM])
        b = _cumsum_rows(lf2, seq)
        b_last = _group_allsum(lf2, seq)
        x_mid = b - _group_allsum(jnp.where(pos < seq // 2, lf2, 0.0), seq)
        a = _dot_nt((q * jnp.exp2(x_mid)).astype(BF16), (k * jnp.exp2(-x_mid)).astype(BF16))
        a = jnp.where(same_seq_causal, a, 0.0)
        vb = v.astype(BF16)
        oi_ref[:, cols] = _dot(a.astype(BF16), vb)
        qe_ref[:, cols] = q * jnp.exp2(b)
        kd_ref[:, cols] = k * jnp.exp2(b_last - b)
        v_ref[:, cols] = v
        dec = jnp.exp2(b_last)
        dsw = jnp.where((row & 7) < 4, pltpu.roll(dec, n - 4, axis=0), pltpu.roll(dec, 4, axis=0))
        hi = dsw.astype(BF16).astype(F32)
        mid = (dsw - hi).astype(BF16).astype(F32)
        lo = (dsw - hi - mid).astype(BF16).astype(F32)
        decp_ref[:, cols] = jnp.where(pos == 0, hi, jnp.where(pos == 1, mid, jnp.where(pos == 2, lo, 0.0)))


def _hgrn_sample_rec_kernel(qe_ref, kd_ref, decp_ref, v_ref, oi_ref, s0_ref, *rest):
    o_ref, s1_ref = rest[-2:]
    nseq = s0_ref.shape[0]
    heads = s0_ref.shape[1]
    row = lax.broadcasted_iota(jnp.int32, (8, HEAD_DIM), 0)
    pad = jnp.zeros((HEAD_DIM - 8, HEAD_DIM), F32)
    pad2 = jnp.zeros((HEAD_DIM - 8, 2 * HEAD_DIM), F32)

    def pair_body(p, carry):
        rows = pl.ds(pl.multiple_of(p * 8, 8), 8)
        for hd in range(heads):
            cols = slice(hd * HEAD_DIM, (hd + 1) * HEAD_DIM)
            qe8 = qe_ref[rows, cols].astype(BF16)
            kd8 = kd_ref[rows, cols]
            dp8 = decp_ref[rows, cols]
            v8 = v_ref[rows, cols]
            o8 = oi_ref[rows, cols]
            for half in range(2):
                mine = (row >= 4) if half else (row < 4)
                s0 = s0_ref[2 * p + half, hd]
                o_inter = _dot(qe8, s0.astype(BF16))
                o8 = o8 + jnp.where(mine, o_inter, 0.0)
                lhs = jnp.concatenate([jnp.where(mine, kd8, dp8), pad], axis=0).astype(BF16)
                ones = jnp.where(mine, 0.0, 1.0)
                rhs = jnp.concatenate(
                    [jnp.concatenate([jnp.where(mine, v8, 0.0), ones], axis=1), pad2], axis=0).astype(BF16)
                upd = _dot_tn(lhs, rhs)
                s1_ref[2 * p + half, hd] = upd[:, HEAD_DIM:] * s0 + upd[:, :HEAD_DIM]
            o_ref[rows, cols] = o8
        return carry

    lax.fori_loop(0, nseq // 2, pair_body, 0)


def _hgrn_sample_post_kernel(x_ref, o_ref, gate_ref, gon_ref, wout_ref, gpost_ref, y_ref):
    x = x_ref[...]
    g = _head_norm_gate(o_ref[...], gate_ref[...], gon_ref[...])
    y = _dot(g.astype(BF16), wout_ref[...])
    y_ref[...] = x + _rms(y, gpost_ref[...])


def _const_spec(shape):
    nd = len(shape)
    return pl.BlockSpec(shape, lambda *_: (0,) * nd)


def _layer_spec(arr, layer):
    nd = arr.ndim - 1
    return pl.BlockSpec((None,) + arr.shape[1:], lambda *_: (layer,) + (0,) * nd)


def _params(sem):
    return pltpu.CompilerParams(dimension_semantics=sem, vmem_limit_bytes=VMEM_LIMIT_BYTES)


def _conv_prompt(i, j, x, gpre, win, cw, wout, gpost):
    bsz, t, d = x.shape
    e = wout.shape[1]
    tm = PROMPT_TILE
    assert t % tm == 0 and tm % 8 == 0
    return pl.pallas_call(
        _conv_prompt_kernel,
        grid=(bsz, t // tm),
        in_specs=[pl.BlockSpec((1, tm, d), lambda b, s: (b, s, 0)),
                  _layer_spec(gpre, i), _layer_spec(win, i), _layer_spec(cw, j),
                  _layer_spec(wout, i), _layer_spec(gpost, i)],
        out_specs=[pl.BlockSpec((1, tm, d), lambda b, s: (b, s, 0)),
                   pl.BlockSpec((1, CONV_W - 1, e), lambda b, s: (b, 0, 0))],
        out_shape=[jax.ShapeDtypeStruct(x.shape, F32),
                   jax.ShapeDtypeStruct((bsz, CONV_W - 1, e), F32)],
        scratch_shapes=[pltpu.VMEM((8, e), F32)],
        compiler_params=_params(("arbitrary", "arbitrary")),
        name="conv_prompt",
    )(x, gpre, win, cw, wout, gpost)


def _conv_sample(i, j, x, buf, gpre, win, cw, wout, gpost):
    nb, seq, d = x.shape
    e = wout.shape[1]
    assert seq & (seq - 1) == 0 and seq >= CONV_W - 1
    n = nb * seq
    xf = x.reshape(n, d)
    zeros = lambda r: jnp.zeros((nb, r, e), F32)
    p1 = jnp.concatenate([buf[:, 1:2], zeros(seq - 1)], axis=1).reshape(n, e)
    p2 = jnp.concatenate([buf[:, 0:2], zeros(seq - 2)], axis=1).reshape(n, e)
    y, u = pl.pallas_call(
        functools.partial(_conv_sample_kernel, seq),
        grid=(1,),
        in_specs=[_const_spec(xf.shape), _const_spec(p1.shape), _const_spec(p2.shape),
                  _layer_spec(gpre, i), _layer_spec(win, i), _layer_spec(cw, j),
                  _layer_spec(wout, i), _layer_spec(gpost, i)],
        out_specs=[_const_spec((n, d)), _const_spec((n, e))],
        out_shape=[jax.ShapeDtypeStruct((n, d), F32), jax.ShapeDtypeStruct((n, e), F32)],
        compiler_params=_params(("arbitrary",)),
        name="conv_sample",
    )(xf, p1, p2, gpre, win, cw, wout, gpost)
    return y.reshape(nb, seq, d), u.reshape(nb, seq, e)[:, seq - (CONV_W - 1):]


def _hgrn_prompt(i, j, x, gpre, win, lb_logits, gon, wout, gpost):
    bsz, t, d = x.shape
    e = wout.shape[1]
    heads = e // HEAD_DIM
    tm = PROMPT_TILE
    assert t % tm == 0 and tm % HGRN_CHUNK == 0
    nt = t // tm
    n_tiles = bsz * nt
    n_bf16 = 5 + len(_level_sizes(HGRN_CHUNK))
    scratch = ([pltpu.VMEM((heads, HEAD_DIM, HEAD_DIM), F32),
                pltpu.VMEM((tm // HGRN_CHUNK, 8, e), F32),
                pltpu.VMEM((tm, 4 * e), F32),
                pltpu.VMEM((tm, e), F32)]
               + [pltpu.VMEM((tm, e), BF16) for _ in range(n_bf16)])

    def projected(g):
        tile = jnp.minimum(g, n_tiles - 1)
        return tile // nt, tile % nt

    def finished(g):
        tile = jnp.maximum(g - 2, 0)
        return tile // nt, tile % nt

    return pl.pallas_call(
        functools.partial(_hgrn_prompt_kernel, j, nt),
        grid=(n_tiles + 2,),
        in_specs=[pl.BlockSpec((1, tm, d), lambda g: projected(g) + (0,)),
                  pl.BlockSpec((1, tm, d), lambda g: finished(g) + (0,)),
                  _layer_spec(gpre, i), _layer_spec(win, i), _const_spec(lb_logits.shape),
                  _layer_spec(gon, j), _layer_spec(wout, i), _layer_spec(gpost, i)],
        out_specs=[pl.BlockSpec((1, tm, d), lambda g: finished(g) + (0,)),
                   pl.BlockSpec((1, heads, HEAD_DIM, HEAD_DIM), lambda g: (finished(g)[0], 0, 0, 0))],
        out_shape=[jax.ShapeDtypeStruct(x.shape, F32),
                   jax.ShapeDtypeStruct((bsz, heads, HEAD_DIM, HEAD_DIM), F32)],
        scratch_shapes=scratch,
        compiler_params=_params(("arbitrary",)),
        name="hgrn_prompt",
    )(x, x, gpre, win, lb_logits, gon, wout, gpost)


def _hgrn_sample(i, j, x, state, states_out, gpre, win, lb_logits, gon, wout, gpost):
    nb, seq, d = x.shape
    e = wout.shape[1]
    heads = e // HEAD_DIM
    n = nb * seq
    g = SAMPLE_SEQS_PER_STEP
    assert seq == 4 and nb % g == 0 and g % 2 == 0
    xf = x.reshape(n, d)
    act = jax.ShapeDtypeStruct((n, e), F32)
    qe, kd, decp, v, gate, oi = pl.pallas_call(
        functools.partial(_hgrn_sample_pre_kernel, j, seq),
        grid=(1,),
        in_specs=[_const_spec(xf.shape), _layer_spec(gpre, i), _layer_spec(win, i),
                  _const_spec(lb_logits.shape)],
        out_specs=[_const_spec((n, e))] * 6,
        out_shape=[act] * 6,
        compiler_params=_params(("arbitrary",)),
        name="hgrn_sample_pre",
    )(xf, gpre, win, lb_logits)

    row_spec = pl.BlockSpec((g * seq, e), lambda s: (s, 0))
    st_spec = pl.BlockSpec((None, g, heads, HEAD_DIM, HEAD_DIM), lambda s: (j, s, 0, 0, 0))
    rec_in = [qe, kd, decp, v, oi, state]
    rec_specs = [row_spec] * 5 + [st_spec]
    aliases = {}
    if states_out is not None:
        rec_in.append(states_out)
        rec_specs.append(pl.BlockSpec(memory_space=pl.ANY))
        aliases = {len(rec_in) - 1: 1}
    o, states_out = pl.pallas_call(
        _hgrn_sample_rec_kernel,
        grid=(nb // g,),
        in_specs=rec_specs,
        out_specs=[row_spec, st_spec],
        out_shape=[act, jax.ShapeDtypeStruct(state.shape, F32)],
        input_output_aliases=aliases,
        compiler_params=_params(("arbitrary",)),
        name="hgrn_sample_rec",
    )(*rec_in)

    y = pl.pallas_call(
        _hgrn_sample_post_kernel,
        grid=(1,),
        in_specs=[_const_spec(xf.shape), _const_spec((n, e)), _const_spec((n, e)),
                  _layer_spec(gon, j), _layer_spec(wout, i), _layer_spec(gpost, i)],
        out_specs=_const_spec((n, d)),
        out_shape=jax.ShapeDtypeStruct((n, d), F32),
        compiler_params=_params(("arbitrary",)),
        name="hgrn_sample_post",
    )(xf, o, gate, gon, wout, gpost)
    return y.reshape(nb, seq, d), states_out


def kernel(x_prompt, x_sample, state_conv, state_hgrn, norm_pre, w_in, conv_w, hgrn_lb_logits,
           hgrn_onorm, w_out, norm_post):
    depth = w_in.shape[0]
    win_b = w_in.astype(BF16)
    wout_b = w_out.astype(BF16)
    gpre = norm_pre.astype(F32)[:, None, :]
    gpost = norm_post.astype(F32)[:, None, :]
    gon = hgrn_onorm.astype(F32)[:, None, :]
    lb_logits = hgrn_lb_logits.astype(F32)
    state_hgrn = state_hgrn.astype(F32)
    xp, xs = x_prompt, x_sample
    conv_p, conv_s, hgrn_p, hgrn_s = [], [], [], None
    for i in range(depth):
        j = i // 2
        if i % 2 == 0:
            xp, bp = _conv_prompt(i, j, xp, gpre, win_b, conv_w, wout_b, gpost)
            xs, bs = _conv_sample(i, j, xs, state_conv[j], gpre, win_b, conv_w, wout_b, gpost)
            conv_p.append(bp)
            conv_s.append(bs)
        else:
            xp, sp = _hgrn_prompt(i, j, xp, gpre, win_b, lb_logits, gon, wout_b, gpost)
            xs, hgrn_s = _hgrn_sample(i, j, xs, state_hgrn, hgrn_s, gpre, win_b, lb_logits, gon,
                                      wout_b, gpost)
            hgrn_p.append(sp)
    return (xp, xs, jnp.stack(conv_p), jnp.stack(conv_s), jnp.stack(hgrn_p), hgrn_s)
```

```python
import functools
import math

import jax
import jax.numpy as jnp
from jax import lax
from jax.experimental import pallas as pl
from jax.experimental.pallas import tpu as pltpu

F32 = jnp.float32
BF16 = jnp.bfloat16

RMS_EPS = 1e-6
LOG_F_FLOOR = -20.0
LOG2_E = math.log2(math.e)
HEAD_DIM = 128
MXU_WIDTH = 256
CONV_W = 3
HGRN_CHUNK = 128
BASE_BLOCK = 8
SCORE_LOOKAHEAD = 2
PROMPT_TILE = 256
CONV_PROMPT_TILE = 512
SAMPLE_SEQS_PER_STEP = 8
VMEM_LIMIT_BYTES = 56 * 1024 * 1024


def _rms(x, g):
    ms = jnp.mean(x * x, axis=-1, keepdims=True)
    return x * lax.rsqrt(ms + RMS_EPS) * g


def _silu(x):
    return x * (1.0 / (1.0 + jnp.exp2(x * (-LOG2_E))))


def _dot(a, b):
    return jnp.dot(a, b, preferred_element_type=F32)


def _dot_nt(a, b):
    return lax.dot_general(a, b, (((1,), (1,)), ((), ())), preferred_element_type=F32)


def _dot_tn(a, b):
    return lax.dot_general(a, b, (((0,), (0,)), ((), ())), preferred_element_type=F32)


def _lower_bound(logits, j):
    m = jnp.max(logits, axis=0, keepdims=True)
    e = jnp.exp(logits - m)
    p = e / jnp.sum(e, axis=0, keepdims=True)
    if j == 0:
        return jnp.zeros_like(p[0:1])
    return jnp.sum(p[1:j + 1], axis=0, keepdims=True)


def _forget_gate(fpre, lb):
    e = jnp.exp2(jnp.abs(fpre) * (-LOG2_E))
    r = 1.0 / (1.0 + e)
    er = e * r
    pos = fpre >= 0
    sig = jnp.where(pos, r, er)
    nsig = jnp.where(pos, er, r)
    f = lb + (1.0 - lb) * sig
    log2_f = jnp.maximum(jnp.log(f) * LOG2_E, LOG_F_FLOOR * LOG2_E)
    k = jnp.minimum((1.0 - lb) * nsig, 1.0 - math.exp(LOG_F_FLOOR))
    return log2_f, k


def _cumsum_rows(x, group):
    pos = lax.broadcasted_iota(jnp.int32, x.shape, 0) & (group - 1)
    s = 1
    while s < group:
        x = x + jnp.where(pos >= s, pltpu.roll(x, s, axis=0), 0.0)
        s *= 2
    return x


def _group_allsum(x, group):
    n = x.shape[0]
    pos = lax.broadcasted_iota(jnp.int32, x.shape, 0) & (group - 1)
    s = 1
    while s < group:
        partner = jnp.where((pos & s) == 0, pltpu.roll(x, n - s, axis=0), pltpu.roll(x, s, axis=0))
        x = x + partner
        s *= 2
    return x


def _level_matrix(n):
    t = lax.broadcasted_iota(jnp.int32, (n, n), 0)
    s = lax.broadcasted_iota(jnp.int32, (n, n), 1)
    sh = BASE_BLOCK.bit_length() - 1
    lv = jnp.where(((t >> sh) == (s >> sh)) & (t >= s), 1, 0)
    c, i = 2 * BASE_BLOCK, 2
    while c <= n:
        half, sh = c // 2, sh + 1
        own = ((t >> sh) == (s >> sh)) & ((t & half) != 0) & ((s & half) == 0)
        lv = jnp.where(own, i, lv)
        c, i = 2 * c, i + 1
    return lv


def _level_sizes(chunk):
    sizes, c = [], 2 * BASE_BLOCK
    while c <= chunk:
        sizes.append(c)
        c *= 2
    return sizes


def _chunk_operands(q, k, lf2):
    c = q.shape[0]
    local = _cumsum_rows(lf2, BASE_BLOCK)
    b_parts, x_parts, carry = [], [], None
    for g in range(c // BASE_BLOCK):
        blk = local[g * BASE_BLOCK:(g + 1) * BASE_BLOCK]
        x_parts.append(blk - blk[BASE_BLOCK // 2 - 1:BASE_BLOCK // 2])
        if carry is not None:
            blk = blk + carry
        b_parts.append(blk)
        carry = blk[BASE_BLOCK - 1:BASE_BLOCK]
    b = jnp.concatenate(b_parts, axis=0)
    x = jnp.concatenate(x_parts, axis=0)
    b_last = carry
    qe = (q * jnp.exp2(b)).astype(BF16)
    kd = (k * jnp.exp2(b_last - b)).astype(BF16)
    q8 = (q * jnp.exp2(x)).astype(BF16)
    k8 = (k * jnp.exp2(-x)).astype(BF16)
    w = []
    for size in _level_sizes(c):
        half = size // 2
        parts = []
        for s0 in range(0, c, size):
            r = b[s0 + half - 1:s0 + half]
            lo, hi = slice(s0, s0 + half), slice(s0 + half, s0 + size)
            parts.append(k[lo] * jnp.exp2(r - b[lo]))
            parts.append(q[hi] * jnp.exp2(b[hi] - r))
        w.append(jnp.concatenate(parts, axis=0).astype(BF16))
    return qe, kd, q8, k8, w, jnp.exp2(b_last)


def _head_norm_gate(o, gate, gon):
    parts = []
    for hd in range(o.shape[1] // HEAD_DIM):
        oh = o[:, hd * HEAD_DIM:(hd + 1) * HEAD_DIM]
        parts.append(_rms(oh, gon))
    return jnp.concatenate(parts, axis=1) * gate


def _conv_mix(h, win_ref, cw_ref, prev1, prev2, pos):
    e = win_ref.shape[1] // 4
    v = _dot(h, win_ref[:, 0:e])
    cg = _dot(h, win_ref[:, 2 * e:3 * e])
    u = cg * v
    u1 = jnp.where(pos >= 1, pltpu.roll(u, 1, axis=0), prev1)
    u2 = jnp.where(pos >= 2, pltpu.roll(u, 2, axis=0), prev2)
    conv = cw_ref[0:1, :] * u2 + cw_ref[1:2, :] * u1 + cw_ref[2:3, :] * u
    bg = _dot(h, win_ref[:, e:2 * e])
    z = _dot(h, win_ref[:, 3 * e:4 * e])
    return bg * conv * _silu(z), u


def _conv_prompt_kernel(x_ref, gpre_ref, win_in_ref, cw_ref, wout_in_ref, gpost_ref,
                        y_ref, buf_ref, tail_ref, win_ref, wout_ref):
    @pl.when(jnp.logical_and(pl.program_id(0) == 0, pl.program_id(1) == 0))
    def _():
        win_ref[...] = win_in_ref[...]
        wout_ref[...] = wout_in_ref[...]

    @pl.when(pl.program_id(1) == 0)
    def _():
        tail_ref[...] = jnp.zeros_like(tail_ref)

    x = x_ref[0]
    tm = x.shape[0]
    h = _rms(x, gpre_ref[...]).astype(BF16)
    pos = lax.broadcasted_iota(jnp.int32, (tm, win_ref.shape[1] // 4), 0)
    tail = tail_ref[...]
    prev1 = tail[7:8, :]
    prev2 = jnp.where(pos == 0, tail[6:7, :], tail[7:8, :])
    g, u = _conv_mix(h, win_ref, cw_ref, prev1, prev2, pos)
    y = _dot(g.astype(BF16), wout_ref[...])
    y_ref[0] = x + _rms(y, gpost_ref[...])
    tail_ref[...] = u[tm - 8:tm, :]
    buf_ref[0] = u[tm - (CONV_W - 1):tm, :]


def _conv_sample_kernel(seq, x_ref, p1_ref, p2_ref, gpre_ref, win_ref, cw_ref, wout_ref, gpost_ref,
                        y_ref, u_ref):
    x = x_ref[...]
    h = _rms(x, gpre_ref[...]).astype(BF16)
    pos = lax.broadcasted_iota(jnp.int32, p1_ref.shape, 0) & (seq - 1)
    g, u = _conv_mix(h, win_ref, cw_ref, p1_ref[...], p2_ref[...], pos)
    y = _dot(g.astype(BF16), wout_ref[...])
    y_ref[...] = x + _rms(y, gpost_ref[...])
    u_ref[...] = u


def _hgrn_prompt_kernel(j, nt, xp_ref, xr_ref, gpre_ref, win_in_ref, lbl_ref, gon_ref, wout_in_ref, gpost_ref,
                        y_ref, sout_ref, win_ref, wout_ref, st_s, dec_s, raw_s, gate_s,
                        qe_s, kd_s, q8_s, k8_s, v_s, *w_s):
    g = pl.program_id(0)
    e = win_ref.shape[1] // 4
    heads = e // HEAD_DIM
    tm = xp_ref.shape[1]
    fin = g - 2

    @pl.when(g == 0)
    def _():
        win_ref[...] = win_in_ref[...]
        wout_ref[...] = wout_in_ref[...]
        for ref in (dec_s, raw_s, gate_s, qe_s, kd_s, q8_s, k8_s, v_s) + tuple(w_s):
            ref[...] = jnp.zeros_like(ref)

    @pl.when(jnp.logical_or(g < 2, lax.rem(fin, nt) == 0))
    def _():
        st_s[...] = jnp.zeros_like(st_s)

    lv = _level_matrix(HGRN_CHUNK)
    owned = [lv == i + 1 for i in range(len(w_s) + 1)]
    gon = gon_ref[...]
    chunks = tm // HGRN_CHUNK

    def chunk_scores(c, hd):
        rows = slice(c * HGRN_CHUNK, (c + 1) * HGRN_CHUNK)
        cols = slice(hd * HEAD_DIM, (hd + 1) * HEAD_DIM)
        a = jnp.where(owned[0], _dot_nt(q8_s[rows, cols], k8_s[rows, cols]), 0.0)
        for i, w_ref in enumerate(w_s):
            wv = w_ref[rows, cols]
            a = jnp.where(owned[i + 1], _dot_nt(wv, wv), a)
        return a.astype(BF16)

    def chunk_output(c, hd, a):
        rows = slice(c * HGRN_CHUNK, (c + 1) * HGRN_CHUNK)
        cols = slice(hd * HEAD_DIM, (hd + 1) * HEAD_DIM)
        vb = v_s[rows, cols]
        st = st_s[hd]
        o = _dot(a, vb) + _dot_nt(qe_s[rows, cols], st.astype(BF16))
        st3 = st.reshape(HEAD_DIM // 8, 8, HEAD_DIM) * dec_s[c, :, cols]
        st_s[hd] = st3.reshape(HEAD_DIM, HEAD_DIM) + _dot_tn(vb, kd_s[rows, cols])
        return (_rms(o, gon) * gate_s[rows, cols]).astype(BF16)

    h = _rms(xp_ref[0], gpre_ref[...]).astype(BF16)
    lb = _lower_bound(lbl_ref[...], j)
    out_blocks = [[None] * heads for _ in range(chunks)]
    groups = e // MXU_WIDTH

    def part_cols(cg):
        return [slice(part * e + cg * MXU_WIDTH, part * e + (cg + 1) * MXU_WIDTH) for part in range(4)]


    def stage(cg, c, q_raw, f_raw, v_raw, z_raw):
        gc = slice(cg * MXU_WIDTH, (cg + 1) * MXU_WIDTH)
        rows = slice(c * HGRN_CHUNK, (c + 1) * HGRN_CHUNK)
        q = _silu(q_raw) * (HEAD_DIM ** -0.5)
        lf2, k = _forget_gate(f_raw, lb[:, gc])
        v_s[rows, gc] = v_raw.astype(BF16)
        gate_s[rows, gc] = _silu(z_raw)
        qe, kd, q8, k8, w, decay = _chunk_operands(q, k, lf2)
        qe_s[rows, gc] = qe
        kd_s[rows, gc] = kd
        q8_s[rows, gc] = q8
        k8_s[rows, gc] = k8
        for w_ref, w_val in zip(w_s, w):
            w_ref[rows, gc] = w_val
        dec_s[c, :, gc] = jnp.broadcast_to(decay, (8, MXU_WIDTH))

    def project(cols):
        raw_s[:, cols] = _dot(h, win_ref[:, cols])

    def output():
        gated = jnp.concatenate([jnp.concatenate(blocks, axis=1) for blocks in out_blocks], axis=0)
        y = _dot(gated, wout_ref[...])
        y_ref[0] = xr_ref[0] + _rms(y, gpost_ref[...])

    heads_per_group = MXU_WIDTH // HEAD_DIM
    raws = {}
    for cg in range(groups):
        for c in range(chunks):
            rows = slice(c * HGRN_CHUNK, (c + 1) * HGRN_CHUNK)
            raws[cg, c] = [raw_s[rows, cols] for cols in part_cols(cg)]
    projections = [cols for cg in range(groups) for cols in part_cols(cg)]
    units = [(cg, c, hd) for cg in range(groups) for c in range(chunks)
             for hd in range(cg * heads_per_group, (cg + 1) * heads_per_group)]
    scores = {}
    for n in range(len(units) + SCORE_LOOKAHEAD):
        if n < len(units):
            _, c, hd = units[n]
            scores[c, hd] = chunk_scores(c, hd)
        if projections:
            project(projections.pop(0))
        if n >= SCORE_LOOKAHEAD:
            done = n - SCORE_LOOKAHEAD
            cg, c, hd = units[done]
            out_blocks[c][hd] = chunk_output(c, hd, scores.pop((c, hd)))
            if (done + 1) % heads_per_group == 0:
                stage(cg, c, *raws.pop((cg, c)))
    for cols in projections:
        project(cols)
    output()

    @pl.when(jnp.logical_and(g >= 2, lax.rem(fin, nt) == nt - 1))
    def _():
        for hd in range(heads):
            sout_ref[0, hd] = st_s[hd].T


def _hgrn_sample_pre_kernel(j, seq, x_ref, gpre_ref, win_ref, lbl_ref,
                            qe_ref, kd_ref, decp_ref, v_ref, gate_ref, oi_ref):
    e = win_ref.shape[1] // 4
    heads = e // HEAD_DIM
    x = x_ref[...]
    n = x.shape[0]
    h = _rms(x, gpre_ref[...]).astype(BF16)
    lb = _lower_bound(lbl_ref[...], j)
    gate_ref[...] = _silu(_dot(h, win_ref[:, 3 * e:4 * e]))

    t = lax.broadcasted_iota(jnp.int32, (n, n), 0)
    s = lax.broadcasted_iota(jnp.int32, (n, n), 1)
    sh = seq.bit_length() - 1
    same_seq_causal = ((t >> sh) == (s >> sh)) & (t >= s)
    row = lax.broadcasted_iota(jnp.int32, (n, HEAD_DIM), 0)
    pos = row & (seq - 1)

    for hd in range(heads):
        cols = slice(hd * HEAD_DIM, (hd + 1) * HEAD_DIM)
        q = _silu(_dot(h, win_ref[:, cols])) * (HEAD_DIM ** -0.5)
        lf2, k = _forget_gate(_dot(h, win_ref[:, e + hd * HEAD_DIM:e + (hd + 1) * HEAD_DIM]), lb[:, cols])
        v = _dot(h, win_ref[:, 2 * e + hd * HEAD_DIM:2 * e + (hd + 1) * HEAD_DIM])
        b = _cumsum_rows(lf2, seq)
        b_last = _group_allsum(lf2, seq)
        x_mid = b - _group_allsum(jnp.where(pos < seq // 2, lf2, 0.0), seq)
        a = _dot_nt((q * jnp.exp2(x_mid)).astype(BF16), (k * jnp.exp2(-x_mid)).astype(BF16))
        a = jnp.where(same_seq_causal, a, 0.0)
        vb = v.astype(BF16)
        oi_ref[:, cols] = _dot(a.astype(BF16), vb)
        qe_ref[:, cols] = q * jnp.exp2(b)
        kd_ref[:, cols] = k * jnp.exp2(b_last - b)
        v_ref[:, cols] = v
        dec = jnp.exp2(b_last)
        dsw = jnp.where((row & 7) < 4, pltpu.roll(dec, n - 4, axis=0), pltpu.roll(dec, 4, axis=0))
        hi = dsw.astype(BF16).astype(F32)
        mid = (dsw - hi).astype(BF16).astype(F32)
        lo = (dsw - hi - mid).astype(BF16).astype(F32)
        decp_ref[:, cols] = jnp.where(pos == 0, hi, jnp.where(pos == 1, mid, jnp.where(pos == 2, lo, 0.0)))


def _hgrn_sample_rec_kernel(qe_ref, kd_ref, decp_ref, v_ref, oi_ref, s0_ref, *rest):
    o_ref, s1_ref = rest[-2:]
    nseq = s0_ref.shape[0]
    heads = s0_ref.shape[1]
    row = lax.broadcasted_iota(jnp.int32, (8, HEAD_DIM), 0)
    pad = jnp.zeros((HEAD_DIM - 8, HEAD_DIM), F32)
    pad2 = jnp.zeros((HEAD_DIM - 8, 2 * HEAD_DIM), F32)

    def pair_body(p, carry):
        rows = pl.ds(pl.multiple_of(p * 8, 8), 8)
        for hd in range(heads):
            cols = slice(hd * HEAD_DIM, (hd + 1) * HEAD_DIM)
            qe8 = qe_ref[rows, cols].astype(BF16)
            kd8 = kd_ref[rows, cols]
            dp8 = decp_ref[rows, cols]
            v8 = v_ref[rows, cols]
            o8 = oi_ref[rows, cols]
            for half in range(2):
                mine = (row >= 4) if half else (row < 4)
                s0 = s0_ref[2 * p + half, hd]
                o_inter = _dot(qe8, s0.astype(BF16))
                o8 = o8 + jnp.where(mine, o_inter, 0.0)
                lhs = jnp.concatenate([jnp.where(mine, kd8, dp8), pad], axis=0).astype(BF16)
                ones = jnp.where(mine, 0.0, 1.0)
                rhs = jnp.concatenate(
                    [jnp.concatenate([jnp.where(mine, v8, 0.0), ones], axis=1), pad2], axis=0).astype(BF16)
                upd = _dot_tn(lhs, rhs)
                s1_ref[2 * p + half, hd] = upd[:, HEAD_DIM:] * s0 + upd[:, :HEAD_DIM]
            o_ref[rows, cols] = o8
        return carry

    lax.fori_loop(0, nseq // 2, pair_body, 0)


def _hgrn_sample_post_kernel(x_ref, o_ref, gate_ref, gon_ref, wout_ref, gpost_ref, y_ref):
    x = x_ref[...]
    g = _head_norm_gate(o_ref[...], gate_ref[...], gon_ref[...])
    y = _dot(g.astype(BF16), wout_ref[...])
    y_ref[...] = x + _rms(y, gpost_ref[...])


def _const_spec(shape):
    nd = len(shape)
    return pl.BlockSpec(shape, lambda *_: (0,) * nd)


def _layer_spec(arr, layer):
    nd = arr.ndim - 1
    return pl.BlockSpec((None,) + arr.shape[1:], lambda *_: (layer,) + (0,) * nd)


def _params(sem):
    return pltpu.CompilerParams(dimension_semantics=sem, vmem_limit_bytes=VMEM_LIMIT_BYTES)


def _conv_prompt(i, j, x, gpre, win, cw, wout, gpost):
    bsz, t, d = x.shape
    e = wout.shape[1]
    tm = CONV_PROMPT_TILE
    assert t % tm == 0 and tm % 8 == 0
    return pl.pallas_call(
        _conv_prompt_kernel,
        grid=(bsz, t // tm),
        in_specs=[pl.BlockSpec((1, tm, d), lambda b, s: (b, s, 0)),
                  _layer_spec(gpre, i), _layer_spec(win, i), _layer_spec(cw, j),
                  _layer_spec(wout, i), _layer_spec(gpost, i)],
        out_specs=[pl.BlockSpec((1, tm, d), lambda b, s: (b, s, 0)),
                   pl.BlockSpec((1, CONV_W - 1, e), lambda b, s: (b, 0, 0))],
        out_shape=[jax.ShapeDtypeStruct(x.shape, F32),
                   jax.ShapeDtypeStruct((bsz, CONV_W - 1, e), F32)],
        scratch_shapes=[pltpu.VMEM((8, e), F32),
                        pltpu.VMEM(win.shape[1:], BF16), pltpu.VMEM(wout.shape[1:], BF16)],
        compiler_params=_params(("arbitrary", "arbitrary")),
        name="conv_prompt",
    )(x, gpre, win, cw, wout, gpost)


def _conv_sample(i, j, xf, seq, buf, gpre, win, cw, wout, gpost):
    n, d = xf.shape
    nb = n // seq
    e = wout.shape[1]
    assert seq & (seq - 1) == 0 and seq >= CONV_W - 1
    zeros = lambda r: jnp.zeros((nb, r, e), F32)
    p1 = jnp.concatenate([buf[:, 1:2], zeros(seq - 1)], axis=1).reshape(n, e)
    p2 = jnp.concatenate([buf[:, 0:2], zeros(seq - 2)], axis=1).reshape(n, e)
    y, u = pl.pallas_call(
        functools.partial(_conv_sample_kernel, seq),
        grid=(1,),
        in_specs=[_const_spec(xf.shape), _const_spec(p1.shape), _const_spec(p2.shape),
                  _layer_spec(gpre, i), _layer_spec(win, i), _layer_spec(cw, j),
                  _layer_spec(wout, i), _layer_spec(gpost, i)],
        out_specs=[_const_spec((n, d)), _const_spec((n, e))],
        out_shape=[jax.ShapeDtypeStruct((n, d), F32), jax.ShapeDtypeStruct((n, e), F32)],
        compiler_params=_params(("arbitrary",)),
        name="conv_sample",
    )(xf, p1, p2, gpre, win, cw, wout, gpost)
    return y, u.reshape(nb, seq, e)[:, seq - (CONV_W - 1):]


def _hgrn_prompt(i, j, x, gpre, win, lb_logits, gon, wout, gpost):
    bsz, t, d = x.shape
    e = wout.shape[1]
    heads = e // HEAD_DIM
    tm = PROMPT_TILE
    assert t % tm == 0 and tm % HGRN_CHUNK == 0
    nt = t // tm
    n_tiles = bsz * nt
    n_bf16 = 5 + len(_level_sizes(HGRN_CHUNK))
    scratch = ([pltpu.VMEM(win.shape[1:], BF16), pltpu.VMEM(wout.shape[1:], BF16),
                pltpu.VMEM((heads, HEAD_DIM, HEAD_DIM), F32),
                pltpu.VMEM((tm // HGRN_CHUNK, 8, e), F32),
                pltpu.VMEM((tm, 4 * e), F32),
                pltpu.VMEM((tm, e), F32)]
               + [pltpu.VMEM((tm, e), BF16) for _ in range(n_bf16)])

    def projected(g):
        tile = jnp.minimum(g, n_tiles - 1)
        return tile // nt, tile % nt

    def finished(g):
        tile = jnp.maximum(g - 2, 0)
        return tile // nt, tile % nt

    return pl.pallas_call(
        functools.partial(_hgrn_prompt_kernel, j, nt),
        grid=(n_tiles + 2,),
        in_specs=[pl.BlockSpec((1, tm, d), lambda g: projected(g) + (0,)),
                  pl.BlockSpec((1, tm, d), lambda g: finished(g) + (0,)),
                  _layer_spec(gpre, i), _layer_spec(win, i), _const_spec(lb_logits.shape),
                  _layer_spec(gon, j), _layer_spec(wout, i), _layer_spec(gpost, i)],
        out_specs=[pl.BlockSpec((1, tm, d), lambda g: finished(g) + (0,)),
                   pl.BlockSpec((1, heads, HEAD_DIM, HEAD_DIM), lambda g: (finished(g)[0], 0, 0, 0))],
        out_shape=[jax.ShapeDtypeStruct(x.shape, F32),
                   jax.ShapeDtypeStruct((bsz, heads, HEAD_DIM, HEAD_DIM), F32)],
        scratch_shapes=scratch,
        compiler_params=_params(("arbitrary",)),
        name="hgrn_prompt",
    )(x, x, gpre, win, lb_logits, gon, wout, gpost)


def _hgrn_sample(i, j, xf, seq, state, states_out, gpre, win, lb_logits, gon, wout, gpost):
    n, d = xf.shape
    nb = n // seq
    e = wout.shape[1]
    heads = e // HEAD_DIM
    g = SAMPLE_SEQS_PER_STEP
    assert seq == 4 and nb % g == 0 and g % 2 == 0
    act = jax.ShapeDtypeStruct((n, e), F32)
    qe, kd, decp, v, gate, oi = pl.pallas_call(
        functools.partial(_hgrn_sample_pre_kernel, j, seq),
        grid=(1,),
        in_specs=[_const_spec(xf.shape), _layer_spec(gpre, i), _layer_spec(win, i),
                  _const_spec(lb_logits.shape)],
        out_specs=[_const_spec((n, e))] * 6,
        out_shape=[act] * 6,
        compiler_params=_params(("arbitrary",)),
        name="hgrn_sample_pre",
    )(xf, gpre, win, lb_logits)

    row_spec = pl.BlockSpec((g * seq, e), lambda s: (s, 0))
    st_spec = pl.BlockSpec((None, g, heads, HEAD_DIM, HEAD_DIM), lambda s: (j, s, 0, 0, 0))
    rec_in = [qe, kd, decp, v, oi, state]
    rec_specs = [row_spec] * 5 + [st_spec]
    aliases = {}
    if states_out is not None:
        rec_in.append(states_out)
        rec_specs.append(pl.BlockSpec(memory_space=pl.ANY))
        aliases = {len(rec_in) - 1: 1}
    o, states_out = pl.pallas_call(
        _hgrn_sample_rec_kernel,
        grid=(nb // g,),
        in_specs=rec_specs,
        out_specs=[row_spec, st_spec],
        out_shape=[act, jax.ShapeDtypeStruct(state.shape, F32)],
        input_output_aliases=aliases,
        compiler_params=_params(("arbitrary",)),
        name="hgrn_sample_rec",
    )(*rec_in)

    y = pl.pallas_call(
        _hgrn_sample_post_kernel,
        grid=(1,),
        in_specs=[_const_spec(xf.shape), _const_spec((n, e)), _const_spec((n, e)),
                  _layer_spec(gon, j), _layer_spec(wout, i), _layer_spec(gpost, i)],
        out_specs=_const_spec((n, d)),
        out_shape=jax.ShapeDtypeStruct((n, d), F32),
        compiler_params=_params(("arbitrary",)),
        name="hgrn_sample_post",
    )(xf, o, gate, gon, wout, gpost)
    return y, states_out


def kernel(x_prompt, x_sample, state_conv, state_hgrn, norm_pre, w_in, conv_w, hgrn_lb_logits,
           hgrn_onorm, w_out, norm_post):
    depth = w_in.shape[0]
    win_b = w_in.astype(BF16)
    wout_b = w_out.astype(BF16)
    gpre = norm_pre.astype(F32)[:, None, :]
    gpost = norm_post.astype(F32)[:, None, :]
    gon = hgrn_onorm.astype(F32)[:, None, :]
    lb_logits = hgrn_lb_logits.astype(F32)
    state_hgrn = state_hgrn.astype(F32)
    nb, seq, d = x_sample.shape
    xp, xs = x_prompt, x_sample.reshape(nb * seq, d)
    conv_p, conv_s, hgrn_p, hgrn_s = [], [], [], None
    for i in range(depth):
        j = i // 2
        if i % 2 == 0:
            xp, bp = _conv_prompt(i, j, xp, gpre, win_b, conv_w, wout_b, gpost)
            xs, bs = _conv_sample(i, j, xs, seq, state_conv[j], gpre, win_b, conv_w, wout_b, gpost)
            conv_p.append(bp)
            conv_s.append(bs)
        else:
            xp, sp = _hgrn_prompt(i, j, xp, gpre, win_b, lb_logits, gon, wout_b, gpost)
            xs, hgrn_s = _hgrn_sample(i, j, xs, seq, state_hgrn, hgrn_s, gpre, win_b, lb_logits, gon,
                                      wout_b, gpost)
            hgrn_p.append(sp)
    return (xp, xs.reshape(nb, seq, d), jnp.stack(conv_p), jnp.stack(conv_s), jnp.stack(hgrn_p), hgrn_s)
```

```python
import functools
import math

import jax
import jax.numpy as jnp
from jax import lax
from jax.experimental import pallas as pl
from jax.experimental.pallas import tpu as pltpu

F32 = jnp.float32
BF16 = jnp.bfloat16

RMS_EPS = 1e-6
LOG_F_FLOOR = -20.0
LOG2_E = math.log2(math.e)
HEAD_DIM = 128
MXU_WIDTH = 256
CONV_W = 3
HGRN_CHUNK = 128
BASE_BLOCK = 8
SCORE_LOOKAHEAD = 2
PROMPT_TILE = 256
CONV_PROMPT_TILE = 512
SAMPLE_SEQS_PER_STEP = 64
SAMPLE_PAIRS_PER_ITER = 8
VMEM_LIMIT_BYTES = 56 * 1024 * 1024


def _rms(x, g):
    ms = jnp.mean(x * x, axis=-1, keepdims=True)
    return x * lax.rsqrt(ms + RMS_EPS) * g


def _silu(x):
    return x * (1.0 / (1.0 + jnp.exp2(x * (-LOG2_E))))


def _dot(a, b):
    return jnp.dot(a, b, preferred_element_type=F32)


def _dot_nt(a, b):
    return lax.dot_general(a, b, (((1,), (1,)), ((), ())), preferred_element_type=F32)


def _dot_tn(a, b):
    return lax.dot_general(a, b, (((0,), (0,)), ((), ())), preferred_element_type=F32)


def _lower_bound(logits, j):
    m = jnp.max(logits, axis=0, keepdims=True)
    e = jnp.exp(logits - m)
    p = e / jnp.sum(e, axis=0, keepdims=True)
    if j == 0:
        return jnp.zeros_like(p[0:1])
    return jnp.sum(p[1:j + 1], axis=0, keepdims=True)


def _forget_gate(fpre, lb):
    e = jnp.exp2(jnp.abs(fpre) * (-LOG2_E))
    r = 1.0 / (1.0 + e)
    er = e * r
    pos = fpre >= 0
    sig = jnp.where(pos, r, er)
    nsig = jnp.where(pos, er, r)
    f = lb + (1.0 - lb) * sig
    log2_f = jnp.maximum(jnp.log(f) * LOG2_E, LOG_F_FLOOR * LOG2_E)
    k = jnp.minimum((1.0 - lb) * nsig, 1.0 - math.exp(LOG_F_FLOOR))
    return log2_f, k


def _cumsum_rows(x, group):
    pos = lax.broadcasted_iota(jnp.int32, x.shape, 0) & (group - 1)
    s = 1
    while s < group:
        x = x + jnp.where(pos >= s, pltpu.roll(x, s, axis=0), 0.0)
        s *= 2
    return x


def _group_allsum(x, group):
    n = x.shape[0]
    pos = lax.broadcasted_iota(jnp.int32, x.shape, 0) & (group - 1)
    s = 1
    while s < group:
        partner = jnp.where((pos & s) == 0, pltpu.roll(x, n - s, axis=0), pltpu.roll(x, s, axis=0))
        x = x + partner
        s *= 2
    return x


def _level_matrix(n):
    t = lax.broadcasted_iota(jnp.int32, (n, n), 0)
    s = lax.broadcasted_iota(jnp.int32, (n, n), 1)
    sh = BASE_BLOCK.bit_length() - 1
    lv = jnp.where(((t >> sh) == (s >> sh)) & (t >= s), 1, 0)
    c, i = 2 * BASE_BLOCK, 2
    while c <= n:
        half, sh = c // 2, sh + 1
        own = ((t >> sh) == (s >> sh)) & ((t & half) != 0) & ((s & half) == 0)
        lv = jnp.where(own, i, lv)
        c, i = 2 * c, i + 1
    return lv


def _level_sizes(chunk):
    sizes, c = [], 2 * BASE_BLOCK
    while c <= chunk:
        sizes.append(c)
        c *= 2
    return sizes


def _chunk_operands(q, k, lf2):
    c = q.shape[0]
    local = _cumsum_rows(lf2, BASE_BLOCK)
    b_parts, x_parts, carry = [], [], None
    for g in range(c // BASE_BLOCK):
        blk = local[g * BASE_BLOCK:(g + 1) * BASE_BLOCK]
        x_parts.append(blk - blk[BASE_BLOCK // 2 - 1:BASE_BLOCK // 2])
        if carry is not None:
            blk = blk + carry
        b_parts.append(blk)
        carry = blk[BASE_BLOCK - 1:BASE_BLOCK]
    b = jnp.concatenate(b_parts, axis=0)
    x = jnp.concatenate(x_parts, axis=0)
    b_last = carry
    qe = (q * jnp.exp2(b)).astype(BF16)
    kd = (k * jnp.exp2(b_last - b)).astype(BF16)
    q8 = (q * jnp.exp2(x)).astype(BF16)
    k8 = (k * jnp.exp2(-x)).astype(BF16)
    w = []
    for size in _level_sizes(c):
        half = size // 2
        parts = []
        for s0 in range(0, c, size):
            r = b[s0 + half - 1:s0 + half]
            lo, hi = slice(s0, s0 + half), slice(s0 + half, s0 + size)
            parts.append(k[lo] * jnp.exp2(r - b[lo]))
            parts.append(q[hi] * jnp.exp2(b[hi] - r))
        w.append(jnp.concatenate(parts, axis=0).astype(BF16))
    return qe, kd, q8, k8, w, jnp.exp2(b_last)


def _conv_mix(h, win_ref, cw_ref, prev1, prev2, pos):
    e = win_ref.shape[1] // 4
    v = _dot(h, win_ref[:, 0:e])
    cg = _dot(h, win_ref[:, 2 * e:3 * e])
    u = cg * v
    u1 = jnp.where(pos >= 1, pltpu.roll(u, 1, axis=0), prev1)
    u2 = jnp.where(pos >= 2, pltpu.roll(u, 2, axis=0), prev2)
    conv = cw_ref[0:1, :] * u2 + cw_ref[1:2, :] * u1 + cw_ref[2:3, :] * u
    bg = _dot(h, win_ref[:, e:2 * e])
    z = _dot(h, win_ref[:, 3 * e:4 * e])
    return bg * conv * _silu(z), u


def _conv_prompt_kernel(x_ref, gpre_ref, win_in_ref, cw_ref, wout_in_ref, gpost_ref,
                        y_ref, buf_ref, tail_ref, win_ref, wout_ref):
    @pl.when(jnp.logical_and(pl.program_id(0) == 0, pl.program_id(1) == 0))
    def _():
        win_ref[...] = win_in_ref[...]
        wout_ref[...] = wout_in_ref[...]

    @pl.when(pl.program_id(1) == 0)
    def _():
        tail_ref[...] = jnp.zeros_like(tail_ref)

    x = x_ref[0]
    tm = x.shape[0]
    h = _rms(x, gpre_ref[...]).astype(BF16)
    pos = lax.broadcasted_iota(jnp.int32, (tm, win_ref.shape[1] // 4), 0)
    tail = tail_ref[...]
    prev1 = tail[7:8, :]
    prev2 = jnp.where(pos == 0, tail[6:7, :], tail[7:8, :])
    g, u = _conv_mix(h, win_ref, cw_ref, prev1, prev2, pos)
    y = _dot(g.astype(BF16), wout_ref[...])
    y_ref[0] = x + _rms(y, gpost_ref[...])
    tail_ref[...] = u[tm - 8:tm, :]
    buf_ref[0] = u[tm - (CONV_W - 1):tm, :]


def _conv_sample_kernel(seq, x_ref, p1_ref, p2_ref, gpre_ref, win_ref, cw_ref, wout_ref, gpost_ref,
                        y_ref, u_ref):
    x = x_ref[...]
    h = _rms(x, gpre_ref[...]).astype(BF16)
    pos = lax.broadcasted_iota(jnp.int32, p1_ref.shape, 0) & (seq - 1)
    g, u = _conv_mix(h, win_ref, cw_ref, p1_ref[...], p2_ref[...], pos)
    y = _dot(g.astype(BF16), wout_ref[...])
    y_ref[...] = x + _rms(y, gpost_ref[...])
    u_ref[...] = u


def _hgrn_prompt_kernel(j, nt, xp_ref, xr_ref, gpre_ref, win_in_ref, lbl_ref, gon_ref, wout_in_ref, gpost_ref,
                        y_ref, sout_ref, win_ref, wout_ref, st_s, dec_s, raw_s, gate_s,
                        qe_s, kd_s, q8_s, k8_s, v_s, *w_s):
    g = pl.program_id(0)
    e = win_ref.shape[1] // 4
    heads = e // HEAD_DIM
    tm = xp_ref.shape[1]
    fin = g - 2

    @pl.when(g == 0)
    def _():
        win_ref[...] = win_in_ref[...]
        wout_ref[...] = wout_in_ref[...]
        for ref in (dec_s, raw_s, gate_s, qe_s, kd_s, q8_s, k8_s, v_s) + tuple(w_s):
            ref[...] = jnp.zeros_like(ref)

    @pl.when(jnp.logical_or(g < 2, lax.rem(fin, nt) == 0))
    def _():
        st_s[...] = jnp.zeros_like(st_s)

    lv = _level_matrix(HGRN_CHUNK)
    owned = [lv == i + 1 for i in range(len(w_s) + 1)]
    gon = gon_ref[...]
    chunks = tm // HGRN_CHUNK

    def chunk_scores(c, hd):
        rows = slice(c * HGRN_CHUNK, (c + 1) * HGRN_CHUNK)
        cols = slice(hd * HEAD_DIM, (hd + 1) * HEAD_DIM)
        a = jnp.where(owned[0], _dot_nt(q8_s[rows, cols], k8_s[rows, cols]), 0.0)
        for i, w_ref in enumerate(w_s):
            wv = w_ref[rows, cols]
            a = jnp.where(owned[i + 1], _dot_nt(wv, wv), a)
        return a.astype(BF16)

    def chunk_output(c, hd, a):
        rows = slice(c * HGRN_CHUNK, (c + 1) * HGRN_CHUNK)
        cols = slice(hd * HEAD_DIM, (hd + 1) * HEAD_DIM)
        vb = v_s[rows, cols]
        st = st_s[hd]
        o = _dot(a, vb) + _dot_nt(qe_s[rows, cols], st.astype(BF16))
        st3 = st.reshape(HEAD_DIM // 8, 8, HEAD_DIM) * dec_s[c, :, cols]
        st_s[hd] = st3.reshape(HEAD_DIM, HEAD_DIM) + _dot_tn(vb, kd_s[rows, cols])
        return (_rms(o, gon) * gate_s[rows, cols]).astype(BF16)

    h = _rms(xp_ref[0], gpre_ref[...]).astype(BF16)
    lb = _lower_bound(lbl_ref[...], j)
    out_blocks = [[None] * heads for _ in range(chunks)]
    groups = e // MXU_WIDTH

    def part_cols(cg):
        return [slice(part * e + cg * MXU_WIDTH, part * e + (cg + 1) * MXU_WIDTH) for part in range(4)]


    def stage(cg, c, q_raw, f_raw, v_raw, z_raw):
        gc = slice(cg * MXU_WIDTH, (cg + 1) * MXU_WIDTH)
        rows = slice(c * HGRN_CHUNK, (c + 1) * HGRN_CHUNK)
        q = _silu(q_raw) * (HEAD_DIM ** -0.5)
        lf2, k = _forget_gate(f_raw, lb[:, gc])
        v_s[rows, gc] = v_raw.astype(BF16)
        gate_s[rows, gc] = _silu(z_raw)
        qe, kd, q8, k8, w, decay = _chunk_operands(q, k, lf2)
        qe_s[rows, gc] = qe
        kd_s[rows, gc] = kd
        q8_s[rows, gc] = q8
        k8_s[rows, gc] = k8
        for w_ref, w_val in zip(w_s, w):
            w_ref[rows, gc] = w_val
        dec_s[c, :, gc] = jnp.broadcast_to(decay, (8, MXU_WIDTH))

    def project(cols):
        raw_s[:, cols] = _dot(h, win_ref[:, cols])

    def output():
        gated = jnp.concatenate([jnp.concatenate(blocks, axis=1) for blocks in out_blocks], axis=0)
        y = _dot(gated, wout_ref[...])
        y_ref[0] = xr_ref[0] + _rms(y, gpost_ref[...])

    heads_per_group = MXU_WIDTH // HEAD_DIM
    raws = {}
    for cg in range(groups):
        for c in range(chunks):
            rows = slice(c * HGRN_CHUNK, (c + 1) * HGRN_CHUNK)
            raws[cg, c] = [raw_s[rows, cols] for cols in part_cols(cg)]
    projections = [cols for cg in range(groups) for cols in part_cols(cg)]
    units = [(cg, c, hd) for cg in range(groups) for c in range(chunks)
             for hd in range(cg * heads_per_group, (cg + 1) * heads_per_group)]
    scores = {}
    for n in range(len(units) + SCORE_LOOKAHEAD):
        if n < len(units):
            _, c, hd = units[n]
            scores[c, hd] = chunk_scores(c, hd)
        if projections:
            project(projections.pop(0))
        if n >= SCORE_LOOKAHEAD:
            done = n - SCORE_LOOKAHEAD
            cg, c, hd = units[done]
            out_blocks[c][hd] = chunk_output(c, hd, scores.pop((c, hd)))
            if (done + 1) % heads_per_group == 0:
                stage(cg, c, *raws.pop((cg, c)))
    for cols in projections:
        project(cols)
    output()

    @pl.when(jnp.logical_and(g >= 2, lax.rem(fin, nt) == nt - 1))
    def _():
        for hd in range(heads):
            sout_ref[0, hd] = st_s[hd].T


def _hgrn_sample_kernel(j, seq, x_ref, gpre_ref, wq_ref, wf_ref, wv_ref, wz_ref, lbl_ref, gon_ref,
                        wout_ref, gpost_ref, s0_ref, *rest):
    y_ref, s1_ref, h_s, qe_s, kd_s, decp_s, v_s, o_s, gate_s = rest[-9:]
    hd, blk = pl.program_id(0), pl.program_id(1)
    n = x_ref.shape[0]
    nseq = s0_ref.shape[0]
    heads = o_s.shape[0]

    @pl.when(jnp.logical_and(hd == 0, blk == 0))
    def _():
        h_s[...] = _rms(x_ref[...], gpre_ref[...]).astype(BF16)

    @pl.when(blk == 0)
    def _():
        h = h_s[...]
        lb = _lower_bound(lbl_ref[...], j)
        q = _silu(_dot(h, wq_ref[...])) * (HEAD_DIM ** -0.5)
        lf2, k = _forget_gate(_dot(h, wf_ref[...]), lb)
        v = _dot(h, wv_ref[...])
        gate_s[hd] = _silu(_dot(h, wz_ref[...]))
        t = lax.broadcasted_iota(jnp.int32, (n, n), 0)
        s = lax.broadcasted_iota(jnp.int32, (n, n), 1)
        sh = seq.bit_length() - 1
        same_seq_causal = ((t >> sh) == (s >> sh)) & (t >= s)
        row = lax.broadcasted_iota(jnp.int32, (n, HEAD_DIM), 0)
        pos = row & (seq - 1)
        b = _cumsum_rows(lf2, seq)
        b_last = _group_allsum(lf2, seq)
        x_mid = b - _group_allsum(jnp.where(pos < seq // 2, lf2, 0.0), seq)
        a = _dot_nt((q * jnp.exp2(x_mid)).astype(BF16), (k * jnp.exp2(-x_mid)).astype(BF16))
        a = jnp.where(same_seq_causal, a, 0.0)
        o_s[hd] = _dot(a.astype(BF16), v.astype(BF16))
        qe_s[...] = q * jnp.exp2(b)
        kd_s[...] = k * jnp.exp2(b_last - b)
        v_s[...] = v
        dec = jnp.exp2(b_last)
        dsw = jnp.where((row & 7) < 4, pltpu.roll(dec, n - 4, axis=0), pltpu.roll(dec, 4, axis=0))
        hi = dsw.astype(BF16).astype(F32)
        mid = (dsw - hi).astype(BF16).astype(F32)
        lo = (dsw - hi - mid).astype(BF16).astype(F32)
        decp_s[...] = jnp.where(pos == 0, hi, jnp.where(pos == 1, mid, jnp.where(pos == 2, lo, 0.0)))

    row = lax.broadcasted_iota(jnp.int32, (8, HEAD_DIM), 0)
    pad = jnp.zeros((HEAD_DIM - 8, HEAD_DIM), F32)
    pad2 = jnp.zeros((HEAD_DIM - 8, 2 * HEAD_DIM), F32)
    base = blk * (nseq * seq)

    def pairs_body(it, carry):
        for u in range(SAMPLE_PAIRS_PER_ITER):
            p = it * SAMPLE_PAIRS_PER_ITER + u
            rows = pl.ds(pl.multiple_of(base + p * 8, 8), 8)
            qe8 = qe_s[rows, :].astype(BF16)
            kd8 = kd_s[rows, :]
            dp8 = decp_s[rows, :]
            v8 = v_s[rows, :]
            o8 = o_s[hd, rows, :]
            for half in range(2):
                mine = (row >= 4) if half else (row < 4)
                s0 = s0_ref[2 * p + half]
                o8 = o8 + jnp.where(mine, _dot(qe8, s0.astype(BF16)), 0.0)
                lhs = jnp.concatenate([jnp.where(mine, kd8, dp8), pad], axis=0).astype(BF16)
                ones = jnp.where(mine, 0.0, 1.0)
                rhs = jnp.concatenate(
                    [jnp.concatenate([jnp.where(mine, v8, 0.0), ones], axis=1), pad2], axis=0).astype(BF16)
                upd = _dot_tn(lhs, rhs)
                s1_ref[2 * p + half] = upd[:, HEAD_DIM:] * s0 + upd[:, :HEAD_DIM]
            o_s[hd, rows, :] = o8
        return carry

    lax.fori_loop(0, nseq // (2 * SAMPLE_PAIRS_PER_ITER), pairs_body, 0)

    @pl.when(jnp.logical_and(hd == heads - 1, blk == pl.num_programs(1) - 1))
    def _():
        gon = gon_ref[...]
        gated = jnp.concatenate(
            [(_rms(o_s[i], gon) * gate_s[i]).astype(BF16) for i in range(heads)], axis=1)
        y = _dot(gated, wout_ref[...])
        y_ref[...] = x_ref[...] + _rms(y, gpost_ref[...])


def _const_spec(shape):
    nd = len(shape)
    return pl.BlockSpec(shape, lambda *_: (0,) * nd)


def _layer_spec(arr, layer):
    nd = arr.ndim - 1
    return pl.BlockSpec((None,) + arr.shape[1:], lambda *_: (layer,) + (0,) * nd)


def _params(sem):
    return pltpu.CompilerParams(dimension_semantics=sem, vmem_limit_bytes=VMEM_LIMIT_BYTES)


def _conv_prompt(i, j, x, gpre, win, cw, wout, gpost):
    bsz, t, d = x.shape
    e = wout.shape[1]
    tm = CONV_PROMPT_TILE
    assert t % tm == 0 and tm % 8 == 0
    return pl.pallas_call(
        _conv_prompt_kernel,
        grid=(bsz, t // tm),
        in_specs=[pl.BlockSpec((1, tm, d), lambda b, s: (b, s, 0)),
                  _layer_spec(gpre, i), _layer_spec(win, i), _layer_spec(cw, j),
                  _layer_spec(wout, i), _layer_spec(gpost, i)],
        out_specs=[pl.BlockSpec((1, tm, d), lambda b, s: (b, s, 0)),
                   pl.BlockSpec((1, CONV_W - 1, e), lambda b, s: (b, 0, 0))],
        out_shape=[jax.ShapeDtypeStruct(x.shape, F32),
                   jax.ShapeDtypeStruct((bsz, CONV_W - 1, e), F32)],
        scratch_shapes=[pltpu.VMEM((8, e), F32),
                        pltpu.VMEM(win.shape[1:], BF16), pltpu.VMEM(wout.shape[1:], BF16)],
        compiler_params=_params(("arbitrary", "arbitrary")),
        name="conv_prompt",
    )(x, gpre, win, cw, wout, gpost)


def _conv_sample(i, j, xf, seq, buf, gpre, win, cw, wout, gpost):
    n, d = xf.shape
    nb = n // seq
    e = wout.shape[1]
    assert seq & (seq - 1) == 0 and seq >= CONV_W - 1
    zeros = lambda r: jnp.zeros((nb, r, e), F32)
    p1 = jnp.concatenate([buf[:, 1:2], zeros(seq - 1)], axis=1).reshape(n, e)
    p2 = jnp.concatenate([buf[:, 0:2], zeros(seq - 2)], axis=1).reshape(n, e)
    y, u = pl.pallas_call(
        functools.partial(_conv_sample_kernel, seq),
        grid=(1,),
        in_specs=[_const_spec(xf.shape), _const_spec(p1.shape), _const_spec(p2.shape),
                  _layer_spec(gpre, i), _layer_spec(win, i), _layer_spec(cw, j),
                  _layer_spec(wout, i), _layer_spec(gpost, i)],
        out_specs=[_const_spec((n, d)), _const_spec((n, e))],
        out_shape=[jax.ShapeDtypeStruct((n, d), F32), jax.ShapeDtypeStruct((n, e), F32)],
        compiler_params=_params(("arbitrary",)),
        name="conv_sample",
    )(xf, p1, p2, gpre, win, cw, wout, gpost)
    return y, u.reshape(nb, seq, e)[:, seq - (CONV_W - 1):]


def _hgrn_prompt(i, j, x, gpre, win, lb_logits, gon, wout, gpost):
    bsz, t, d = x.shape
    e = wout.shape[1]
    heads = e // HEAD_DIM
    tm = PROMPT_TILE
    assert t % tm == 0 and tm % HGRN_CHUNK == 0
    nt = t // tm
    n_tiles = bsz * nt
    n_bf16 = 5 + len(_level_sizes(HGRN_CHUNK))
    scratch = ([pltpu.VMEM(win.shape[1:], BF16), pltpu.VMEM(wout.shape[1:], BF16),
                pltpu.VMEM((heads, HEAD_DIM, HEAD_DIM), F32),
                pltpu.VMEM((tm // HGRN_CHUNK, 8, e), F32),
                pltpu.VMEM((tm, 4 * e), F32),
                pltpu.VMEM((tm, e), F32)]
               + [pltpu.VMEM((tm, e), BF16) for _ in range(n_bf16)])

    def projected(g):
        tile = jnp.minimum(g, n_tiles - 1)
        return tile // nt, tile % nt

    def finished(g):
        tile = jnp.maximum(g - 2, 0)
        return tile // nt, tile % nt

    return pl.pallas_call(
        functools.partial(_hgrn_prompt_kernel, j, nt),
        grid=(n_tiles + 2,),
        in_specs=[pl.BlockSpec((1, tm, d), lambda g: projected(g) + (0,)),
                  pl.BlockSpec((1, tm, d), lambda g: finished(g) + (0,)),
                  _layer_spec(gpre, i), _layer_spec(win, i), _const_spec(lb_logits.shape),
                  _layer_spec(gon, j), _layer_spec(wout, i), _layer_spec(gpost, i)],
        out_specs=[pl.BlockSpec((1, tm, d), lambda g: finished(g) + (0,)),
                   pl.BlockSpec((1, heads, HEAD_DIM, HEAD_DIM), lambda g: (finished(g)[0], 0, 0, 0))],
        out_shape=[jax.ShapeDtypeStruct(x.shape, F32),
                   jax.ShapeDtypeStruct((bsz, heads, HEAD_DIM, HEAD_DIM), F32)],
        scratch_shapes=scratch,
        compiler_params=_params(("arbitrary",)),
        name="hgrn_prompt",
    )(x, x, gpre, win, lb_logits, gon, wout, gpost)


def _hgrn_sample(i, j, xf, seq, state, states_out, gpre, win, lb_logits, gon, wout, gpost):
    n, d = xf.shape
    nb = n // seq
    e = wout.shape[1]
    heads = e // HEAD_DIM
    g = SAMPLE_SEQS_PER_STEP
    assert seq == 4 and nb % g == 0 and g % 2 == 0

    def head_cols(part):
        return pl.BlockSpec((None, d, HEAD_DIM), lambda hd, blk: (i, 0, part * heads + hd))

    st_spec = pl.BlockSpec((None, g, None, HEAD_DIM, HEAD_DIM), lambda hd, blk: (j, blk, hd, 0, 0))
    args = [xf, gpre, win, win, win, win, lb_logits, gon, wout, gpost, state]
    in_specs = [_const_spec(xf.shape), _layer_spec(gpre, i)] + [head_cols(part) for part in range(4)] + [
        pl.BlockSpec((lb_logits.shape[0], HEAD_DIM), lambda hd, blk: (0, hd)),
        _layer_spec(gon, j), _layer_spec(wout, i), _layer_spec(gpost, i), st_spec]
    aliases = {}
    if states_out is not None:
        args.append(states_out)
        in_specs.append(pl.BlockSpec(memory_space=pl.ANY))
        aliases = {len(args) - 1: 1}
    y, states_out = pl.pallas_call(
        functools.partial(_hgrn_sample_kernel, j, seq),
        grid=(heads, nb // g),
        in_specs=in_specs,
        out_specs=[_const_spec((n, d)), st_spec],
        out_shape=[jax.ShapeDtypeStruct((n, d), F32), jax.ShapeDtypeStruct(state.shape, F32)],
        scratch_shapes=[pltpu.VMEM((n, d), BF16)]
                       + [pltpu.VMEM((n, HEAD_DIM), F32) for _ in range(4)]
                       + [pltpu.VMEM((heads, n, HEAD_DIM), F32) for _ in range(2)],
        input_output_aliases=aliases,
        compiler_params=_params(("arbitrary", "arbitrary")),
        name="hgrn_sample",
    )(*args)
    return y, states_out


def kernel(x_prompt, x_sample, state_conv, state_hgrn, norm_pre, w_in, conv_w, hgrn_lb_logits,
           hgrn_onorm, w_out, norm_post):
    depth = w_in.shape[0]
    win_b = w_in.astype(BF16)
    wout_b = w_out.astype(BF16)
    gpre = norm_pre.astype(F32)[:, None, :]
    gpost = norm_post.astype(F32)[:, None, :]
    gon = hgrn_onorm.astype(F32)[:, None, :]
    lb_logits = hgrn_lb_logits.astype(F32)
    state_hgrn = state_hgrn.astype(F32)
    nb, seq, d = x_sample.shape
    xp, xs = x_prompt, x_sample.reshape(nb * seq, d)
    conv_p, conv_s, hgrn_p, hgrn_s = [], [], [], None
    for i in range(depth):
        j = i // 2
        if i % 2 == 0:
            xp, bp = _conv_prompt(i, j, xp, gpre, win_b, conv_w, wout_b, gpost)
            xs, bs = _conv_sample(i, j, xs, seq, state_conv[j], gpre, win_b, conv_w, wout_b, gpost)
            conv_p.append(bp)
            conv_s.append(bs)
        else:
            xp, sp = _hgrn_prompt(i, j, xp, gpre, win_b, lb_logits, gon, wout_b, gpost)
            xs, hgrn_s = _hgrn_sample(i, j, xs, seq, state_hgrn, hgrn_s, gpre, win_b, lb_logits, gon,
                                      wout_b, gpost)
            hgrn_p.append(sp)
    return (xp, xs.reshape(nb, seq, d), jnp.stack(conv_p), jnp.stack(conv_s), jnp.stack(hgrn_p), hgrn_s)
```

```python
import functools
import math

import jax
import jax.numpy as jnp
from jax import lax
from jax.experimental import pallas as pl
from jax.experimental.pallas import tpu as pltpu

F32 = jnp.float32
BF16 = jnp.bfloat16

RMS_EPS = 1e-6
LOG_F_FLOOR = -20.0
LOG2_E = math.log2(math.e)
HEAD_DIM = 128
MXU_WIDTH = 256
CONV_W = 3
HGRN_CHUNK = 128
BASE_BLOCK = 8
SCORE_LOOKAHEAD = 2
PROMPT_TILE = 256
CONV_PROMPT_TILE = 512
SAMPLE_SEQS_PER_STEP = 128
SAMPLE_PAIRS_PER_ITER = 8
VMEM_LIMIT_BYTES = 56 * 1024 * 1024


def _rms(x, g):
    ms = jnp.mean(x * x, axis=-1, keepdims=True)
    return x * lax.rsqrt(ms + RMS_EPS) * g


def _silu(x):
    return x * (1.0 / (1.0 + jnp.exp2(x * (-LOG2_E))))


def _dot(a, b):
    return jnp.dot(a, b, preferred_element_type=F32)


def _dot_nt(a, b):
    return lax.dot_general(a, b, (((1,), (1,)), ((), ())), preferred_element_type=F32)


def _dot_tn(a, b):
    return lax.dot_general(a, b, (((0,), (0,)), ((), ())), preferred_element_type=F32)


def _lower_bound(logits, j):
    m = jnp.max(logits, axis=0, keepdims=True)
    e = jnp.exp(logits - m)
    p = e / jnp.sum(e, axis=0, keepdims=True)
    if j == 0:
        return jnp.zeros_like(p[0:1])
    return jnp.sum(p[1:j + 1], axis=0, keepdims=True)


def _forget_gate(fpre, lb):
    e = jnp.exp2(jnp.abs(fpre) * (-LOG2_E))
    r = 1.0 / (1.0 + e)
    er = e * r
    pos = fpre >= 0
    sig = jnp.where(pos, r, er)
    nsig = jnp.where(pos, er, r)
    f = lb + (1.0 - lb) * sig
    log2_f = jnp.maximum(jnp.log(f) * LOG2_E, LOG_F_FLOOR * LOG2_E)
    k = jnp.minimum((1.0 - lb) * nsig, 1.0 - math.exp(LOG_F_FLOOR))
    return log2_f, k


def _cumsum_rows(x, group):
    pos = lax.broadcasted_iota(jnp.int32, x.shape, 0) & (group - 1)
    s = 1
    while s < group:
        x = x + jnp.where(pos >= s, pltpu.roll(x, s, axis=0), 0.0)
        s *= 2
    return x


def _group_allsum(x, group):
    n = x.shape[0]
    pos = lax.broadcasted_iota(jnp.int32, x.shape, 0) & (group - 1)
    s = 1
    while s < group:
        partner = jnp.where((pos & s) == 0, pltpu.roll(x, n - s, axis=0), pltpu.roll(x, s, axis=0))
        x = x + partner
        s *= 2
    return x


def _level_matrix(n):
    t = lax.broadcasted_iota(jnp.int32, (n, n), 0)
    s = lax.broadcasted_iota(jnp.int32, (n, n), 1)
    sh = BASE_BLOCK.bit_length() - 1
    lv = jnp.where(((t >> sh) == (s >> sh)) & (t >= s), 1, 0)
    c, i = 2 * BASE_BLOCK, 2
    while c <= n:
        half, sh = c // 2, sh + 1
        own = ((t >> sh) == (s >> sh)) & ((t & half) != 0) & ((s & half) == 0)
        lv = jnp.where(own, i, lv)
        c, i = 2 * c, i + 1
    return lv


def _level_sizes(chunk):
    sizes, c = [], 2 * BASE_BLOCK
    while c <= chunk:
        sizes.append(c)
        c *= 2
    return sizes


def _chunk_operands(q, k, lf2):
    c = q.shape[0]
    local = _cumsum_rows(lf2, BASE_BLOCK)
    b_parts, x_parts, carry = [], [], None
    for g in range(c // BASE_BLOCK):
        blk = local[g * BASE_BLOCK:(g + 1) * BASE_BLOCK]
        x_parts.append(blk - blk[BASE_BLOCK // 2 - 1:BASE_BLOCK // 2])
        if carry is not None:
            blk = blk + carry
        b_parts.append(blk)
        carry = blk[BASE_BLOCK - 1:BASE_BLOCK]
    b = jnp.concatenate(b_parts, axis=0)
    x = jnp.concatenate(x_parts, axis=0)
    b_last = carry
    qe = (q * jnp.exp2(b)).astype(BF16)
    kd = (k * jnp.exp2(b_last - b)).astype(BF16)
    q8 = (q * jnp.exp2(x)).astype(BF16)
    k8 = (k * jnp.exp2(-x)).astype(BF16)
    w = []
    for size in _level_sizes(c):
        half = size // 2
        parts = []
        for s0 in range(0, c, size):
            r = b[s0 + half - 1:s0 + half]
            lo, hi = slice(s0, s0 + half), slice(s0 + half, s0 + size)
            parts.append(k[lo] * jnp.exp2(r - b[lo]))
            parts.append(q[hi] * jnp.exp2(b[hi] - r))
        w.append(jnp.concatenate(parts, axis=0).astype(BF16))
    return qe, kd, q8, k8, w, jnp.exp2(b_last)


def _conv_mix(h, win_ref, cw_ref, prev1, prev2, pos):
    e = win_ref.shape[1] // 4
    v = _dot(h, win_ref[:, 0:e])
    cg = _dot(h, win_ref[:, 2 * e:3 * e])
    u = cg * v
    u1 = jnp.where(pos >= 1, pltpu.roll(u, 1, axis=0), prev1)
    u2 = jnp.where(pos >= 2, pltpu.roll(u, 2, axis=0), prev2)
    conv = cw_ref[0:1, :] * u2 + cw_ref[1:2, :] * u1 + cw_ref[2:3, :] * u
    bg = _dot(h, win_ref[:, e:2 * e])
    z = _dot(h, win_ref[:, 3 * e:4 * e])
    return bg * conv * _silu(z), u


def _conv_prompt_kernel(x_ref, gpre_ref, win_in_ref, cw_ref, wout_in_ref, gpost_ref,
                        y_ref, buf_ref, tail_ref, win_ref, wout_ref):
    @pl.when(jnp.logical_and(pl.program_id(0) == 0, pl.program_id(1) == 0))
    def _():
        win_ref[...] = win_in_ref[...].astype(BF16)
        wout_ref[...] = wout_in_ref[...].astype(BF16)

    @pl.when(pl.program_id(1) == 0)
    def _():
        tail_ref[...] = jnp.zeros_like(tail_ref)

    x = x_ref[0]
    tm = x.shape[0]
    h = _rms(x, gpre_ref[...]).astype(BF16)
    pos = lax.broadcasted_iota(jnp.int32, (tm, win_ref.shape[1] // 4), 0)
    tail = tail_ref[...]
    prev1 = tail[7:8, :]
    prev2 = jnp.where(pos == 0, tail[6:7, :], tail[7:8, :])
    g, u = _conv_mix(h, win_ref, cw_ref, prev1, prev2, pos)
    y = _dot(g.astype(BF16), wout_ref[...])
    y_ref[0] = x + _rms(y, gpost_ref[...])
    tail_ref[...] = u[tm - 8:tm, :]
    buf_ref[0] = u[tm - (CONV_W - 1):tm, :]


def _conv_sample_kernel(seq, x_ref, p1_ref, p2_ref, gpre_ref, win_in_ref, cw_ref, wout_in_ref, gpost_ref,
                        y_ref, u_ref, win_ref, wout_ref):
    win_ref[...] = win_in_ref[...].astype(BF16)
    wout_ref[...] = wout_in_ref[...].astype(BF16)
    x = x_ref[...]
    h = _rms(x, gpre_ref[...]).astype(BF16)
    pos = lax.broadcasted_iota(jnp.int32, p1_ref.shape, 0) & (seq - 1)
    g, u = _conv_mix(h, win_ref, cw_ref, p1_ref[...], p2_ref[...], pos)
    y = _dot(g.astype(BF16), wout_ref[...])
    y_ref[...] = x + _rms(y, gpost_ref[...])
    u_ref[...] = u


def _hgrn_prompt_kernel(j, nt, xp_ref, xr_ref, gpre_ref, win_in_ref, lbl_ref, gon_ref, wout_in_ref, gpost_ref,
                        y_ref, sout_ref, win_ref, wout_ref, st_s, dec_s, raw_s, gate_s,
                        qe_s, kd_s, q8_s, k8_s, v_s, *w_s):
    g = pl.program_id(0)
    e = win_ref.shape[1] // 4
    heads = e // HEAD_DIM
    tm = xp_ref.shape[1]
    fin = g - 2

    @pl.when(g == 0)
    def _():
        win_ref[...] = win_in_ref[...].astype(BF16)
        wout_ref[...] = wout_in_ref[...].astype(BF16)
        for ref in (dec_s, raw_s, gate_s, qe_s, kd_s, q8_s, k8_s, v_s) + tuple(w_s):
            ref[...] = jnp.zeros_like(ref)

    @pl.when(jnp.logical_or(g < 2, lax.rem(fin, nt) == 0))
    def _():
        st_s[...] = jnp.zeros_like(st_s)

    lv = _level_matrix(HGRN_CHUNK)
    owned = [lv == i + 1 for i in range(len(w_s) + 1)]
    gon = gon_ref[...]
    chunks = tm // HGRN_CHUNK

    def chunk_scores(c, hd):
        rows = slice(c * HGRN_CHUNK, (c + 1) * HGRN_CHUNK)
        cols = slice(hd * HEAD_DIM, (hd + 1) * HEAD_DIM)
        a = jnp.where(owned[0], _dot_nt(q8_s[rows, cols], k8_s[rows, cols]), 0.0)
        for i, w_ref in enumerate(w_s):
            wv = w_ref[rows, cols]
            a = jnp.where(owned[i + 1], _dot_nt(wv, wv), a)
        return a.astype(BF16)

    def chunk_output(c, hd, a):
        rows = slice(c * HGRN_CHUNK, (c + 1) * HGRN_CHUNK)
        cols = slice(hd * HEAD_DIM, (hd + 1) * HEAD_DIM)
        vb = v_s[rows, cols]
        st = st_s[hd]
        o = _dot(a, vb) + _dot_nt(qe_s[rows, cols], st.astype(BF16))
        st3 = st.reshape(HEAD_DIM // 8, 8, HEAD_DIM) * dec_s[c, :, cols]
        st_s[hd] = st3.reshape(HEAD_DIM, HEAD_DIM) + _dot_tn(vb, kd_s[rows, cols])
        return (_rms(o, gon) * gate_s[rows, cols]).astype(BF16)

    h = _rms(xp_ref[0], gpre_ref[...]).astype(BF16)
    lb = _lower_bound(lbl_ref[...], j)
    out_blocks = [[None] * heads for _ in range(chunks)]
    groups = e // MXU_WIDTH

    def part_cols(cg):
        return [slice(part * e + cg * MXU_WIDTH, part * e + (cg + 1) * MXU_WIDTH) for part in range(4)]


    def stage(cg, c, q_raw, f_raw, v_raw, z_raw):
        gc = slice(cg * MXU_WIDTH, (cg + 1) * MXU_WIDTH)
        rows = slice(c * HGRN_CHUNK, (c + 1) * HGRN_CHUNK)
        q = _silu(q_raw) * (HEAD_DIM ** -0.5)
        lf2, k = _forget_gate(f_raw, lb[:, gc])
        v_s[rows, gc] = v_raw.astype(BF16)
        gate_s[rows, gc] = _silu(z_raw)
        qe, kd, q8, k8, w, decay = _chunk_operands(q, k, lf2)
        qe_s[rows, gc] = qe
        kd_s[rows, gc] = kd
        q8_s[rows, gc] = q8
        k8_s[rows, gc] = k8
        for w_ref, w_val in zip(w_s, w):
            w_ref[rows, gc] = w_val
        dec_s[c, :, gc] = jnp.broadcast_to(decay, (8, MXU_WIDTH))

    def project(cols):
        raw_s[:, cols] = _dot(h, win_ref[:, cols])

    def output():
        gated = jnp.concatenate([jnp.concatenate(blocks, axis=1) for blocks in out_blocks], axis=0)
        y = _dot(gated, wout_ref[...])
        y_ref[0] = xr_ref[0] + _rms(y, gpost_ref[...])

    heads_per_group = MXU_WIDTH // HEAD_DIM
    raws = {}
    for cg in range(groups):
        for c in range(chunks):
            rows = slice(c * HGRN_CHUNK, (c + 1) * HGRN_CHUNK)
            raws[cg, c] = [raw_s[rows, cols] for cols in part_cols(cg)]
    projections = [cols for cg in range(groups) for cols in part_cols(cg)]
    units = [(cg, c, hd) for cg in range(groups) for c in range(chunks)
             for hd in range(cg * heads_per_group, (cg + 1) * heads_per_group)]
    scores = {}
    for n in range(len(units) + SCORE_LOOKAHEAD):
        if n < len(units):
            _, c, hd = units[n]
            scores[c, hd] = chunk_scores(c, hd)
        if projections:
            project(projections.pop(0))
        if n >= SCORE_LOOKAHEAD:
            done = n - SCORE_LOOKAHEAD
            cg, c, hd = units[done]
            out_blocks[c][hd] = chunk_output(c, hd, scores.pop((c, hd)))
            if (done + 1) % heads_per_group == 0:
                stage(cg, c, *raws.pop((cg, c)))
    for cols in projections:
        project(cols)
    output()

    @pl.when(jnp.logical_and(g >= 2, lax.rem(fin, nt) == nt - 1))
    def _():
        for hd in range(heads):
            sout_ref[0, hd] = st_s[hd].T


def _hgrn_sample_kernel(j, seq, x_ref, gpre_ref, wq_ref, wf_ref, wv_ref, wz_ref, lbl_ref, gon_ref,
                        wout_ref, gpost_ref, s0_ref, *rest):
    y_ref, s1_ref, h_s, qe_s, kd_s, decp_s, v_s, o_s, gate_s = rest[-9:]
    hd, blk = pl.program_id(0), pl.program_id(1)
    n = x_ref.shape[0]
    nseq = s0_ref.shape[0]
    heads = o_s.shape[0]

    @pl.when(jnp.logical_and(hd == 0, blk == 0))
    def _():
        h_s[...] = _rms(x_ref[...], gpre_ref[...]).astype(BF16)

    @pl.when(blk == 0)
    def _():
        h = h_s[...]
        lb = _lower_bound(lbl_ref[...], j)
        q = _silu(_dot(h, wq_ref[...].astype(BF16))) * (HEAD_DIM ** -0.5)
        lf2, k = _forget_gate(_dot(h, wf_ref[...].astype(BF16)), lb)
        v = _dot(h, wv_ref[...].astype(BF16))
        gate_s[hd] = _silu(_dot(h, wz_ref[...].astype(BF16)))
        t = lax.broadcasted_iota(jnp.int32, (n, n), 0)
        s = lax.broadcasted_iota(jnp.int32, (n, n), 1)
        sh = seq.bit_length() - 1
        same_seq_causal = ((t >> sh) == (s >> sh)) & (t >= s)
        row = lax.broadcasted_iota(jnp.int32, (n, HEAD_DIM), 0)
        pos = row & (seq - 1)
        b = _cumsum_rows(lf2, seq)
        b_last = _group_allsum(lf2, seq)
        x_mid = b - _group_allsum(jnp.where(pos < seq // 2, lf2, 0.0), seq)
        a = _dot_nt((q * jnp.exp2(x_mid)).astype(BF16), (k * jnp.exp2(-x_mid)).astype(BF16))
        a = jnp.where(same_seq_causal, a, 0.0)
        o_s[hd] = _dot(a.astype(BF16), v.astype(BF16))
        qe_s[...] = q * jnp.exp2(b)
        kd_s[...] = k * jnp.exp2(b_last - b)
        v_s[...] = v
        dec = jnp.exp2(b_last)
        dsw = jnp.where((row & 7) < 4, pltpu.roll(dec, n - 4, axis=0), pltpu.roll(dec, 4, axis=0))
        hi = dsw.astype(BF16).astype(F32)
        mid = (dsw - hi).astype(BF16).astype(F32)
        lo = (dsw - hi - mid).astype(BF16).astype(F32)
        decp_s[...] = jnp.where(pos == 0, hi, jnp.where(pos == 1, mid, jnp.where(pos == 2, lo, 0.0)))

    row = lax.broadcasted_iota(jnp.int32, (8, HEAD_DIM), 0)
    pad = jnp.zeros((HEAD_DIM - 8, HEAD_DIM), F32)
    pad2 = jnp.zeros((HEAD_DIM - 8, 2 * HEAD_DIM), F32)
    base = blk * (nseq * seq)

    def pairs_body(it, carry):
        for u in range(SAMPLE_PAIRS_PER_ITER):
            p = it * SAMPLE_PAIRS_PER_ITER + u
            rows = pl.ds(pl.multiple_of(base + p * 8, 8), 8)
            qe8 = qe_s[rows, :].astype(BF16)
            kd8 = kd_s[rows, :]
            dp8 = decp_s[rows, :]
            v8 = v_s[rows, :]
            o8 = o_s[hd, rows, :]
            for half in range(2):
                mine = (row >= 4) if half else (row < 4)
                s0 = s0_ref[2 * p + half]
                o8 = o8 + jnp.where(mine, _dot(qe8, s0.astype(BF16)), 0.0)
                lhs = jnp.concatenate([jnp.where(mine, kd8, dp8), pad], axis=0).astype(BF16)
                ones = jnp.where(mine, 0.0, 1.0)
                rhs = jnp.concatenate(
                    [jnp.concatenate([jnp.where(mine, v8, 0.0), ones], axis=1), pad2], axis=0).astype(BF16)
                upd = _dot_tn(lhs, rhs)
                s1_ref[2 * p + half] = upd[:, HEAD_DIM:] * s0 + upd[:, :HEAD_DIM]
            o_s[hd, rows, :] = o8
        return carry

    lax.fori_loop(0, nseq // (2 * SAMPLE_PAIRS_PER_ITER), pairs_body, 0)

    @pl.when(jnp.logical_and(hd == heads - 1, blk == pl.num_programs(1) - 1))
    def _():
        gon = gon_ref[...]
        gated = jnp.concatenate(
            [(_rms(o_s[i], gon) * gate_s[i]).astype(BF16) for i in range(heads)], axis=1)
        y = _dot(gated, wout_ref[...].astype(BF16))
        y_ref[...] = x_ref[...] + _rms(y, gpost_ref[...])


def _const_spec(shape):
    nd = len(shape)
    return pl.BlockSpec(shape, lambda *_: (0,) * nd)


def _layer_spec(arr, layer):
    nd = arr.ndim - 1
    return pl.BlockSpec((None,) + arr.shape[1:], lambda *_: (layer,) + (0,) * nd)


def _params(sem):
    return pltpu.CompilerParams(dimension_semantics=sem, vmem_limit_bytes=VMEM_LIMIT_BYTES)


def _conv_prompt(i, j, x, gpre, win, cw, wout, gpost):
    bsz, t, d = x.shape
    e = wout.shape[1]
    tm = CONV_PROMPT_TILE
    assert t % tm == 0 and tm % 8 == 0
    return pl.pallas_call(
        _conv_prompt_kernel,
        grid=(bsz, t // tm),
        in_specs=[pl.BlockSpec((1, tm, d), lambda b, s: (b, s, 0)),
                  _layer_spec(gpre, i), _layer_spec(win, i), _layer_spec(cw, j),
                  _layer_spec(wout, i), _layer_spec(gpost, i)],
        out_specs=[pl.BlockSpec((1, tm, d), lambda b, s: (b, s, 0)),
                   pl.BlockSpec((1, CONV_W - 1, e), lambda b, s: (b, 0, 0))],
        out_shape=[jax.ShapeDtypeStruct(x.shape, F32),
                   jax.ShapeDtypeStruct((bsz, CONV_W - 1, e), F32)],
        scratch_shapes=[pltpu.VMEM((8, e), F32),
                        pltpu.VMEM(win.shape[1:], BF16), pltpu.VMEM(wout.shape[1:], BF16)],
        compiler_params=_params(("arbitrary", "arbitrary")),
        name="conv_prompt",
    )(x, gpre, win, cw, wout, gpost)


def _conv_sample(i, j, xf, seq, buf, gpre, win, cw, wout, gpost):
    n, d = xf.shape
    nb = n // seq
    e = wout.shape[1]
    assert seq & (seq - 1) == 0 and seq >= CONV_W - 1
    zeros = lambda r: jnp.zeros((nb, r, e), F32)
    p1 = jnp.concatenate([buf[:, 1:2], zeros(seq - 1)], axis=1).reshape(n, e)
    p2 = jnp.concatenate([buf[:, 0:2], zeros(seq - 2)], axis=1).reshape(n, e)
    y, u = pl.pallas_call(
        functools.partial(_conv_sample_kernel, seq),
        grid=(1,),
        in_specs=[_const_spec(xf.shape), _const_spec(p1.shape), _const_spec(p2.shape),
                  _layer_spec(gpre, i), _layer_spec(win, i), _layer_spec(cw, j),
                  _layer_spec(wout, i), _layer_spec(gpost, i)],
        out_specs=[_const_spec((n, d)), _const_spec((n, e))],
        out_shape=[jax.ShapeDtypeStruct((n, d), F32), jax.ShapeDtypeStruct((n, e), F32)],
        scratch_shapes=[pltpu.VMEM(win.shape[1:], BF16), pltpu.VMEM(wout.shape[1:], BF16)],
        compiler_params=_params(("arbitrary",)),
        name="conv_sample",
    )(xf, p1, p2, gpre, win, cw, wout, gpost)
    return y, u.reshape(nb, seq, e)[:, seq - (CONV_W - 1):]


def _hgrn_prompt(i, j, x, gpre, win, lb_logits, gon, wout, gpost):
    bsz, t, d = x.shape
    e = wout.shape[1]
    heads = e // HEAD_DIM
    tm = PROMPT_TILE
    assert t % tm == 0 and tm % HGRN_CHUNK == 0
    nt = t // tm
    n_tiles = bsz * nt
    n_bf16 = 5 + len(_level_sizes(HGRN_CHUNK))
    scratch = ([pltpu.VMEM(win.shape[1:], BF16), pltpu.VMEM(wout.shape[1:], BF16),
                pltpu.VMEM((heads, HEAD_DIM, HEAD_DIM), F32),
                pltpu.VMEM((tm // HGRN_CHUNK, 8, e), F32),
                pltpu.VMEM((tm, 4 * e), F32),
                pltpu.VMEM((tm, e), F32)]
               + [pltpu.VMEM((tm, e), BF16) for _ in range(n_bf16)])

    def projected(g):
        tile = jnp.minimum(g, n_tiles - 1)
        return tile // nt, tile % nt

    def finished(g):
        tile = jnp.maximum(g - 2, 0)
        return tile // nt, tile % nt

    return pl.pallas_call(
        functools.partial(_hgrn_prompt_kernel, j, nt),
        grid=(n_tiles + 2,),
        in_specs=[pl.BlockSpec((1, tm, d), lambda g: projected(g) + (0,)),
                  pl.BlockSpec((1, tm, d), lambda g: finished(g) + (0,)),
                  _layer_spec(gpre, i), _layer_spec(win, i), _const_spec(lb_logits.shape),
                  _layer_spec(gon, j), _layer_spec(wout, i), _layer_spec(gpost, i)],
        out_specs=[pl.BlockSpec((1, tm, d), lambda g: finished(g) + (0,)),
                   pl.BlockSpec((1, heads, HEAD_DIM, HEAD_DIM), lambda g: (finished(g)[0], 0, 0, 0))],
        out_shape=[jax.ShapeDtypeStruct(x.shape, F32),
                   jax.ShapeDtypeStruct((bsz, heads, HEAD_DIM, HEAD_DIM), F32)],
        scratch_shapes=scratch,
        compiler_params=_params(("arbitrary",)),
        name="hgrn_prompt",
    )(x, x, gpre, win, lb_logits, gon, wout, gpost)


def _hgrn_sample(i, j, xf, seq, state, states_out, gpre, win, lb_logits, gon, wout, gpost):
    n, d = xf.shape
    nb = n // seq
    e = wout.shape[1]
    heads = e // HEAD_DIM
    g = SAMPLE_SEQS_PER_STEP
    assert seq == 4 and nb % g == 0 and g % 2 == 0

    def head_cols(part):
        return pl.BlockSpec((None, d, HEAD_DIM), lambda hd, blk: (i, 0, part * heads + hd))

    st_spec = pl.BlockSpec((None, g, None, HEAD_DIM, HEAD_DIM), lambda hd, blk: (j, blk, hd, 0, 0))
    args = [xf, gpre, win, win, win, win, lb_logits, gon, wout, gpost, state]
    in_specs = [_const_spec(xf.shape), _layer_spec(gpre, i)] + [head_cols(part) for part in range(4)] + [
        pl.BlockSpec((lb_logits.shape[0], HEAD_DIM), lambda hd, blk: (0, hd)),
        _layer_spec(gon, j), _layer_spec(wout, i), _layer_spec(gpost, i), st_spec]
    aliases = {}
    if states_out is not None:
        args.append(states_out)
        in_specs.append(pl.BlockSpec(memory_space=pl.ANY))
        aliases = {len(args) - 1: 1}
    y, states_out = pl.pallas_call(
        functools.partial(_hgrn_sample_kernel, j, seq),
        grid=(heads, nb // g),
        in_specs=in_specs,
        out_specs=[_const_spec((n, d)), st_spec],
        out_shape=[jax.ShapeDtypeStruct((n, d), F32), jax.ShapeDtypeStruct(state.shape, F32)],
        scratch_shapes=[pltpu.VMEM((n, d), BF16)]
                       + [pltpu.VMEM((n, HEAD_DIM), F32) for _ in range(4)]
                       + [pltpu.VMEM((heads, n, HEAD_DIM), F32) for _ in range(2)],
        input_output_aliases=aliases,
        compiler_params=_params(("arbitrary", "arbitrary")),
        name="hgrn_sample",
    )(*args)
    return y, states_out


def kernel(x_prompt, x_sample, state_conv, state_hgrn, norm_pre, w_in, conv_w, hgrn_lb_logits,
           hgrn_onorm, w_out, norm_post):
    depth = w_in.shape[0]
    w_in32 = w_in.astype(F32)
    w_out32 = w_out.astype(F32)
    gpre = norm_pre.astype(F32)[:, None, :]
    gpost = norm_post.astype(F32)[:, None, :]
    gon = hgrn_onorm.astype(F32)[:, None, :]
    lb_logits = hgrn_lb_logits.astype(F32)
    state_hgrn = state_hgrn.astype(F32)
    nb, seq, d = x_sample.shape
    xp, xs = x_prompt, x_sample.reshape(nb * seq, d)
    conv_p, conv_s, hgrn_p, hgrn_s = [], [], [], None
    for i in range(depth):
        j = i // 2
        if i % 2 == 0:
            xp, bp = _conv_prompt(i, j, xp, gpre, w_in32, conv_w, w_out32, gpost)
            xs, bs = _conv_sample(i, j, xs, seq, state_conv[j], gpre, w_in32, conv_w, w_out32, gpost)
            conv_p.append(bp)
            conv_s.append(bs)
        else:
            xp, sp = _hgrn_prompt(i, j, xp, gpre, w_in32, lb_logits, gon, w_out32, gpost)
            xs, hgrn_s = _hgrn_sample(i, j, xs, seq, state_hgrn, hgrn_s, gpre, w_in32, lb_logits, gon,
                                      w_out32, gpost)
            hgrn_p.append(sp)
    return (xp, xs.reshape(nb, seq, d), jnp.stack(conv_p), jnp.stack(conv_s), jnp.stack(hgrn_p), hgrn_s)
```

```python
import functools
import math

import jax
import jax.numpy as jnp
from jax import lax
from jax.experimental import pallas as pl
from jax.experimental.pallas import tpu as pltpu

F32 = jnp.float32
BF16 = jnp.bfloat16

RMS_EPS = 1e-6
LOG_F_FLOOR = -20.0
LOG2_E = math.log2(math.e)
HEAD_DIM = 128
LANES = 128
MXU_WIDTH = 256
CONV_W = 3
HGRN_CHUNK = 128
BASE_BLOCK = 8
SCORE_LOOKAHEAD = 2
PROMPT_TILE = 256
CONV_PROMPT_TILE = 512
SAMPLE_SEQS_PER_STEP = 128
SAMPLE_PAIRS_PER_ITER = 8
VMEM_LIMIT_BYTES = 56 * 1024 * 1024


def _rms(x, g):
    ms = jnp.mean(x * x, axis=-1, keepdims=True)
    return x * lax.rsqrt(ms + RMS_EPS) * g


def _silu(x):
    return x * (1.0 / (1.0 + jnp.exp2(x * (-LOG2_E))))


def _dot(a, b):
    return jnp.dot(a, b, preferred_element_type=F32)


def _dot_nt(a, b):
    return lax.dot_general(a, b, (((1,), (1,)), ((), ())), preferred_element_type=F32)


def _dot_tn(a, b):
    return lax.dot_general(a, b, (((0,), (0,)), ((), ())), preferred_element_type=F32)


def _lower_bound(logits, j):
    m = jnp.max(logits, axis=0, keepdims=True)
    e = jnp.exp(logits - m)
    p = e / jnp.sum(e, axis=0, keepdims=True)
    if j == 0:
        return jnp.zeros_like(p[0:1])
    return jnp.sum(p[1:j + 1], axis=0, keepdims=True)


def _forget_gate(fpre, lb):
    e = jnp.exp2(jnp.abs(fpre) * (-LOG2_E))
    r = 1.0 / (1.0 + e)
    er = e * r
    pos = fpre >= 0
    sig = jnp.where(pos, r, er)
    nsig = jnp.where(pos, er, r)
    f = lb + (1.0 - lb) * sig
    log2_f = jnp.maximum(jnp.log(f) * LOG2_E, LOG_F_FLOOR * LOG2_E)
    k = jnp.minimum((1.0 - lb) * nsig, 1.0 - math.exp(LOG_F_FLOOR))
    return log2_f, k


def _cumsum_rows(x, group):
    pos = lax.broadcasted_iota(jnp.int32, x.shape, 0) & (group - 1)
    s = 1
    while s < group:
        x = x + jnp.where(pos >= s, pltpu.roll(x, s, axis=0), 0.0)
        s *= 2
    return x


def _group_allsum(x, group):
    n = x.shape[0]
    pos = lax.broadcasted_iota(jnp.int32, x.shape, 0) & (group - 1)
    s = 1
    while s < group:
        partner = jnp.where((pos & s) == 0, pltpu.roll(x, n - s, axis=0), pltpu.roll(x, s, axis=0))
        x = x + partner
        s *= 2
    return x


def _level_matrix(n):
    t = lax.broadcasted_iota(jnp.int32, (n, n), 0)
    s = lax.broadcasted_iota(jnp.int32, (n, n), 1)
    sh = BASE_BLOCK.bit_length() - 1
    lv = jnp.where(((t >> sh) == (s >> sh)) & (t >= s), 1, 0)
    c, i = 2 * BASE_BLOCK, 2
    while c <= n:
        half, sh = c // 2, sh + 1
        own = ((t >> sh) == (s >> sh)) & ((t & half) != 0) & ((s & half) == 0)
        lv = jnp.where(own, i, lv)
        c, i = 2 * c, i + 1
    return lv


def _level_sizes(chunk):
    sizes, c = [], 2 * BASE_BLOCK
    while c <= chunk:
        sizes.append(c)
        c *= 2
    return sizes


def _chunk_operands(q, k, lf2):
    c = q.shape[0]
    local = _cumsum_rows(lf2, BASE_BLOCK)
    b_parts, x_parts, carry = [], [], None
    for g in range(c // BASE_BLOCK):
        blk = local[g * BASE_BLOCK:(g + 1) * BASE_BLOCK]
        x_parts.append(blk - blk[BASE_BLOCK // 2 - 1:BASE_BLOCK // 2])
        if carry is not None:
            blk = blk + carry
        b_parts.append(blk)
        carry = blk[BASE_BLOCK - 1:BASE_BLOCK]
    b = jnp.concatenate(b_parts, axis=0)
    x = jnp.concatenate(x_parts, axis=0)
    b_last = carry
    qe = (q * jnp.exp2(b)).astype(BF16)
    kd = (k * jnp.exp2(b_last - b)).astype(BF16)
    q8 = (q * jnp.exp2(x)).astype(BF16)
    k8 = (k * jnp.exp2(-x)).astype(BF16)
    w = []
    for size in _level_sizes(c):
        half = size // 2
        parts = []
        for s0 in range(0, c, size):
            r = b[s0 + half - 1:s0 + half]
            lo, hi = slice(s0, s0 + half), slice(s0 + half, s0 + size)
            parts.append(k[lo] * jnp.exp2(r - b[lo]))
            parts.append(q[hi] * jnp.exp2(b[hi] - r))
        w.append(jnp.concatenate(parts, axis=0).astype(BF16))
    return qe, kd, q8, k8, w, jnp.exp2(b_last)


def _conv_mix(h, win_ref, cw_ref, prev1, prev2, pos):
    e = win_ref.shape[1] // 4
    v = _dot(h, win_ref[:, 0:e])
    cg = _dot(h, win_ref[:, 2 * e:3 * e])
    u = cg * v
    u1 = jnp.where(pos >= 1, pltpu.roll(u, 1, axis=0), prev1)
    u2 = jnp.where(pos >= 2, pltpu.roll(u, 2, axis=0), prev2)
    conv = cw_ref[0:1, :] * u2 + cw_ref[1:2, :] * u1 + cw_ref[2:3, :] * u
    bg = _dot(h, win_ref[:, e:2 * e])
    z = _dot(h, win_ref[:, 3 * e:4 * e])
    return bg * conv * _silu(z), u


def _conv_prompt_kernel(x_ref, gpre_ref, win_in_ref, cw_ref, wout_in_ref, gpost_ref,
                        y_ref, buf_ref, tail_ref, win_ref, wout_ref):
    @pl.when(jnp.logical_and(pl.program_id(0) == 0, pl.program_id(1) == 0))
    def _():
        win_ref[...] = win_in_ref[...].astype(BF16)
        wout_ref[...] = wout_in_ref[...].astype(BF16)

    @pl.when(pl.program_id(1) == 0)
    def _():
        tail_ref[...] = jnp.zeros_like(tail_ref)

    x = x_ref[0]
    tm = x.shape[0]
    h = _rms(x, gpre_ref[...]).astype(BF16)
    pos = lax.broadcasted_iota(jnp.int32, (tm, win_ref.shape[1] // 4), 0)
    tail = tail_ref[...]
    prev1 = tail[7:8, :]
    prev2 = jnp.where(pos == 0, tail[6:7, :], tail[7:8, :])
    g, u = _conv_mix(h, win_ref, cw_ref, prev1, prev2, pos)
    y = _dot(g.astype(BF16), wout_ref[...])
    y_ref[0] = x + _rms(y, gpost_ref[...])
    tail_ref[...] = u[tm - 8:tm, :]
    buf_ref[0] = u[tm - (CONV_W - 1):tm, :]


def _conv_sample_kernel(seq, x_ref, buf_ref, gpre_ref, win_in_ref, cw_ref, wout_in_ref, gpost_ref,
                        y_ref, new_buf_ref, win_ref, wout_ref, p1_s, p2_s):
    win_ref[...] = win_in_ref[...].astype(BF16)
    wout_ref[...] = wout_in_ref[...].astype(BF16)
    x = x_ref[...]
    nb = buf_ref.shape[0]
    lane_tiles, n, lanes = p1_s.shape
    e = lane_tiles * lanes
    h = _rms(x, gpre_ref[...]).astype(BF16)
    pos = lax.broadcasted_iota(jnp.int32, (n, e), 0) & (seq - 1)
    p1_s[...] = jnp.zeros_like(p1_s)
    p2_s[...] = jnp.zeros_like(p2_s)
    for c in range(lane_tiles):
        buf0 = buf_ref[:, c * lanes:(c + 1) * lanes]
        buf1 = buf_ref[:, e + c * lanes:e + (c + 1) * lanes]
        p1_s[c, pl.ds(0, nb, stride=seq), :] = buf1
        p2_s[c, pl.ds(0, nb, stride=seq), :] = buf0
        p2_s[c, pl.ds(1, nb, stride=seq), :] = buf1
    prev1 = jnp.concatenate([p1_s[c] for c in range(lane_tiles)], axis=1)
    prev2 = jnp.concatenate([p2_s[c] for c in range(lane_tiles)], axis=1)
    g, u = _conv_mix(h, win_ref, cw_ref, prev1, prev2, pos)
    y = _dot(g.astype(BF16), wout_ref[...])
    y_ref[...] = x + _rms(y, gpost_ref[...])
    for c in range(lane_tiles):
        p1_s[c] = u[:, c * lanes:(c + 1) * lanes]
        for r in range(CONV_W - 1):
            new_buf_ref[:, r * e + c * lanes:r * e + (c + 1) * lanes] = (
                p1_s[c, pl.ds(seq - (CONV_W - 1) + r, nb, stride=seq), :])


def _hgrn_prompt_kernel(j, nt, xp_ref, xr_ref, gpre_ref, win_in_ref, lbl_ref, gon_ref, wout_in_ref, gpost_ref,
                        y_ref, sout_ref, win_ref, wout_ref, st_s, dec_s, raw_s, gate_s,
                        qe_s, kd_s, q8_s, k8_s, v_s, *w_s):
    g = pl.program_id(0)
    e = win_ref.shape[1] // 4
    heads = e // HEAD_DIM
    tm = xp_ref.shape[1]
    fin = g - 2

    @pl.when(g == 0)
    def _():
        win_ref[...] = win_in_ref[...].astype(BF16)
        wout_ref[...] = wout_in_ref[...].astype(BF16)
        for ref in (dec_s, raw_s, gate_s, qe_s, kd_s, q8_s, k8_s, v_s) + tuple(w_s):
            ref[...] = jnp.zeros_like(ref)

    @pl.when(jnp.logical_or(g < 2, lax.rem(fin, nt) == 0))
    def _():
        st_s[...] = jnp.zeros_like(st_s)

    lv = _level_matrix(HGRN_CHUNK)
    owned = [lv == i + 1 for i in range(len(w_s) + 1)]
    gon = gon_ref[...]
    chunks = tm // HGRN_CHUNK

    def chunk_scores(c, hd):
        rows = slice(c * HGRN_CHUNK, (c + 1) * HGRN_CHUNK)
        cols = slice(hd * HEAD_DIM, (hd + 1) * HEAD_DIM)
        a = jnp.where(owned[0], _dot_nt(q8_s[rows, cols], k8_s[rows, cols]), 0.0)
        for i, w_ref in enumerate(w_s):
            wv = w_ref[rows, cols]
            a = jnp.where(owned[i + 1], _dot_nt(wv, wv), a)
        return a.astype(BF16)

    def chunk_output(c, hd, a):
        rows = slice(c * HGRN_CHUNK, (c + 1) * HGRN_CHUNK)
        cols = slice(hd * HEAD_DIM, (hd + 1) * HEAD_DIM)
        vb = v_s[rows, cols]
        st = st_s[hd]
        o = _dot(a, vb) + _dot_nt(qe_s[rows, cols], st.astype(BF16))
        st3 = st.reshape(HEAD_DIM // 8, 8, HEAD_DIM) * dec_s[c, :, cols]
        st_s[hd] = st3.reshape(HEAD_DIM, HEAD_DIM) + _dot_tn(vb, kd_s[rows, cols])
        return (_rms(o, gon) * gate_s[rows, cols]).astype(BF16)

    h = _rms(xp_ref[0], gpre_ref[...]).astype(BF16)
    lb = _lower_bound(lbl_ref[...], j)
    out_blocks = [[None] * heads for _ in range(chunks)]
    groups = e // MXU_WIDTH

    def part_cols(cg):
        return [slice(part * e + cg * MXU_WIDTH, part * e + (cg + 1) * MXU_WIDTH) for part in range(4)]


    def stage(cg, c, q_raw, f_raw, v_raw, z_raw):
        gc = slice(cg * MXU_WIDTH, (cg + 1) * MXU_WIDTH)
        rows = slice(c * HGRN_CHUNK, (c + 1) * HGRN_CHUNK)
        q = _silu(q_raw) * (HEAD_DIM ** -0.5)
        lf2, k = _forget_gate(f_raw, lb[:, gc])
        v_s[rows, gc] = v_raw.astype(BF16)
        gate_s[rows, gc] = _silu(z_raw)
        qe, kd, q8, k8, w, decay = _chunk_operands(q, k, lf2)
        qe_s[rows, gc] = qe
        kd_s[rows, gc] = kd
        q8_s[rows, gc] = q8
        k8_s[rows, gc] = k8
        for w_ref, w_val in zip(w_s, w):
            w_ref[rows, gc] = w_val
        dec_s[c, :, gc] = jnp.broadcast_to(decay, (8, MXU_WIDTH))

    def project(cols):
        raw_s[:, cols] = _dot(h, win_ref[:, cols])

    def output():
        gated = jnp.concatenate([jnp.concatenate(blocks, axis=1) for blocks in out_blocks], axis=0)
        y = _dot(gated, wout_ref[...])
        y_ref[0] = xr_ref[0] + _rms(y, gpost_ref[...])

    heads_per_group = MXU_WIDTH // HEAD_DIM
    raws = {}
    for cg in range(groups):
        for c in range(chunks):
            rows = slice(c * HGRN_CHUNK, (c + 1) * HGRN_CHUNK)
            raws[cg, c] = [raw_s[rows, cols] for cols in part_cols(cg)]
    projections = [cols for cg in range(groups) for cols in part_cols(cg)]
    units = [(cg, c, hd) for cg in range(groups) for c in range(chunks)
             for hd in range(cg * heads_per_group, (cg + 1) * heads_per_group)]
    scores = {}
    for n in range(len(units) + SCORE_LOOKAHEAD):
        if n < len(units):
            _, c, hd = units[n]
            scores[c, hd] = chunk_scores(c, hd)
        if projections:
            project(projections.pop(0))
        if n >= SCORE_LOOKAHEAD:
            done = n - SCORE_LOOKAHEAD
            cg, c, hd = units[done]
            out_blocks[c][hd] = chunk_output(c, hd, scores.pop((c, hd)))
            if (done + 1) % heads_per_group == 0:
                stage(cg, c, *raws.pop((cg, c)))
    for cols in projections:
        project(cols)
    output()

    @pl.when(jnp.logical_and(g >= 2, lax.rem(fin, nt) == nt - 1))
    def _():
        for hd in range(heads):
            sout_ref[0, hd] = st_s[hd].T


def _hgrn_sample_kernel(j, seq, x_ref, gpre_ref, wq_ref, wf_ref, wv_ref, wz_ref, lbl_ref, gon_ref,
                        wout_ref, gpost_ref, s0_ref, *rest):
    y_ref, s1_ref, h_s, qe_s, kd_s, decp_s, v_s, o_s, gate_s = rest[-9:]
    hd, blk = pl.program_id(0), pl.program_id(1)
    n = x_ref.shape[0]
    nseq = s0_ref.shape[0]
    heads = o_s.shape[0]

    @pl.when(jnp.logical_and(hd == 0, blk == 0))
    def _():
        h_s[...] = _rms(x_ref[...], gpre_ref[...]).astype(BF16)

    @pl.when(blk == 0)
    def _():
        h = h_s[...]
        lb = _lower_bound(lbl_ref[...], j)
        q = _silu(_dot(h, wq_ref[...].astype(BF16))) * (HEAD_DIM ** -0.5)
        lf2, k = _forget_gate(_dot(h, wf_ref[...].astype(BF16)), lb)
        v = _dot(h, wv_ref[...].astype(BF16))
        gate_s[hd] = _silu(_dot(h, wz_ref[...].astype(BF16)))
        t = lax.broadcasted_iota(jnp.int32, (n, n), 0)
        s = lax.broadcasted_iota(jnp.int32, (n, n), 1)
        sh = seq.bit_length() - 1
        same_seq_causal = ((t >> sh) == (s >> sh)) & (t >= s)
        row = lax.broadcasted_iota(jnp.int32, (n, HEAD_DIM), 0)
        pos = row & (seq - 1)
        b = _cumsum_rows(lf2, seq)
        b_last = _group_allsum(lf2, seq)
        x_mid = b - _group_allsum(jnp.where(pos < seq // 2, lf2, 0.0), seq)
        a = _dot_nt((q * jnp.exp2(x_mid)).astype(BF16), (k * jnp.exp2(-x_mid)).astype(BF16))
        a = jnp.where(same_seq_causal, a, 0.0)
        o_s[hd] = _dot(a.astype(BF16), v.astype(BF16))
        qe_s[...] = q * jnp.exp2(b)
        kd_s[...] = k * jnp.exp2(b_last - b)
        v_s[...] = v
        dec = jnp.exp2(b_last)
        dsw = jnp.where((row & 7) < 4, pltpu.roll(dec, n - 4, axis=0), pltpu.roll(dec, 4, axis=0))
        hi = dsw.astype(BF16).astype(F32)
        mid = (dsw - hi).astype(BF16).astype(F32)
        lo = (dsw - hi - mid).astype(BF16).astype(F32)
        decp_s[...] = jnp.where(pos == 0, hi, jnp.where(pos == 1, mid, jnp.where(pos == 2, lo, 0.0)))

    row = lax.broadcasted_iota(jnp.int32, (8, HEAD_DIM), 0)
    pad = jnp.zeros((HEAD_DIM - 8, HEAD_DIM), F32)
    pad2 = jnp.zeros((HEAD_DIM - 8, 2 * HEAD_DIM), F32)
    base = blk * (nseq * seq)

    def pairs_body(it, carry):
        for u in range(SAMPLE_PAIRS_PER_ITER):
            p = it * SAMPLE_PAIRS_PER_ITER + u
            rows = pl.ds(pl.multiple_of(base + p * 8, 8), 8)
            qe8 = qe_s[rows, :].astype(BF16)
            kd8 = kd_s[rows, :]
            dp8 = decp_s[rows, :]
            v8 = v_s[rows, :]
            o8 = o_s[hd, rows, :]
            for half in range(2):
                mine = (row >= 4) if half else (row < 4)
                s0 = s0_ref[2 * p + half]
                o8 = o8 + jnp.where(mine, _dot(qe8, s0.astype(BF16)), 0.0)
                lhs = jnp.concatenate([jnp.where(mine, kd8, dp8), pad], axis=0).astype(BF16)
                ones = jnp.where(mine, 0.0, 1.0)
                rhs = jnp.concatenate(
                    [jnp.concatenate([jnp.where(mine, v8, 0.0), ones], axis=1), pad2], axis=0).astype(BF16)
                upd = _dot_tn(lhs, rhs)
                s1_ref[2 * p + half] = upd[:, HEAD_DIM:] * s0 + upd[:, :HEAD_DIM]
            o_s[hd, rows, :] = o8
        return carry

    lax.fori_loop(0, nseq // (2 * SAMPLE_PAIRS_PER_ITER), pairs_body, 0)

    @pl.when(jnp.logical_and(hd == heads - 1, blk == pl.num_programs(1) - 1))
    def _():
        gon = gon_ref[...]
        gated = jnp.concatenate(
            [(_rms(o_s[i], gon) * gate_s[i]).astype(BF16) for i in range(heads)], axis=1)
        y = _dot(gated, wout_ref[...].astype(BF16))
        y_ref[...] = x_ref[...] + _rms(y, gpost_ref[...])


def _const_spec(shape):
    nd = len(shape)
    return pl.BlockSpec(shape, lambda *_: (0,) * nd)


def _layer_spec(arr, layer):
    nd = arr.ndim - 1
    return pl.BlockSpec((None,) + arr.shape[1:], lambda *_: (layer,) + (0,) * nd)


def _params(sem):
    return pltpu.CompilerParams(dimension_semantics=sem, vmem_limit_bytes=VMEM_LIMIT_BYTES)


def _conv_prompt(i, j, x, gpre, win, cw, wout, gpost):
    bsz, t, d = x.shape
    e = wout.shape[1]
    tm = CONV_PROMPT_TILE
    assert t % tm == 0 and tm % 8 == 0
    return pl.pallas_call(
        _conv_prompt_kernel,
        grid=(bsz, t // tm),
        in_specs=[pl.BlockSpec((1, tm, d), lambda b, s: (b, s, 0)),
                  _layer_spec(gpre, i), _layer_spec(win, i), _layer_spec(cw, j),
                  _layer_spec(wout, i), _layer_spec(gpost, i)],
        out_specs=[pl.BlockSpec((1, tm, d), lambda b, s: (b, s, 0)),
                   pl.BlockSpec((1, CONV_W - 1, e), lambda b, s: (b, 0, 0))],
        out_shape=[jax.ShapeDtypeStruct(x.shape, F32),
                   jax.ShapeDtypeStruct((bsz, CONV_W - 1, e), F32)],
        scratch_shapes=[pltpu.VMEM((8, e), F32),
                        pltpu.VMEM(win.shape[1:], BF16), pltpu.VMEM(wout.shape[1:], BF16)],
        compiler_params=_params(("arbitrary", "arbitrary")),
        name="conv_prompt",
    )(x, gpre, win, cw, wout, gpost)


def _conv_sample(i, j, xf, seq, state, gpre, win, cw, wout, gpost):
    n, d = xf.shape
    nb = n // seq
    e = wout.shape[1]
    assert CONV_W == 3 and seq & (seq - 1) == 0 and seq >= CONV_W - 1
    y, new_buf = pl.pallas_call(
        functools.partial(_conv_sample_kernel, seq),
        grid=(1,),
        in_specs=[_const_spec(xf.shape), _layer_spec(state, j),
                  _layer_spec(gpre, i), _layer_spec(win, i), _layer_spec(cw, j),
                  _layer_spec(wout, i), _layer_spec(gpost, i)],
        out_specs=[_const_spec((n, d)), _const_spec((nb, (CONV_W - 1) * e))],
        out_shape=[jax.ShapeDtypeStruct((n, d), F32),
                   jax.ShapeDtypeStruct((nb, (CONV_W - 1) * e), F32)],
        scratch_shapes=[pltpu.VMEM(win.shape[1:], BF16), pltpu.VMEM(wout.shape[1:], BF16),
                        pltpu.VMEM((e // LANES, n, LANES), F32),
                        pltpu.VMEM((e // LANES, n, LANES), F32)],
        compiler_params=_params(("arbitrary",)),
        name="conv_sample",
    )(xf, state, gpre, win, cw, wout, gpost)
    return y, new_buf.reshape(nb, CONV_W - 1, e)


def _hgrn_prompt(i, j, x, gpre, win, lb_logits, gon, wout, gpost):
    bsz, t, d = x.shape
    e = wout.shape[1]
    heads = e // HEAD_DIM
    tm = PROMPT_TILE
    assert t % tm == 0 and tm % HGRN_CHUNK == 0
    nt = t // tm
    n_tiles = bsz * nt
    n_bf16 = 5 + len(_level_sizes(HGRN_CHUNK))
    scratch = ([pltpu.VMEM(win.shape[1:], BF16), pltpu.VMEM(wout.shape[1:], BF16),
                pltpu.VMEM((heads, HEAD_DIM, HEAD_DIM), F32),
                pltpu.VMEM((tm // HGRN_CHUNK, 8, e), F32),
                pltpu.VMEM((tm, 4 * e), F32),
                pltpu.VMEM((tm, e), F32)]
               + [pltpu.VMEM((tm, e), BF16) for _ in range(n_bf16)])

    def projected(g):
        tile = jnp.minimum(g, n_tiles - 1)
        return tile // nt, tile % nt

    def finished(g):
        tile = jnp.maximum(g - 2, 0)
        return tile // nt, tile % nt

    return pl.pallas_call(
        functools.partial(_hgrn_prompt_kernel, j, nt),
        grid=(n_tiles + 2,),
        in_specs=[pl.BlockSpec((1, tm, d), lambda g: projected(g) + (0,)),
                  pl.BlockSpec((1, tm, d), lambda g: finished(g) + (0,)),
                  _layer_spec(gpre, i), _layer_spec(win, i), _const_spec(lb_logits.shape),
                  _layer_spec(gon, j), _layer_spec(wout, i), _layer_spec(gpost, i)],
        out_specs=[pl.BlockSpec((1, tm, d), lambda g: finished(g) + (0,)),
                   pl.BlockSpec((1, heads, HEAD_DIM, HEAD_DIM), lambda g: (finished(g)[0], 0, 0, 0))],
        out_shape=[jax.ShapeDtypeStruct(x.shape, F32),
                   jax.ShapeDtypeStruct((bsz, heads, HEAD_DIM, HEAD_DIM), F32)],
        scratch_shapes=scratch,
        compiler_params=_params(("arbitrary",)),
        name="hgrn_prompt",
    )(x, x, gpre, win, lb_logits, gon, wout, gpost)


def _hgrn_sample(i, j, xf, seq, state, states_out, gpre, win, lb_logits, gon, wout, gpost):
    n, d = xf.shape
    nb = n // seq
    e = wout.shape[1]
    heads = e // HEAD_DIM
    g = SAMPLE_SEQS_PER_STEP
    assert seq == 4 and nb % g == 0 and g % 2 == 0

    def head_cols(part):
        return pl.BlockSpec((None, d, HEAD_DIM), lambda hd, blk: (i, 0, part * heads + hd))

    st_spec = pl.BlockSpec((None, g, None, HEAD_DIM, HEAD_DIM), lambda hd, blk: (j, blk, hd, 0, 0))
    args = [xf, gpre, win, win, win, win, lb_logits, gon, wout, gpost, state]
    in_specs = [_const_spec(xf.shape), _layer_spec(gpre, i)] + [head_cols(part) for part in range(4)] + [
        pl.BlockSpec((lb_logits.shape[0], HEAD_DIM), lambda hd, blk: (0, hd)),
        _layer_spec(gon, j), _layer_spec(wout, i), _layer_spec(gpost, i), st_spec]
    aliases = {}
    if states_out is not None:
        args.append(states_out)
        in_specs.append(pl.BlockSpec(memory_space=pl.ANY))
        aliases = {len(args) - 1: 1}
    y, states_out = pl.pallas_call(
        functools.partial(_hgrn_sample_kernel, j, seq),
        grid=(heads, nb // g),
        in_specs=in_specs,
        out_specs=[_const_spec((n, d)), st_spec],
        out_shape=[jax.ShapeDtypeStruct((n, d), F32), jax.ShapeDtypeStruct(state.shape, F32)],
        scratch_shapes=[pltpu.VMEM((n, d), BF16)]
                       + [pltpu.VMEM((n, HEAD_DIM), F32) for _ in range(4)]
                       + [pltpu.VMEM((heads, n, HEAD_DIM), F32) for _ in range(2)],
        input_output_aliases=aliases,
        compiler_params=_params(("arbitrary", "arbitrary")),
        name="hgrn_sample",
    )(*args)
    return y, states_out


def kernel(x_prompt, x_sample, state_conv, state_hgrn, norm_pre, w_in, conv_w, hgrn_lb_logits,
           hgrn_onorm, w_out, norm_post):
    depth = w_in.shape[0]
    w_in32 = w_in.astype(F32)
    w_out32 = w_out.astype(F32)
    gpre = norm_pre.astype(F32)[:, None, :]
    gpost = norm_post.astype(F32)[:, None, :]
    gon = hgrn_onorm.astype(F32)[:, None, :]
    lb_logits = hgrn_lb_logits.astype(F32)
    state_hgrn = state_hgrn.astype(F32)
    conv_rows = state_conv.astype(F32).reshape(state_conv.shape[0], state_conv.shape[1], -1)
    nb, seq, d = x_sample.shape
    xp, xs = x_prompt, x_sample.reshape(nb * seq, d)
    conv_p, conv_s, hgrn_p, hgrn_s = [], [], [], None
    for i in range(depth):
        j = i // 2
        if i % 2 == 0:
            xp, bp = _conv_prompt(i, j, xp, gpre, w_in32, conv_w, w_out32, gpost)
            xs, bs = _conv_sample(i, j, xs, seq, conv_rows, gpre, w_in32, conv_w, w_out32, gpost)
            conv_p.append(bp)
            conv_s.append(bs)
        else:
            xp, sp = _hgrn_prompt(i, j, xp, gpre, w_in32, lb_logits, gon, w_out32, gpost)
            xs, hgrn_s = _hgrn_sample(i, j, xs, seq, state_hgrn, hgrn_s, gpre, w_in32, lb_logits, gon,
                                      w_out32, gpost)
            hgrn_p.append(sp)
    return (xp, xs.reshape(nb, seq, d), jnp.stack(conv_p), jnp.stack(conv_s), jnp.stack(hgrn_p), hgrn_s)
```

```python
import functools
import math

import jax
import jax.numpy as jnp
from jax import lax
from jax.experimental import pallas as pl
from jax.experimental.pallas import tpu as pltpu

F32 = jnp.float32
BF16 = jnp.bfloat16

RMS_EPS = 1e-6
LOG_F_FLOOR = -20.0
LOG2_E = math.log2(math.e)
HEAD_DIM = 128
LANES = 128
MXU_WIDTH = 256
CONV_W = 3
HGRN_CHUNK = 128
BASE_BLOCK = 8
SCORE_LOOKAHEAD = 2
PROMPT_TILE = 256
CONV_PROMPT_TILE = 512
SAMPLE_SEQS_PER_STEP = 128
SAMPLE_PAIRS_PER_ITER = 8
VMEM_LIMIT_BYTES = 56 * 1024 * 1024


def _rms(x, g):
    ms = jnp.mean(x * x, axis=-1, keepdims=True)
    return x * lax.rsqrt(ms + RMS_EPS) * g


def _silu(x):
    return x * (1.0 / (1.0 + jnp.exp2(x * (-LOG2_E))))


def _dot(a, b):
    return jnp.dot(a, b, preferred_element_type=F32)


def _dot_nt(a, b):
    return lax.dot_general(a, b, (((1,), (1,)), ((), ())), preferred_element_type=F32)


def _dot_tn(a, b):
    return lax.dot_general(a, b, (((0,), (0,)), ((), ())), preferred_element_type=F32)


def _lower_bound(logits, j):
    m = jnp.max(logits, axis=0, keepdims=True)
    e = jnp.exp(logits - m)
    p = e / jnp.sum(e, axis=0, keepdims=True)
    if j == 0:
        return jnp.zeros_like(p[0:1])
    return jnp.sum(p[1:j + 1], axis=0, keepdims=True)


def _forget_gate(fpre, lb):
    e = jnp.exp2(jnp.abs(fpre) * (-LOG2_E))
    r = 1.0 / (1.0 + e)
    er = e * r
    pos = fpre >= 0
    sig = jnp.where(pos, r, er)
    nsig = jnp.where(pos, er, r)
    f = lb + (1.0 - lb) * sig
    log2_f = jnp.maximum(jnp.log(f) * LOG2_E, LOG_F_FLOOR * LOG2_E)
    k = jnp.minimum((1.0 - lb) * nsig, 1.0 - math.exp(LOG_F_FLOOR))
    return log2_f, k


def _cumsum_rows(x, group):
    pos = lax.broadcasted_iota(jnp.int32, x.shape, 0) & (group - 1)
    s = 1
    while s < group:
        x = x + jnp.where(pos >= s, pltpu.roll(x, s, axis=0), 0.0)
        s *= 2
    return x


def _group_allsum(x, group):
    n = x.shape[0]
    pos = lax.broadcasted_iota(jnp.int32, x.shape, 0) & (group - 1)
    s = 1
    while s < group:
        partner = jnp.where((pos & s) == 0, pltpu.roll(x, n - s, axis=0), pltpu.roll(x, s, axis=0))
        x = x + partner
        s *= 2
    return x


def _level_matrix(n):
    t = lax.broadcasted_iota(jnp.int32, (n, n), 0)
    s = lax.broadcasted_iota(jnp.int32, (n, n), 1)
    sh = BASE_BLOCK.bit_length() - 1
    lv = jnp.where(((t >> sh) == (s >> sh)) & (t >= s), 1, 0)
    c, i = 2 * BASE_BLOCK, 2
    while c <= n:
        half, sh = c // 2, sh + 1
        own = ((t >> sh) == (s >> sh)) & ((t & half) != 0) & ((s & half) == 0)
        lv = jnp.where(own, i, lv)
        c, i = 2 * c, i + 1
    return lv


def _level_sizes(chunk):
    sizes, c = [], 2 * BASE_BLOCK
    while c <= chunk:
        sizes.append(c)
        c *= 2
    return sizes


def _chunk_operands(q, k, lf2):
    c = q.shape[0]
    local = _cumsum_rows(lf2, BASE_BLOCK)
    b_parts, x_parts, carry = [], [], None
    for g in range(c // BASE_BLOCK):
        blk = local[g * BASE_BLOCK:(g + 1) * BASE_BLOCK]
        x_parts.append(blk - blk[BASE_BLOCK // 2 - 1:BASE_BLOCK // 2])
        if carry is not None:
            blk = blk + carry
        b_parts.append(blk)
        carry = blk[BASE_BLOCK - 1:BASE_BLOCK]
    b = jnp.concatenate(b_parts, axis=0)
    x = jnp.concatenate(x_parts, axis=0)
    b_last = carry
    qe = (q * jnp.exp2(b)).astype(BF16)
    kd = (k * jnp.exp2(b_last - b)).astype(BF16)
    q8 = (q * jnp.exp2(x)).astype(BF16)
    k8 = (k * jnp.exp2(-x)).astype(BF16)
    w = []
    for size in _level_sizes(c):
        half = size // 2
        parts = []
        for s0 in range(0, c, size):
            r = b[s0 + half - 1:s0 + half]
            lo, hi = slice(s0, s0 + half), slice(s0 + half, s0 + size)
            parts.append(k[lo] * jnp.exp2(r - b[lo]))
            parts.append(q[hi] * jnp.exp2(b[hi] - r))
        w.append(jnp.concatenate(parts, axis=0).astype(BF16))
    return qe, kd, q8, k8, w, jnp.exp2(b_last)


def _conv_mix(h, win_ref, cw_ref, prev1, prev2, pos):
    e = win_ref.shape[1] // 4
    v = _dot(h, win_ref[:, 0:e])
    cg = _dot(h, win_ref[:, 2 * e:3 * e])
    u = cg * v
    u1 = jnp.where(pos >= 1, pltpu.roll(u, 1, axis=0), prev1)
    u2 = jnp.where(pos >= 2, pltpu.roll(u, 2, axis=0), prev2)
    conv = cw_ref[0:1, :] * u2 + cw_ref[1:2, :] * u1 + cw_ref[2:3, :] * u
    bg = _dot(h, win_ref[:, e:2 * e])
    z = _dot(h, win_ref[:, 3 * e:4 * e])
    return bg * conv * _silu(z), u


def _conv_prompt_kernel(x_ref, gpre_ref, win_in_ref, cw_ref, wout_in_ref, gpost_ref,
                        y_ref, buf_ref, tail_ref, win_ref, wout_ref):
    @pl.when(jnp.logical_and(pl.program_id(0) == 0, pl.program_id(1) == 0))
    def _():
        win_ref[...] = win_in_ref[...].astype(BF16)
        wout_ref[...] = wout_in_ref[...].astype(BF16)

    @pl.when(pl.program_id(1) == 0)
    def _():
        tail_ref[...] = jnp.zeros_like(tail_ref)

    x = x_ref[0]
    tm = x.shape[0]
    h = _rms(x, gpre_ref[...]).astype(BF16)
    pos = lax.broadcasted_iota(jnp.int32, (tm, win_ref.shape[1] // 4), 0)
    tail = tail_ref[...]
    prev1 = tail[7:8, :]
    prev2 = jnp.where(pos == 0, tail[6:7, :], tail[7:8, :])
    g, u = _conv_mix(h, win_ref, cw_ref, prev1, prev2, pos)
    y = _dot(g.astype(BF16), wout_ref[...])
    y_ref[0] = x + _rms(y, gpost_ref[...])
    tail_ref[...] = u[tm - 8:tm, :]
    buf_ref[0] = u[tm - (CONV_W - 1):tm, :]


def _conv_sample_kernel(seq, x_ref, buf_ref, gpre_ref, win_in_ref, cw_ref, wout_in_ref, gpost_ref,
                        y_ref, new_buf_ref, win_ref, wout_ref, p1_s, p2_s):
    win_ref[...] = win_in_ref[...].astype(BF16)
    wout_ref[...] = wout_in_ref[...].astype(BF16)
    x = x_ref[...]
    nb = buf_ref.shape[0]
    lane_tiles, n, lanes = p1_s.shape
    e = lane_tiles * lanes
    h = _rms(x, gpre_ref[...]).astype(BF16)
    pos = lax.broadcasted_iota(jnp.int32, (n, e), 0) & (seq - 1)
    p1_s[...] = jnp.zeros_like(p1_s)
    p2_s[...] = jnp.zeros_like(p2_s)
    for c in range(lane_tiles):
        buf0 = buf_ref[:, c * lanes:(c + 1) * lanes]
        buf1 = buf_ref[:, e + c * lanes:e + (c + 1) * lanes]
        p1_s[c, pl.ds(0, nb, stride=seq), :] = buf1
        p2_s[c, pl.ds(0, nb, stride=seq), :] = buf0
        p2_s[c, pl.ds(1, nb, stride=seq), :] = buf1
    prev1 = jnp.concatenate([p1_s[c] for c in range(lane_tiles)], axis=1)
    prev2 = jnp.concatenate([p2_s[c] for c in range(lane_tiles)], axis=1)
    g, u = _conv_mix(h, win_ref, cw_ref, prev1, prev2, pos)
    y = _dot(g.astype(BF16), wout_ref[...])
    y_ref[...] = x + _rms(y, gpost_ref[...])
    for c in range(lane_tiles):
        p1_s[c] = u[:, c * lanes:(c + 1) * lanes]
        for r in range(CONV_W - 1):
            new_buf_ref[:, r * e + c * lanes:r * e + (c + 1) * lanes] = (
                p1_s[c, pl.ds(seq - (CONV_W - 1) + r, nb, stride=seq), :])


def _hgrn_prompt_kernel(j, nt, xp_ref, xr_ref, gpre_ref, win_in_ref, lbl_ref, gon_ref, wout_in_ref, gpost_ref,
                        y_ref, sout_ref, win_ref, wout_ref, st_s, dec_s, raw_s, gate_s,
                        qe_s, kd_s, q8_s, k8_s, v_s, vt_s, *w_s):
    g = pl.program_id(0)
    e = win_ref.shape[1] // 4
    heads = e // HEAD_DIM
    tm = xp_ref.shape[1]
    fin = g - 2

    @pl.when(g == 0)
    def _():
        win_ref[...] = win_in_ref[...].astype(BF16)
        wout_ref[...] = wout_in_ref[...].astype(BF16)
        for ref in (dec_s, raw_s, gate_s, qe_s, kd_s, q8_s, k8_s, v_s, vt_s) + tuple(w_s):
            ref[...] = jnp.zeros_like(ref)

    @pl.when(jnp.logical_or(g < 2, lax.rem(fin, nt) == 0))
    def _():
        st_s[...] = jnp.zeros_like(st_s)

    lv = _level_matrix(HGRN_CHUNK)
    owned = [lv == i + 1 for i in range(len(w_s) + 1)]
    gon = gon_ref[...]
    chunks = tm // HGRN_CHUNK

    def chunk_scores(c, hd):
        rows = slice(c * HGRN_CHUNK, (c + 1) * HGRN_CHUNK)
        cols = slice(hd * HEAD_DIM, (hd + 1) * HEAD_DIM)
        a = jnp.where(owned[0], _dot_nt(q8_s[rows, cols], k8_s[rows, cols]), 0.0)
        for i, w_ref in enumerate(w_s):
            wv = w_ref[rows, cols]
            a = jnp.where(owned[i + 1], _dot_nt(wv, wv), a)
        return a.astype(BF16)

    def chunk_output(c, hd, a):
        rows = slice(c * HGRN_CHUNK, (c + 1) * HGRN_CHUNK)
        cols = slice(hd * HEAD_DIM, (hd + 1) * HEAD_DIM)
        vb = v_s[rows, cols]
        st = st_s[hd]
        o = _dot(a, vb) + _dot_nt(qe_s[rows, cols], st.astype(BF16))
        st3 = st.reshape(HEAD_DIM // 8, 8, HEAD_DIM) * dec_s[c, :, cols]
        st_s[hd] = st3.reshape(HEAD_DIM, HEAD_DIM) + _dot(vt_s[rows, cols], kd_s[rows, cols])
        return (_rms(o, gon) * gate_s[rows, cols]).astype(BF16)

    h = _rms(xp_ref[0], gpre_ref[...]).astype(BF16)
    lb = _lower_bound(lbl_ref[...], j)
    out_blocks = [[None] * heads for _ in range(chunks)]
    groups = e // MXU_WIDTH

    def part_cols(cg):
        return [slice(part * e + cg * MXU_WIDTH, part * e + (cg + 1) * MXU_WIDTH) for part in range(4)]


    def stage(cg, c, q_raw, f_raw, v_raw, z_raw):
        gc = slice(cg * MXU_WIDTH, (cg + 1) * MXU_WIDTH)
        rows = slice(c * HGRN_CHUNK, (c + 1) * HGRN_CHUNK)
        q = _silu(q_raw) * (HEAD_DIM ** -0.5)
        lf2, k = _forget_gate(f_raw, lb[:, gc])
        v_s[rows, gc] = v_raw.astype(BF16)
        for hd in range(MXU_WIDTH // HEAD_DIM):
            lanes = slice(hd * HEAD_DIM, (hd + 1) * HEAD_DIM)
            vt_s[rows, cg * MXU_WIDTH + hd * HEAD_DIM:cg * MXU_WIDTH + (hd + 1) * HEAD_DIM] = (
                v_raw[:, lanes].T.astype(BF16))
        gate_s[rows, gc] = _silu(z_raw)
        qe, kd, q8, k8, w, decay = _chunk_operands(q, k, lf2)
        qe_s[rows, gc] = qe
        kd_s[rows, gc] = kd
        q8_s[rows, gc] = q8
        k8_s[rows, gc] = k8
        for w_ref, w_val in zip(w_s, w):
            w_ref[rows, gc] = w_val
        dec_s[c, :, gc] = jnp.broadcast_to(decay, (8, MXU_WIDTH))

    def project(cols):
        raw_s[:, cols] = _dot(h, win_ref[:, cols])

    def output():
        gated = jnp.concatenate([jnp.concatenate(blocks, axis=1) for blocks in out_blocks], axis=0)
        y = _dot(gated, wout_ref[...])
        y_ref[0] = xr_ref[0] + _rms(y, gpost_ref[...])

    heads_per_group = MXU_WIDTH // HEAD_DIM
    raws = {}
    for cg in range(groups):
        for c in range(chunks):
            rows = slice(c * HGRN_CHUNK, (c + 1) * HGRN_CHUNK)
            raws[cg, c] = [raw_s[rows, cols] for cols in part_cols(cg)]
    projections = [cols for cg in range(groups) for cols in part_cols(cg)]
    units = [(cg, c, hd) for cg in range(groups) for c in range(chunks)
             for hd in range(cg * heads_per_group, (cg + 1) * heads_per_group)]
    scores = {}
    for n in range(len(units) + SCORE_LOOKAHEAD):
        if n < len(units):
            _, c, hd = units[n]
            scores[c, hd] = chunk_scores(c, hd)
        if projections:
            project(projections.pop(0))
        if n >= SCORE_LOOKAHEAD:
            done = n - SCORE_LOOKAHEAD
            cg, c, hd = units[done]
            out_blocks[c][hd] = chunk_output(c, hd, scores.pop((c, hd)))
            if (done + 1) % heads_per_group == 0:
                stage(cg, c, *raws.pop((cg, c)))
    for cols in projections:
        project(cols)
    output()

    @pl.when(jnp.logical_and(g >= 2, lax.rem(fin, nt) == nt - 1))
    def _():
        for hd in range(heads):
            sout_ref[0, hd] = st_s[hd].T


def _hgrn_sample_kernel(j, seq, x_ref, gpre_ref, wq_ref, wf_ref, wv_ref, wz_ref, lbl_ref, gon_ref,
                        wout_ref, gpost_ref, s0_ref, *rest):
    y_ref, s1_ref, h_s, qe_s, kd_s, decp_s, v_s, o_s, gate_s = rest[-9:]
    hd, blk = pl.program_id(0), pl.program_id(1)
    n = x_ref.shape[0]
    nseq = s0_ref.shape[0]
    heads = o_s.shape[0]

    @pl.when(jnp.logical_and(hd == 0, blk == 0))
    def _():
        h_s[...] = _rms(x_ref[...], gpre_ref[...]).astype(BF16)

    @pl.when(blk == 0)
    def _():
        h = h_s[...]
        lb = _lower_bound(lbl_ref[...], j)
        q = _silu(_dot(h, wq_ref[...].astype(BF16))) * (HEAD_DIM ** -0.5)
        lf2, k = _forget_gate(_dot(h, wf_ref[...].astype(BF16)), lb)
        v = _dot(h, wv_ref[...].astype(BF16))
        gate_s[hd] = _silu(_dot(h, wz_ref[...].astype(BF16)))
        t = lax.broadcasted_iota(jnp.int32, (n, n), 0)
        s = lax.broadcasted_iota(jnp.int32, (n, n), 1)
        sh = seq.bit_length() - 1
        same_seq_causal = ((t >> sh) == (s >> sh)) & (t >= s)
        row = lax.broadcasted_iota(jnp.int32, (n, HEAD_DIM), 0)
        pos = row & (seq - 1)
        b = _cumsum_rows(lf2, seq)
        b_last = _group_allsum(lf2, seq)
        x_mid = b - _group_allsum(jnp.where(pos < seq // 2, lf2, 0.0), seq)
        a = _dot_nt((q * jnp.exp2(x_mid)).astype(BF16), (k * jnp.exp2(-x_mid)).astype(BF16))
        a = jnp.where(same_seq_causal, a, 0.0)
        o_s[hd] = _dot(a.astype(BF16), v.astype(BF16))
        qe_s[...] = q * jnp.exp2(b)
        kd_s[...] = k * jnp.exp2(b_last - b)
        v_s[...] = v
        dec = jnp.exp2(b_last)
        dsw = jnp.where((row & 7) < 4, pltpu.roll(dec, n - 4, axis=0), pltpu.roll(dec, 4, axis=0))
        hi = dsw.astype(BF16).astype(F32)
        mid = (dsw - hi).astype(BF16).astype(F32)
        lo = (dsw - hi - mid).astype(BF16).astype(F32)
        decp_s[...] = jnp.where(pos == 0, hi, jnp.where(pos == 1, mid, jnp.where(pos == 2, lo, 0.0)))

    row = lax.broadcasted_iota(jnp.int32, (8, HEAD_DIM), 0)
    pad = jnp.zeros((HEAD_DIM - 8, HEAD_DIM), F32)
    pad2 = jnp.zeros((HEAD_DIM - 8, 2 * HEAD_DIM), F32)
    base = blk * (nseq * seq)

    def pairs_body(it, carry):
        for u in range(SAMPLE_PAIRS_PER_ITER):
            p = it * SAMPLE_PAIRS_PER_ITER + u
            rows = pl.ds(pl.multiple_of(base + p * 8, 8), 8)
            qe8 = qe_s[rows, :].astype(BF16)
            kd8 = kd_s[rows, :]
            dp8 = decp_s[rows, :]
            v8 = v_s[rows, :]
            o8 = o_s[hd, rows, :]
            for half in range(2):
                mine = (row >= 4) if half else (row < 4)
                s0 = s0_ref[2 * p + half]
                o8 = o8 + jnp.where(mine, _dot(qe8, s0.astype(BF16)), 0.0)
                lhs = jnp.concatenate([jnp.where(mine, kd8, dp8), pad], axis=0).astype(BF16)
                ones = jnp.where(mine, 0.0, 1.0)
                rhs = jnp.concatenate(
                    [jnp.concatenate([jnp.where(mine, v8, 0.0), ones], axis=1), pad2], axis=0).astype(BF16)
                upd = _dot_tn(lhs, rhs)
                s1_ref[2 * p + half] = upd[:, HEAD_DIM:] * s0 + upd[:, :HEAD_DIM]
            o_s[hd, rows, :] = o8
        return carry

    lax.fori_loop(0, nseq // (2 * SAMPLE_PAIRS_PER_ITER), pairs_body, 0)

    @pl.when(jnp.logical_and(hd == heads - 1, blk == pl.num_programs(1) - 1))
    def _():
        gon = gon_ref[...]
        gated = jnp.concatenate(
            [(_rms(o_s[i], gon) * gate_s[i]).astype(BF16) for i in range(heads)], axis=1)
        y = _dot(gated, wout_ref[...].astype(BF16))
        y_ref[...] = x_ref[...] + _rms(y, gpost_ref[...])


def _const_spec(shape):
    nd = len(shape)
    return pl.BlockSpec(shape, lambda *_: (0,) * nd)


def _layer_spec(arr, layer):
    nd = arr.ndim - 1
    return pl.BlockSpec((None,) + arr.shape[1:], lambda *_: (layer,) + (0,) * nd)


def _params(sem):
    return pltpu.CompilerParams(dimension_semantics=sem, vmem_limit_bytes=VMEM_LIMIT_BYTES)


def _conv_prompt(i, j, x, gpre, win, cw, wout, gpost):
    bsz, t, d = x.shape
    e = wout.shape[1]
    tm = CONV_PROMPT_TILE
    assert t % tm == 0 and tm % 8 == 0
    return pl.pallas_call(
        _conv_prompt_kernel,
        grid=(bsz, t // tm),
        in_specs=[pl.BlockSpec((1, tm, d), lambda b, s: (b, s, 0)),
                  _layer_spec(gpre, i), _layer_spec(win, i), _layer_spec(cw, j),
                  _layer_spec(wout, i), _layer_spec(gpost, i)],
        out_specs=[pl.BlockSpec((1, tm, d), lambda b, s: (b, s, 0)),
                   pl.BlockSpec((1, CONV_W - 1, e), lambda b, s: (b, 0, 0))],
        out_shape=[jax.ShapeDtypeStruct(x.shape, F32),
                   jax.ShapeDtypeStruct((bsz, CONV_W - 1, e), F32)],
        scratch_shapes=[pltpu.VMEM((8, e), F32),
                        pltpu.VMEM(win.shape[1:], BF16), pltpu.VMEM(wout.shape[1:], BF16)],
        compiler_params=_params(("arbitrary", "arbitrary")),
        name="conv_prompt",
    )(x, gpre, win, cw, wout, gpost)


def _conv_sample(i, j, xf, seq, state, gpre, win, cw, wout, gpost):
    n, d = xf.shape
    nb = n // seq
    e = wout.shape[1]
    assert CONV_W == 3 and seq & (seq - 1) == 0 and seq >= CONV_W - 1
    y, new_buf = pl.pallas_call(
        functools.partial(_conv_sample_kernel, seq),
        grid=(1,),
        in_specs=[_const_spec(xf.shape), _layer_spec(state, j),
                  _layer_spec(gpre, i), _layer_spec(win, i), _layer_spec(cw, j),
                  _layer_spec(wout, i), _layer_spec(gpost, i)],
        out_specs=[_const_spec((n, d)), _const_spec((nb, (CONV_W - 1) * e))],
        out_shape=[jax.ShapeDtypeStruct((n, d), F32),
                   jax.ShapeDtypeStruct((nb, (CONV_W - 1) * e), F32)],
        scratch_shapes=[pltpu.VMEM(win.shape[1:], BF16), pltpu.VMEM(wout.shape[1:], BF16),
                        pltpu.VMEM((e // LANES, n, LANES), F32),
                        pltpu.VMEM((e // LANES, n, LANES), F32)],
        compiler_params=_params(("arbitrary",)),
        name="conv_sample",
    )(xf, state, gpre, win, cw, wout, gpost)
    return y, new_buf.reshape(nb, CONV_W - 1, e)


def _hgrn_prompt(i, j, x, gpre, win, lb_logits, gon, wout, gpost):
    bsz, t, d = x.shape
    e = wout.shape[1]
    heads = e // HEAD_DIM
    tm = PROMPT_TILE
    assert t % tm == 0 and tm % HGRN_CHUNK == 0
    nt = t // tm
    n_tiles = bsz * nt
    n_bf16 = 6 + len(_level_sizes(HGRN_CHUNK))
    scratch = ([pltpu.VMEM(win.shape[1:], BF16), pltpu.VMEM(wout.shape[1:], BF16),
                pltpu.VMEM((heads, HEAD_DIM, HEAD_DIM), F32),
                pltpu.VMEM((tm // HGRN_CHUNK, 8, e), F32),
                pltpu.VMEM((tm, 4 * e), F32),
                pltpu.VMEM((tm, e), F32)]
               + [pltpu.VMEM((tm, e), BF16) for _ in range(n_bf16)])

    def projected(g):
        tile = jnp.minimum(g, n_tiles - 1)
        return tile // nt, tile % nt

    def finished(g):
        tile = jnp.maximum(g - 2, 0)
        return tile // nt, tile % nt

    return pl.pallas_call(
        functools.partial(_hgrn_prompt_kernel, j, nt),
        grid=(n_tiles + 2,),
        in_specs=[pl.BlockSpec((1, tm, d), lambda g: projected(g) + (0,)),
                  pl.BlockSpec((1, tm, d), lambda g: finished(g) + (0,)),
                  _layer_spec(gpre, i), _layer_spec(win, i), _const_spec(lb_logits.shape),
                  _layer_spec(gon, j), _layer_spec(wout, i), _layer_spec(gpost, i)],
        out_specs=[pl.BlockSpec((1, tm, d), lambda g: finished(g) + (0,)),
                   pl.BlockSpec((1, heads, HEAD_DIM, HEAD_DIM), lambda g: (finished(g)[0], 0, 0, 0))],
        out_shape=[jax.ShapeDtypeStruct(x.shape, F32),
                   jax.ShapeDtypeStruct((bsz, heads, HEAD_DIM, HEAD_DIM), F32)],
        scratch_shapes=scratch,
        compiler_params=_params(("arbitrary",)),
        name="hgrn_prompt",
    )(x, x, gpre, win, lb_logits, gon, wout, gpost)


def _hgrn_sample(i, j, xf, seq, state, states_out, gpre, win, lb_logits, gon, wout, gpost):
    n, d = xf.shape
    nb = n // seq
    e = wout.shape[1]
    heads = e // HEAD_DIM
    g = SAMPLE_SEQS_PER_STEP
    assert seq == 4 and nb % g == 0 and g % 2 == 0

    def head_cols(part):
        return pl.BlockSpec((None, d, HEAD_DIM), lambda hd, blk: (i, 0, part * heads + hd))

    st_spec = pl.BlockSpec((None, g, None, HEAD_DIM, HEAD_DIM), lambda hd, blk: (j, blk, hd, 0, 0))
    args = [xf, gpre, win, win, win, win, lb_logits, gon, wout, gpost, state]
    in_specs = [_const_spec(xf.shape), _layer_spec(gpre, i)] + [head_cols(part) for part in range(4)] + [
        pl.BlockSpec((lb_logits.shape[0], HEAD_DIM), lambda hd, blk: (0, hd)),
        _layer_spec(gon, j), _layer_spec(wout, i), _layer_spec(gpost, i), st_spec]
    aliases = {}
    if states_out is not None:
        args.append(states_out)
        in_specs.append(pl.BlockSpec(memory_space=pl.ANY))
        aliases = {len(args) - 1: 1}
    y, states_out = pl.pallas_call(
        functools.partial(_hgrn_sample_kernel, j, seq),
        grid=(heads, nb // g),
        in_specs=in_specs,
        out_specs=[_const_spec((n, d)), st_spec],
        out_shape=[jax.ShapeDtypeStruct((n, d), F32), jax.ShapeDtypeStruct(state.shape, F32)],
        scratch_shapes=[pltpu.VMEM((n, d), BF16)]
                       + [pltpu.VMEM((n, HEAD_DIM), F32) for _ in range(4)]
                       + [pltpu.VMEM((heads, n, HEAD_DIM), F32) for _ in range(2)],
        input_output_aliases=aliases,
        compiler_params=_params(("arbitrary", "arbitrary")),
        name="hgrn_sample",
    )(*args)
    return y, states_out


def kernel(x_prompt, x_sample, state_conv, state_hgrn, norm_pre, w_in, conv_w, hgrn_lb_logits,
           hgrn_onorm, w_out, norm_post):
    depth = w_in.shape[0]
    w_in32 = w_in.astype(F32)
    w_out32 = w_out.astype(F32)
    gpre = norm_pre.astype(F32)[:, None, :]
    gpost = norm_post.astype(F32)[:, None, :]
    gon = hgrn_onorm.astype(F32)[:, None, :]
    lb_logits = hgrn_lb_logits.astype(F32)
    state_hgrn = state_hgrn.astype(F32)
    conv_rows = state_conv.astype(F32).reshape(state_conv.shape[0], state_conv.shape[1], -1)
    nb, seq, d = x_sample.shape
    xp, xs = x_prompt, x_sample.reshape(nb * seq, d)
    conv_p, conv_s, hgrn_p, hgrn_s = [], [], [], None
    for i in range(depth):
        j = i // 2
        if i % 2 == 0:
            xp, bp = _conv_prompt(i, j, xp, gpre, w_in32, conv_w, w_out32, gpost)
            xs, bs = _conv_sample(i, j, xs, seq, conv_rows, gpre, w_in32, conv_w, w_out32, gpost)
            conv_p.append(bp)
            conv_s.append(bs)
        else:
            xp, sp = _hgrn_prompt(i, j, xp, gpre, w_in32, lb_logits, gon, w_out32, gpost)
            xs, hgrn_s = _hgrn_sample(i, j, xs, seq, state_hgrn, hgrn_s, gpre, w_in32, lb_logits, gon,
                                      w_out32, gpost)
            hgrn_p.append(sp)
    return (xp, xs.reshape(nb, seq, d), jnp.stack(conv_p), jnp.stack(conv_s), jnp.stack(hgrn_p), hgrn_s)
```

```python
import functools
import math

import jax
import jax.numpy as jnp
from jax import lax
from jax.experimental import pallas as pl
from jax.experimental.pallas import tpu as pltpu

F32 = jnp.float32
BF16 = jnp.bfloat16

RMS_EPS = 1e-6
LOG_F_FLOOR = -20.0
LOG2_E = math.log2(math.e)
HEAD_DIM = 128
LANES = 128
SUBLANES = 8
MXU_WIDTH = 256
F32_MAX_EXP2 = 127
CONV_W = 3
HGRN_CHUNK = 128
BASE_BLOCK = SUBLANES
assert BASE_BLOCK // 2 * -LOG_F_FLOOR * LOG2_E < F32_MAX_EXP2
SCORE_LOOKAHEAD = 2
PROMPT_TILE = 256
CONV_PROMPT_TILE = 512
SAMPLE_SEQS_PER_STEP = 128
SAMPLE_PAIRS_PER_ITER = 8
VMEM_LIMIT_BYTES = 56 * 1024 * 1024


def _rms(x, g):
    ms = jnp.mean(x * x, axis=-1, keepdims=True)
    return x * lax.rsqrt(ms + RMS_EPS) * g


def _silu(x):
    return x * (1.0 / (1.0 + jnp.exp2(x * (-LOG2_E))))


def _dot(a, b):
    return jnp.dot(a, b, preferred_element_type=F32)


def _dot_nt(a, b):
    return lax.dot_general(a, b, (((1,), (1,)), ((), ())), preferred_element_type=F32)


def _dot_tn(a, b):
    return lax.dot_general(a, b, (((0,), (0,)), ((), ())), preferred_element_type=F32)


def _lower_bound(logits, j):
    m = jnp.max(logits, axis=0, keepdims=True)
    e = jnp.exp(logits - m)
    p = e / jnp.sum(e, axis=0, keepdims=True)
    if j == 0:
        return jnp.zeros_like(p[0:1])
    return jnp.sum(p[1:j + 1], axis=0, keepdims=True)


def _forget_gate(fpre, lb):
    e = jnp.exp2(jnp.abs(fpre) * (-LOG2_E))
    r = 1.0 / (1.0 + e)
    er = e * r
    pos = fpre >= 0
    sig = jnp.where(pos, r, er)
    nsig = jnp.where(pos, er, r)
    f = lb + (1.0 - lb) * sig
    log2_f = jnp.maximum(jnp.log(f) * LOG2_E, LOG_F_FLOOR * LOG2_E)
    k = jnp.minimum((1.0 - lb) * nsig, 1.0 - math.exp(LOG_F_FLOOR))
    return log2_f, k


def _cumsum_rows(x, group):
    pos = lax.broadcasted_iota(jnp.int32, x.shape, 0) & (group - 1)
    s = 1
    while s < group:
        x = x + jnp.where(pos >= s, pltpu.roll(x, s, axis=0), 0.0)
        s *= 2
    return x


def _group_allsum(x, group):
    n = x.shape[0]
    pos = lax.broadcasted_iota(jnp.int32, x.shape, 0) & (group - 1)
    s = 1
    while s < group:
        partner = jnp.where((pos & s) == 0, pltpu.roll(x, n - s, axis=0), pltpu.roll(x, s, axis=0))
        x = x + partner
        s *= 2
    return x


def _level_matrix(n):
    t = lax.broadcasted_iota(jnp.int32, (n, n), 0)
    s = lax.broadcasted_iota(jnp.int32, (n, n), 1)
    sh = BASE_BLOCK.bit_length() - 1
    lv = jnp.where(((t >> sh) == (s >> sh)) & (t >= s), 1, 0)
    c, i = 2 * BASE_BLOCK, 2
    while c <= n:
        half, sh = c // 2, sh + 1
        own = ((t >> sh) == (s >> sh)) & ((t & half) != 0) & ((s & half) == 0)
        lv = jnp.where(own, i, lv)
        c, i = 2 * c, i + 1
    return lv


def _level_sizes(chunk):
    sizes, c = [], 2 * BASE_BLOCK
    while c <= chunk:
        sizes.append(c)
        c *= 2
    return sizes


def _chunk_operands(q, k, lf2):
    c = q.shape[0]
    local = _cumsum_rows(lf2, BASE_BLOCK)
    b_parts, x_parts, carry = [], [], None
    for g in range(c // BASE_BLOCK):
        blk = local[g * BASE_BLOCK:(g + 1) * BASE_BLOCK]
        x_parts.append(blk - blk[BASE_BLOCK // 2 - 1:BASE_BLOCK // 2])
        if carry is not None:
            blk = blk + carry
        b_parts.append(blk)
        carry = blk[BASE_BLOCK - 1:BASE_BLOCK]
    b = jnp.concatenate(b_parts, axis=0)
    x = jnp.concatenate(x_parts, axis=0)
    b_last = carry
    qe = (q * jnp.exp2(b)).astype(BF16)
    kd = (k * jnp.exp2(b_last - b)).astype(BF16)
    q8 = (q * jnp.exp2(x)).astype(BF16)
    k8 = (k * jnp.exp2(-x)).astype(BF16)
    w = []
    for size in _level_sizes(c):
        half = size // 2
        parts = []
        for s0 in range(0, c, size):
            r = b[s0 + half - 1:s0 + half]
            lo, hi = slice(s0, s0 + half), slice(s0 + half, s0 + size)
            parts.append(k[lo] * jnp.exp2(r - b[lo]))
            parts.append(q[hi] * jnp.exp2(b[hi] - r))
        w.append(jnp.concatenate(parts, axis=0).astype(BF16))
    return qe, kd, q8, k8, w, jnp.exp2(b_last)


def _conv_mix(h, win_ref, cw_ref, prev1, prev2, pos):
    e = win_ref.shape[1] // 4
    v = _dot(h, win_ref[:, 0:e])
    cg = _dot(h, win_ref[:, 2 * e:3 * e])
    u = cg * v
    u1 = jnp.where(pos >= 1, pltpu.roll(u, 1, axis=0), prev1)
    u2 = jnp.where(pos >= 2, pltpu.roll(u, 2, axis=0), prev2)
    conv = cw_ref[0:1, :] * u2 + cw_ref[1:2, :] * u1 + cw_ref[2:3, :] * u
    bg = _dot(h, win_ref[:, e:2 * e])
    z = _dot(h, win_ref[:, 3 * e:4 * e])
    return bg * conv * _silu(z), u


def _conv_prompt_kernel(x_ref, gpre_ref, win_in_ref, cw_ref, wout_in_ref, gpost_ref,
                        y_ref, buf_ref, tail_ref, win_ref, wout_ref):
    @pl.when(jnp.logical_and(pl.program_id(0) == 0, pl.program_id(1) == 0))
    def _():
        win_ref[...] = win_in_ref[...].astype(BF16)
        wout_ref[...] = wout_in_ref[...].astype(BF16)

    @pl.when(pl.program_id(1) == 0)
    def _():
        tail_ref[...] = jnp.zeros_like(tail_ref)

    x = x_ref[0]
    tm = x.shape[0]
    h = _rms(x, gpre_ref[...]).astype(BF16)
    pos = lax.broadcasted_iota(jnp.int32, (tm, win_ref.shape[1] // 4), 0)
    tail = tail_ref[...]
    prev1 = tail[7:8, :]
    prev2 = jnp.where(pos == 0, tail[6:7, :], tail[7:8, :])
    g, u = _conv_mix(h, win_ref, cw_ref, prev1, prev2, pos)
    y = _dot(g.astype(BF16), wout_ref[...])
    y_ref[0] = x + _rms(y, gpost_ref[...])
    tail_ref[...] = u[tm - 8:tm, :]
    buf_ref[0] = u[tm - (CONV_W - 1):tm, :]


def _conv_sample_kernel(seq, x_ref, buf_ref, gpre_ref, win_in_ref, cw_ref, wout_in_ref, gpost_ref,
                        y_ref, new_buf_ref, win_ref, wout_ref, p1_s, p2_s):
    win_ref[...] = win_in_ref[...].astype(BF16)
    wout_ref[...] = wout_in_ref[...].astype(BF16)
    x = x_ref[...]
    nb = buf_ref.shape[0]
    lane_tiles, n, lanes = p1_s.shape
    e = lane_tiles * lanes
    h = _rms(x, gpre_ref[...]).astype(BF16)
    pos = lax.broadcasted_iota(jnp.int32, (n, e), 0) & (seq - 1)
    p1_s[...] = jnp.zeros_like(p1_s)
    p2_s[...] = jnp.zeros_like(p2_s)
    for c in range(lane_tiles):
        buf0 = buf_ref[:, c * lanes:(c + 1) * lanes]
        buf1 = buf_ref[:, e + c * lanes:e + (c + 1) * lanes]
        p1_s[c, pl.ds(0, nb, stride=seq), :] = buf1
        p2_s[c, pl.ds(0, nb, stride=seq), :] = buf0
        p2_s[c, pl.ds(1, nb, stride=seq), :] = buf1
    prev1 = jnp.concatenate([p1_s[c] for c in range(lane_tiles)], axis=1)
    prev2 = jnp.concatenate([p2_s[c] for c in range(lane_tiles)], axis=1)
    g, u = _conv_mix(h, win_ref, cw_ref, prev1, prev2, pos)
    y = _dot(g.astype(BF16), wout_ref[...])
    y_ref[...] = x + _rms(y, gpost_ref[...])
    for c in range(lane_tiles):
        p1_s[c] = u[:, c * lanes:(c + 1) * lanes]
        for r in range(CONV_W - 1):
            new_buf_ref[:, r * e + c * lanes:r * e + (c + 1) * lanes] = (
                p1_s[c, pl.ds(seq - (CONV_W - 1) + r, nb, stride=seq), :])


def _hgrn_prompt_kernel(j, nt, xp_ref, xr_ref, gpre_ref, win_in_ref, lbl_ref, gon_ref, wout_in_ref, gpost_ref,
                        y_ref, sout_ref, win_ref, wout_ref, st_s, dec_s, raw_s, gate_s,
                        qe_s, kd_s, q8_s, k8_s, v_s, *w_s):
    g = pl.program_id(0)
    e = win_ref.shape[1] // 4
    heads = e // HEAD_DIM
    tm = xp_ref.shape[1]
    fin = g - 2

    @pl.when(g == 0)
    def _():
        win_ref[...] = win_in_ref[...].astype(BF16)
        wout_ref[...] = wout_in_ref[...].astype(BF16)
        for ref in (dec_s, raw_s, gate_s, qe_s, kd_s, q8_s, k8_s, v_s) + tuple(w_s):
            ref[...] = jnp.zeros_like(ref)

    @pl.when(jnp.logical_or(g < 2, lax.rem(fin, nt) == 0))
    def _():
        st_s[...] = jnp.zeros_like(st_s)

    lv = _level_matrix(HGRN_CHUNK)
    owned = [lv == i + 1 for i in range(len(w_s) + 1)]
    gon = gon_ref[...]
    chunks = tm // HGRN_CHUNK

    def chunk_scores(c, hd):
        rows = slice(c * HGRN_CHUNK, (c + 1) * HGRN_CHUNK)
        cols = slice(hd * HEAD_DIM, (hd + 1) * HEAD_DIM)
        a = jnp.where(owned[0], _dot_nt(q8_s[rows, cols], k8_s[rows, cols]), 0.0)
        for i, w_ref in enumerate(w_s):
            wv = w_ref[rows, cols]
            a = jnp.where(owned[i + 1], _dot_nt(wv, wv), a)
        return a.astype(BF16)

    def chunk_output(c, hd, a):
        rows = slice(c * HGRN_CHUNK, (c + 1) * HGRN_CHUNK)
        cols = slice(hd * HEAD_DIM, (hd + 1) * HEAD_DIM)
        vb = v_s[rows, cols]
        st = st_s[hd]
        o = _dot(a, vb) + _dot_nt(qe_s[rows, cols], st.astype(BF16))
        st3 = st.reshape(HEAD_DIM // SUBLANES, SUBLANES, HEAD_DIM) * dec_s[c, :, cols]
        st_s[hd] = st3.reshape(HEAD_DIM, HEAD_DIM) + _dot_tn(vb, kd_s[rows, cols])
        return (_rms(o, gon) * gate_s[rows, cols]).astype(BF16)

    h = _rms(xp_ref[0], gpre_ref[...]).astype(BF16)
    lb = _lower_bound(lbl_ref[...], j)
    out_blocks = [[None] * heads for _ in range(chunks)]
    groups = e // MXU_WIDTH

    def part_cols(cg):
        return [slice(part * e + cg * MXU_WIDTH, part * e + (cg + 1) * MXU_WIDTH) for part in range(4)]


    def stage(cg, c, q_raw, f_raw, v_raw, z_raw):
        gc = slice(cg * MXU_WIDTH, (cg + 1) * MXU_WIDTH)
        rows = slice(c * HGRN_CHUNK, (c + 1) * HGRN_CHUNK)
        q = _silu(q_raw) * (HEAD_DIM ** -0.5)
        lf2, k = _forget_gate(f_raw, lb[:, gc])
        v_s[rows, gc] = v_raw.astype(BF16)
        gate_s[rows, gc] = _silu(z_raw)
        qe, kd, q8, k8, w, decay = _chunk_operands(q, k, lf2)
        qe_s[rows, gc] = qe
        kd_s[rows, gc] = kd
        q8_s[rows, gc] = q8
        k8_s[rows, gc] = k8
        for w_ref, w_val in zip(w_s, w):
            w_ref[rows, gc] = w_val
        dec_s[c, :, gc] = jnp.broadcast_to(decay, (SUBLANES, MXU_WIDTH))

    def project(cols):
        raw_s[:, cols] = _dot(h, win_ref[:, cols])

    def output():
        gated = jnp.concatenate([jnp.concatenate(blocks, axis=1) for blocks in out_blocks], axis=0)
        y = _dot(gated, wout_ref[...])
        y_ref[0] = xr_ref[0] + _rms(y, gpost_ref[...])

    heads_per_group = MXU_WIDTH // HEAD_DIM
    raws = {}
    for cg in range(groups):
        for c in range(chunks):
            rows = slice(c * HGRN_CHUNK, (c + 1) * HGRN_CHUNK)
            raws[cg, c] = [raw_s[rows, cols] for cols in part_cols(cg)]
    projections = [cols for cg in range(groups) for cols in part_cols(cg)]
    units = [(cg, c, hd) for cg in range(groups) for c in range(chunks)
             for hd in range(cg * heads_per_group, (cg + 1) * heads_per_group)]
    scores = {}
    for n in range(len(units) + SCORE_LOOKAHEAD):
        if n < len(units):
            _, c, hd = units[n]
            scores[c, hd] = chunk_scores(c, hd)
        if projections:
            project(projections.pop(0))
        if n >= SCORE_LOOKAHEAD:
            done = n - SCORE_LOOKAHEAD
            cg, c, hd = units[done]
            out_blocks[c][hd] = chunk_output(c, hd, scores.pop((c, hd)))
            if (done + 1) % heads_per_group == 0:
                stage(cg, c, *raws.pop((cg, c)))
    for cols in projections:
        project(cols)
    output()

    @pl.when(jnp.logical_and(g >= 2, lax.rem(fin, nt) == nt - 1))
    def _():
        for hd in range(heads):
            sout_ref[0, hd] = st_s[hd].T


def _hgrn_sample_kernel(j, seq, x_ref, gpre_ref, wq_ref, wf_ref, wv_ref, wz_ref, lbl_ref, gon_ref,
                        wout_ref, gpost_ref, s0_ref, *rest):
    y_ref, s1_ref, h_s, qe_s, kd_s, decp_s, v_s, o_s, gate_s = rest[-9:]
    hd, blk = pl.program_id(0), pl.program_id(1)
    n = x_ref.shape[0]
    nseq = s0_ref.shape[0]
    heads = o_s.shape[0]

    @pl.when(jnp.logical_and(hd == 0, blk == 0))
    def _():
        h_s[...] = _rms(x_ref[...], gpre_ref[...]).astype(BF16)

    @pl.when(blk == 0)
    def _():
        h = h_s[...]
        lb = _lower_bound(lbl_ref[...], j)
        q = _silu(_dot(h, wq_ref[...].astype(BF16))) * (HEAD_DIM ** -0.5)
        lf2, k = _forget_gate(_dot(h, wf_ref[...].astype(BF16)), lb)
        v = _dot(h, wv_ref[...].astype(BF16))
        gate_s[hd] = _silu(_dot(h, wz_ref[...].astype(BF16)))
        t = lax.broadcasted_iota(jnp.int32, (n, n), 0)
        s = lax.broadcasted_iota(jnp.int32, (n, n), 1)
        sh = seq.bit_length() - 1
        same_seq_causal = ((t >> sh) == (s >> sh)) & (t >= s)
        row = lax.broadcasted_iota(jnp.int32, (n, HEAD_DIM), 0)
        pos = row & (seq - 1)
        b = _cumsum_rows(lf2, seq)
        b_last = _group_allsum(lf2, seq)
        x_mid = b - _group_allsum(jnp.where(pos < seq // 2, lf2, 0.0), seq)
        a = _dot_nt((q * jnp.exp2(x_mid)).astype(BF16), (k * jnp.exp2(-x_mid)).astype(BF16))
        a = jnp.where(same_seq_causal, a, 0.0)
        o_s[hd] = _dot(a.astype(BF16), v.astype(BF16))
        qe_s[...] = q * jnp.exp2(b)
        kd_s[...] = k * jnp.exp2(b_last - b)
        v_s[...] = v
        dec = jnp.exp2(b_last)
        dsw = jnp.where((row & (SUBLANES - 1)) < seq,
                        pltpu.roll(dec, n - seq, axis=0), pltpu.roll(dec, seq, axis=0))
        hi = dsw.astype(BF16).astype(F32)
        mid = (dsw - hi).astype(BF16).astype(F32)
        lo = (dsw - hi - mid).astype(BF16).astype(F32)
        decp_s[...] = jnp.where(pos == 0, hi, jnp.where(pos == 1, mid, jnp.where(pos == 2, lo, 0.0)))

    row = lax.broadcasted_iota(jnp.int32, (SUBLANES, HEAD_DIM), 0)
    pad = jnp.zeros((HEAD_DIM - SUBLANES, HEAD_DIM), F32)
    pad2 = jnp.zeros((HEAD_DIM - SUBLANES, 2 * HEAD_DIM), F32)
    base = blk * (nseq * seq)

    def pairs_body(it, carry):
        for u in range(SAMPLE_PAIRS_PER_ITER):
            p = it * SAMPLE_PAIRS_PER_ITER + u
            rows = pl.ds(pl.multiple_of(base + p * SUBLANES, SUBLANES), SUBLANES)
            qe8 = qe_s[rows, :].astype(BF16)
            kd8 = kd_s[rows, :]
            dp8 = decp_s[rows, :]
            v8 = v_s[rows, :]
            o8 = o_s[hd, rows, :]
            for half in range(2):
                mine = (row >= seq) if half else (row < seq)
                s0 = s0_ref[2 * p + half]
                o8 = o8 + jnp.where(mine, _dot(qe8, s0.astype(BF16)), 0.0)
                lhs = jnp.concatenate([jnp.where(mine, kd8, dp8), pad], axis=0).astype(BF16)
                ones = jnp.where(mine, 0.0, 1.0)
                rhs = jnp.concatenate(
                    [jnp.concatenate([jnp.where(mine, v8, 0.0), ones], axis=1), pad2], axis=0).astype(BF16)
                upd = _dot_tn(lhs, rhs)
                s1_ref[2 * p + half] = upd[:, HEAD_DIM:] * s0 + upd[:, :HEAD_DIM]
            o_s[hd, rows, :] = o8
        return carry

    lax.fori_loop(0, nseq // (2 * SAMPLE_PAIRS_PER_ITER), pairs_body, 0)

    @pl.when(jnp.logical_and(hd == heads - 1, blk == pl.num_programs(1) - 1))
    def _():
        gon = gon_ref[...]
        gated = jnp.concatenate(
            [(_rms(o_s[i], gon) * gate_s[i]).astype(BF16) for i in range(heads)], axis=1)
        y = _dot(gated, wout_ref[...].astype(BF16))
        y_ref[...] = x_ref[...] + _rms(y, gpost_ref[...])


def _const_spec(shape):
    nd = len(shape)
    return pl.BlockSpec(shape, lambda *_: (0,) * nd)


def _layer_spec(arr, layer):
    nd = arr.ndim - 1
    return pl.BlockSpec((None,) + arr.shape[1:], lambda *_: (layer,) + (0,) * nd)


def _params(sem):
    return pltpu.CompilerParams(dimension_semantics=sem, vmem_limit_bytes=VMEM_LIMIT_BYTES)


def _conv_prompt(i, j, x, gpre, win, cw, wout, gpost):
    bsz, t, d = x.shape
    e = wout.shape[1]
    tm = CONV_PROMPT_TILE
    assert t % tm == 0 and tm % SUBLANES == 0
    return pl.pallas_call(
        _conv_prompt_kernel,
        grid=(bsz, t // tm),
        in_specs=[pl.BlockSpec((1, tm, d), lambda b, s: (b, s, 0)),
                  _layer_spec(gpre, i), _layer_spec(win, i), _layer_spec(cw, j),
                  _layer_spec(wout, i), _layer_spec(gpost, i)],
        out_specs=[pl.BlockSpec((1, tm, d), lambda b, s: (b, s, 0)),
                   pl.BlockSpec((1, CONV_W - 1, e), lambda b, s: (b, 0, 0))],
        out_shape=[jax.ShapeDtypeStruct(x.shape, F32),
                   jax.ShapeDtypeStruct((bsz, CONV_W - 1, e), F32)],
        scratch_shapes=[pltpu.VMEM((SUBLANES, e), F32),
                        pltpu.VMEM(win.shape[1:], BF16), pltpu.VMEM(wout.shape[1:], BF16)],
        compiler_params=_params(("arbitrary", "arbitrary")),
        name="conv_prompt",
    )(x, gpre, win, cw, wout, gpost)


def _conv_sample(i, j, xf, seq, state, gpre, win, cw, wout, gpost):
    n, d = xf.shape
    nb = n // seq
    e = wout.shape[1]
    assert CONV_W == 3 and seq & (seq - 1) == 0 and seq >= CONV_W - 1
    y, new_buf = pl.pallas_call(
        functools.partial(_conv_sample_kernel, seq),
        grid=(1,),
        in_specs=[_const_spec(xf.shape), _layer_spec(state, j),
                  _layer_spec(gpre, i), _layer_spec(win, i), _layer_spec(cw, j),
                  _layer_spec(wout, i), _layer_spec(gpost, i)],
        out_specs=[_const_spec((n, d)), _const_spec((nb, (CONV_W - 1) * e))],
        out_shape=[jax.ShapeDtypeStruct((n, d), F32),
                   jax.ShapeDtypeStruct((nb, (CONV_W - 1) * e), F32)],
        scratch_shapes=[pltpu.VMEM(win.shape[1:], BF16), pltpu.VMEM(wout.shape[1:], BF16),
                        pltpu.VMEM((e // LANES, n, LANES), F32),
                        pltpu.VMEM((e // LANES, n, LANES), F32)],
        compiler_params=_params(("arbitrary",)),
        name="conv_sample",
    )(xf, state, gpre, win, cw, wout, gpost)
    return y, new_buf.reshape(nb, CONV_W - 1, e)


def _hgrn_prompt(i, j, x, gpre, win, lb_logits, gon, wout, gpost):
    bsz, t, d = x.shape
    e = wout.shape[1]
    heads = e // HEAD_DIM
    tm = PROMPT_TILE
    assert t % tm == 0 and tm % HGRN_CHUNK == 0
    nt = t // tm
    n_tiles = bsz * nt
    n_bf16 = 5 + len(_level_sizes(HGRN_CHUNK))
    scratch = ([pltpu.VMEM(win.shape[1:], BF16), pltpu.VMEM(wout.shape[1:], BF16),
                pltpu.VMEM((heads, HEAD_DIM, HEAD_DIM), F32),
                pltpu.VMEM((tm // HGRN_CHUNK, SUBLANES, e), F32),
                pltpu.VMEM((tm, 4 * e), F32),
                pltpu.VMEM((tm, e), F32)]
               + [pltpu.VMEM((tm, e), BF16) for _ in range(n_bf16)])

    def projected(g):
        tile = jnp.minimum(g, n_tiles - 1)
        return tile // nt, tile % nt

    def finished(g):
        tile = jnp.maximum(g - 2, 0)
        return tile // nt, tile % nt

    return pl.pallas_call(
        functools.partial(_hgrn_prompt_kernel, j, nt),
        grid=(n_tiles + 2,),
        in_specs=[pl.BlockSpec((1, tm, d), lambda g: projected(g) + (0,)),
                  pl.BlockSpec((1, tm, d), lambda g: finished(g) + (0,)),
                  _layer_spec(gpre, i), _layer_spec(win, i), _const_spec(lb_logits.shape),
                  _layer_spec(gon, j), _layer_spec(wout, i), _layer_spec(gpost, i)],
        out_specs=[pl.BlockSpec((1, tm, d), lambda g: finished(g) + (0,)),
                   pl.BlockSpec((1, heads, HEAD_DIM, HEAD_DIM), lambda g: (finished(g)[0], 0, 0, 0))],
        out_shape=[jax.ShapeDtypeStruct(x.shape, F32),
                   jax.ShapeDtypeStruct((bsz, heads, HEAD_DIM, HEAD_DIM), F32)],
        scratch_shapes=scratch,
        compiler_params=_params(("arbitrary",)),
        name="hgrn_prompt",
    )(x, x, gpre, win, lb_logits, gon, wout, gpost)


def _hgrn_sample(i, j, xf, seq, state, states_out, gpre, win, lb_logits, gon, wout, gpost):
    n, d = xf.shape
    nb = n // seq
    e = wout.shape[1]
    heads = e // HEAD_DIM
    g = SAMPLE_SEQS_PER_STEP
    assert 2 * seq == SUBLANES and nb % g == 0
    assert g % (2 * SAMPLE_PAIRS_PER_ITER) == 0
    assert seq // 2 * -LOG_F_FLOOR * LOG2_E < F32_MAX_EXP2

    def head_cols(part):
        return pl.BlockSpec((None, d, HEAD_DIM), lambda hd, blk: (i, 0, part * heads + hd))

    st_spec = pl.BlockSpec((None, g, None, HEAD_DIM, HEAD_DIM), lambda hd, blk: (j, blk, hd, 0, 0))
    args = [xf, gpre, win, win, win, win, lb_logits, gon, wout, gpost, state]
    in_specs = [_const_spec(xf.shape), _layer_spec(gpre, i)] + [head_cols(part) for part in range(4)] + [
        pl.BlockSpec((lb_logits.shape[0], HEAD_DIM), lambda hd, blk: (0, hd)),
        _layer_spec(gon, j), _layer_spec(wout, i), _layer_spec(gpost, i), st_spec]
    aliases = {}
    if states_out is not None:
        args.append(states_out)
        in_specs.append(pl.BlockSpec(memory_space=pl.ANY))
        aliases = {len(args) - 1: 1}
    y, states_out = pl.pallas_call(
        functools.partial(_hgrn_sample_kernel, j, seq),
        grid=(heads, nb // g),
        in_specs=in_specs,
        out_specs=[_const_spec((n, d)), st_spec],
        out_shape=[jax.ShapeDtypeStruct((n, d), F32), jax.ShapeDtypeStruct(state.shape, F32)],
        scratch_shapes=[pltpu.VMEM((n, d), BF16)]
                       + [pltpu.VMEM((n, HEAD_DIM), F32) for _ in range(4)]
                       + [pltpu.VMEM((heads, n, HEAD_DIM), F32) for _ in range(2)],
        input_output_aliases=aliases,
        compiler_params=_params(("arbitrary", "arbitrary")),
        name="hgrn_sample",
    )(*args)
    return y, states_out


def kernel(x_prompt, x_sample, state_conv, state_hgrn, norm_pre, w_in, conv_w, hgrn_lb_logits,
           hgrn_onorm, w_out, norm_post):
    depth = w_in.shape[0]
    w_in32, w_out32, lb_logits = w_in, w_out, hgrn_lb_logits
    gpre, gpost, gon = norm_pre[:, None, :], norm_post[:, None, :], hgrn_onorm[:, None, :]
    conv_rows = state_conv.reshape(state_conv.shape[0], state_conv.shape[1], -1)
    nb, seq, d = x_sample.shape
    xp, xs = x_prompt, x_sample.reshape(nb * seq, d)
    conv_p, conv_s, hgrn_p, hgrn_s = [], [], [], None
    for i in range(depth):
        j = i // 2
        if i % 2 == 0:
            xp, bp = _conv_prompt(i, j, xp, gpre, w_in32, conv_w, w_out32, gpost)
            xs, bs = _conv_sample(i, j, xs, seq, conv_rows, gpre, w_in32, conv_w, w_out32, gpost)
            conv_p.append(bp)
            conv_s.append(bs)
        else:
            xp, sp = _hgrn_prompt(i, j, xp, gpre, w_in32, lb_logits, gon, w_out32, gpost)
            xs, hgrn_s = _hgrn_sample(i, j, xs, seq, state_hgrn, hgrn_s, gpre, w_in32, lb_logits, gon,
                                      w_out32, gpost)
            hgrn_p.append(sp)
    return (xp, xs.reshape(nb, seq, d), jnp.stack(conv_p), jnp.stack(conv_s), jnp.stack(hgrn_p), hgrn_s)
```

```python
import functools
import math

import jax
import jax.numpy as jnp
from jax import lax
from jax.experimental import pallas as pl
from jax.experimental.pallas import tpu as pltpu

F32 = jnp.float32
BF16 = jnp.bfloat16

RMS_EPS = 1e-6
LOG_F_FLOOR = -20.0
LOG2_E = math.log2(math.e)
HEAD_DIM = 128
LANES = 128
SUBLANES = 8
MXU_WIDTH = 256
F32_MAX_EXP2 = 127
CONV_W = 3
HGRN_CHUNK = 128
BASE_BLOCK = SUBLANES
assert BASE_BLOCK // 2 * -LOG_F_FLOOR * LOG2_E < F32_MAX_EXP2
SCORE_LOOKAHEAD = 2
PROMPT_TILE = 256
CONV_PROMPT_TILE = 512
CONV_ROW_SPLITS = 2
SAMPLE_SEQS_PER_STEP = 128
SAMPLE_PAIRS_PER_ITER = 8
VMEM_LIMIT_BYTES = 56 * 1024 * 1024


def _rms(x, g):
    ms = jnp.mean(x * x, axis=-1, keepdims=True)
    return x * lax.rsqrt(ms + RMS_EPS) * g


def _silu(x):
    return x * (1.0 / (1.0 + jnp.exp2(x * (-LOG2_E))))


def _dot(a, b):
    return jnp.dot(a, b, preferred_element_type=F32)


def _dot_nt(a, b):
    return lax.dot_general(a, b, (((1,), (1,)), ((), ())), preferred_element_type=F32)


def _dot_tn(a, b):
    return lax.dot_general(a, b, (((0,), (0,)), ((), ())), preferred_element_type=F32)


def _lower_bound(logits, j):
    m = jnp.max(logits, axis=0, keepdims=True)
    e = jnp.exp(logits - m)
    p = e / jnp.sum(e, axis=0, keepdims=True)
    if j == 0:
        return jnp.zeros_like(p[0:1])
    return jnp.sum(p[1:j + 1], axis=0, keepdims=True)


def _forget_gate(fpre, lb):
    e = jnp.exp2(jnp.abs(fpre) * (-LOG2_E))
    r = 1.0 / (1.0 + e)
    er = e * r
    pos = fpre >= 0
    sig = jnp.where(pos, r, er)
    nsig = jnp.where(pos, er, r)
    f = lb + (1.0 - lb) * sig
    log2_f = jnp.maximum(jnp.log(f) * LOG2_E, LOG_F_FLOOR * LOG2_E)
    k = jnp.minimum((1.0 - lb) * nsig, 1.0 - math.exp(LOG_F_FLOOR))
    return log2_f, k


def _cumsum_rows(x, group):
    pos = lax.broadcasted_iota(jnp.int32, x.shape, 0) & (group - 1)
    s = 1
    while s < group:
        x = x + jnp.where(pos >= s, pltpu.roll(x, s, axis=0), 0.0)
        s *= 2
    return x


def _group_allsum(x, group):
    n = x.shape[0]
    pos = lax.broadcasted_iota(jnp.int32, x.shape, 0) & (group - 1)
    s = 1
    while s < group:
        partner = jnp.where((pos & s) == 0, pltpu.roll(x, n - s, axis=0), pltpu.roll(x, s, axis=0))
        x = x + partner
        s *= 2
    return x


def _level_matrix(n):
    t = lax.broadcasted_iota(jnp.int32, (n, n), 0)
    s = lax.broadcasted_iota(jnp.int32, (n, n), 1)
    sh = BASE_BLOCK.bit_length() - 1
    lv = jnp.where(((t >> sh) == (s >> sh)) & (t >= s), 1, 0)
    c, i = 2 * BASE_BLOCK, 2
    while c <= n:
        half, sh = c // 2, sh + 1
        own = ((t >> sh) == (s >> sh)) & ((t & half) != 0) & ((s & half) == 0)
        lv = jnp.where(own, i, lv)
        c, i = 2 * c, i + 1
    return lv


def _level_sizes(chunk):
    sizes, c = [], 2 * BASE_BLOCK
    while c <= chunk:
        sizes.append(c)
        c *= 2
    return sizes


def _chunk_operands(q, k, lf2):
    c = q.shape[0]
    local = _cumsum_rows(lf2, BASE_BLOCK)
    b_parts, x_parts, carry = [], [], None
    for g in range(c // BASE_BLOCK):
        blk = local[g * BASE_BLOCK:(g + 1) * BASE_BLOCK]
        x_parts.append(blk - blk[BASE_BLOCK // 2 - 1:BASE_BLOCK // 2])
        if carry is not None:
            blk = blk + carry
        b_parts.append(blk)
        carry = blk[BASE_BLOCK - 1:BASE_BLOCK]
    b = jnp.concatenate(b_parts, axis=0)
    x = jnp.concatenate(x_parts, axis=0)
    b_last = carry
    qe = (q * jnp.exp2(b)).astype(BF16)
    kd = (k * jnp.exp2(b_last - b)).astype(BF16)
    q8 = (q * jnp.exp2(x)).astype(BF16)
    k8 = (k * jnp.exp2(-x)).astype(BF16)
    w = []
    for size in _level_sizes(c):
        half = size // 2
        parts = []
        for s0 in range(0, c, size):
            r = b[s0 + half - 1:s0 + half]
            lo, hi = slice(s0, s0 + half), slice(s0 + half, s0 + size)
            parts.append(k[lo] * jnp.exp2(r - b[lo]))
            parts.append(q[hi] * jnp.exp2(b[hi] - r))
        w.append(jnp.concatenate(parts, axis=0).astype(BF16))
    return qe, kd, q8, k8, w, jnp.exp2(b_last)


def _conv_project(h, win_ref):
    e = win_ref.shape[1] // 4
    return tuple(_dot(h, win_ref[:, part * e:(part + 1) * e]) for part in range(4))


def _conv_gate(proj, cw_ref, prev1, prev2, pos):
    v, bg, cg, z = proj
    u = cg * v
    u1 = jnp.where(pos >= 1, pltpu.roll(u, 1, axis=0), prev1)
    u2 = jnp.where(pos >= 2, pltpu.roll(u, 2, axis=0), prev2)
    conv = cw_ref[0:1, :] * u2 + cw_ref[1:2, :] * u1 + cw_ref[2:3, :] * u
    return bg * conv * _silu(z), u


def _conv_prompt_kernel(x_ref, gpre_ref, win_in_ref, cw_ref, wout_in_ref, gpost_ref,
                        y_ref, buf_ref, tail_ref, win_ref, wout_ref):
    @pl.when(jnp.logical_and(pl.program_id(0) == 0, pl.program_id(1) == 0))
    def _():
        win_ref[...] = win_in_ref[...].astype(BF16)
        wout_ref[...] = wout_in_ref[...].astype(BF16)

    @pl.when(pl.program_id(1) == 0)
    def _():
        tail_ref[...] = jnp.zeros_like(tail_ref)

    tm = x_ref.shape[1]
    rows = tm // CONV_ROW_SPLITS
    e = win_ref.shape[1] // 4
    pos = lax.broadcasted_iota(jnp.int32, (rows, e), 0)
    tail = tail_ref[...]
    xs = [x_ref[0, k * rows:(k + 1) * rows, :] for k in range(CONV_ROW_SPLITS)]
    proj = _conv_project(_rms(xs[0], gpre_ref[...]).astype(BF16), win_ref)
    for k in range(CONV_ROW_SPLITS):
        if k + 1 < CONV_ROW_SPLITS:
            next_proj = _conv_project(_rms(xs[k + 1], gpre_ref[...]).astype(BF16), win_ref)
        prev1 = tail[SUBLANES - 1:SUBLANES, :]
        prev2 = jnp.where(pos == 0, tail[SUBLANES - 2:SUBLANES - 1, :], prev1)
        g, u = _conv_gate(proj, cw_ref, prev1, prev2, pos)
        y = _dot(g.astype(BF16), wout_ref[...])
        y_ref[0, k * rows:(k + 1) * rows, :] = xs[k] + _rms(y, gpost_ref[...])
        tail = u[rows - SUBLANES:rows, :]
        if k + 1 < CONV_ROW_SPLITS:
            proj = next_proj
    tail_ref[...] = tail
    buf_ref[0] = tail[SUBLANES - (CONV_W - 1):SUBLANES, :]


def _conv_sample_kernel(seq, x_ref, buf_ref, gpre_ref, win_in_ref, cw_ref, wout_in_ref, gpost_ref,
                        y_ref, new_buf_ref, win_ref, wout_ref, p1_s, p2_s):
    win_ref[...] = win_in_ref[...].astype(BF16)
    wout_ref[...] = wout_in_ref[...].astype(BF16)
    x = x_ref[...]
    nb = buf_ref.shape[0]
    lane_tiles, n, lanes = p1_s.shape
    e = lane_tiles * lanes
    h = _rms(x, gpre_ref[...]).astype(BF16)
    pos = lax.broadcasted_iota(jnp.int32, (n, e), 0) & (seq - 1)
    p1_s[...] = jnp.zeros_like(p1_s)
    p2_s[...] = jnp.zeros_like(p2_s)
    for c in range(lane_tiles):
        buf0 = buf_ref[:, c * lanes:(c + 1) * lanes]
        buf1 = buf_ref[:, e + c * lanes:e + (c + 1) * lanes]
        p1_s[c, pl.ds(0, nb, stride=seq), :] = buf1
        p2_s[c, pl.ds(0, nb, stride=seq), :] = buf0
        p2_s[c, pl.ds(1, nb, stride=seq), :] = buf1
    prev1 = jnp.concatenate([p1_s[c] for c in range(lane_tiles)], axis=1)
    prev2 = jnp.concatenate([p2_s[c] for c in range(lane_tiles)], axis=1)
    g, u = _conv_gate(_conv_project(h, win_ref), cw_ref, prev1, prev2, pos)
    y = _dot(g.astype(BF16), wout_ref[...])
    y_ref[...] = x + _rms(y, gpost_ref[...])
    for c in range(lane_tiles):
        p1_s[c] = u[:, c * lanes:(c + 1) * lanes]
        for r in range(CONV_W - 1):
            new_buf_ref[:, r * e + c * lanes:r * e + (c + 1) * lanes] = (
                p1_s[c, pl.ds(seq - (CONV_W - 1) + r, nb, stride=seq), :])


def _hgrn_prompt_kernel(j, nt, xp_ref, xr_ref, gpre_ref, win_in_ref, lbl_ref, gon_ref, wout_in_ref, gpost_ref,
                        y_ref, sout_ref, win_ref, wout_ref, st_s, dec_s, raw_s, gate_s,
                        qe_s, kd_s, q8_s, k8_s, v_s, *w_s):
    g = pl.program_id(0)
    e = win_ref.shape[1] // 4
    heads = e // HEAD_DIM
    tm = xp_ref.shape[1]
    fin = g - 2

    @pl.when(g == 0)
    def _():
        win_ref[...] = win_in_ref[...].astype(BF16)
        wout_ref[...] = wout_in_ref[...].astype(BF16)
        for ref in (dec_s, raw_s, gate_s, qe_s, kd_s, q8_s, k8_s, v_s) + tuple(w_s):
            ref[...] = jnp.zeros_like(ref)

    @pl.when(jnp.logical_or(g < 2, lax.rem(fin, nt) == 0))
    def _():
        st_s[...] = jnp.zeros_like(st_s)

    lv = _level_matrix(HGRN_CHUNK)
    owned = [lv == i + 1 for i in range(len(w_s) + 1)]
    gon = gon_ref[...]
    chunks = tm // HGRN_CHUNK

    def chunk_scores(c, hd):
        rows = slice(c * HGRN_CHUNK, (c + 1) * HGRN_CHUNK)
        cols = slice(hd * HEAD_DIM, (hd + 1) * HEAD_DIM)
        a = jnp.where(owned[0], _dot_nt(q8_s[rows, cols], k8_s[rows, cols]), 0.0)
        for i, w_ref in enumerate(w_s):
            wv = w_ref[rows, cols]
            a = jnp.where(owned[i + 1], _dot_nt(wv, wv), a)
        return a.astype(BF16)

    def chunk_output(c, hd, a):
        rows = slice(c * HGRN_CHUNK, (c + 1) * HGRN_CHUNK)
        cols = slice(hd * HEAD_DIM, (hd + 1) * HEAD_DIM)
        vb = v_s[rows, cols]
        st = st_s[hd]
        o = _dot(a, vb) + _dot_nt(qe_s[rows, cols], st.astype(BF16))
        st3 = st.reshape(HEAD_DIM // SUBLANES, SUBLANES, HEAD_DIM) * dec_s[c, :, cols]
        st_s[hd] = st3.reshape(HEAD_DIM, HEAD_DIM) + _dot_tn(vb, kd_s[rows, cols])
        return (_rms(o, gon) * gate_s[rows, cols]).astype(BF16)

    h = _rms(xp_ref[0], gpre_ref[...]).astype(BF16)
    lb = _lower_bound(lbl_ref[...], j)
    out_blocks = [[None] * heads for _ in range(chunks)]
    groups = e // MXU_WIDTH

    def part_cols(cg):
        return [slice(part * e + cg * MXU_WIDTH, part * e + (cg + 1) * MXU_WIDTH) for part in range(4)]


    def stage(cg, c, q_raw, f_raw, v_raw, z_raw):
        gc = slice(cg * MXU_WIDTH, (cg + 1) * MXU_WIDTH)
        rows = slice(c * HGRN_CHUNK, (c + 1) * HGRN_CHUNK)
        q = _silu(q_raw) * (HEAD_DIM ** -0.5)
        lf2, k = _forget_gate(f_raw, lb[:, gc])
        v_s[rows, gc] = v_raw.astype(BF16)
        gate_s[rows, gc] = _silu(z_raw)
        qe, kd, q8, k8, w, decay = _chunk_operands(q, k, lf2)
        qe_s[rows, gc] = qe
        kd_s[rows, gc] = kd
        q8_s[rows, gc] = q8
        k8_s[rows, gc] = k8
        for w_ref, w_val in zip(w_s, w):
            w_ref[rows, gc] = w_val
        dec_s[c, :, gc] = jnp.broadcast_to(decay, (SUBLANES, MXU_WIDTH))

    def project(cols):
        raw_s[:, cols] = _dot(h, win_ref[:, cols])

    def output():
        gated = jnp.concatenate([jnp.concatenate(blocks, axis=1) for blocks in out_blocks], axis=0)
        y = _dot(gated, wout_ref[...])
        y_ref[0] = xr_ref[0] + _rms(y, gpost_ref[...])

    heads_per_group = MXU_WIDTH // HEAD_DIM
    raws = {}
    for cg in range(groups):
        for c in range(chunks):
            rows = slice(c * HGRN_CHUNK, (c + 1) * HGRN_CHUNK)
            raws[cg, c] = [raw_s[rows, cols] for cols in part_cols(cg)]
    projections = [cols for cg in range(groups) for cols in part_cols(cg)]
    units = [(cg, c, hd) for cg in range(groups) for c in range(chunks)
             for hd in range(cg * heads_per_group, (cg + 1) * heads_per_group)]
    scores = {}
    for n in range(len(units) + SCORE_LOOKAHEAD):
        if n < len(units):
            _, c, hd = units[n]
            scores[c, hd] = chunk_scores(c, hd)
        if projections:
            project(projections.pop(0))
        if n >= SCORE_LOOKAHEAD:
            done = n - SCORE_LOOKAHEAD
            cg, c, hd = units[done]
            out_blocks[c][hd] = chunk_output(c, hd, scores.pop((c, hd)))
            if (done + 1) % heads_per_group == 0:
                stage(cg, c, *raws.pop((cg, c)))
    for cols in projections:
        project(cols)
    output()

    @pl.when(jnp.logical_and(g >= 2, lax.rem(fin, nt) == nt - 1))
    def _():
        for hd in range(heads):
            sout_ref[0, hd] = st_s[hd].T


def _hgrn_sample_kernel(j, seq, x_ref, gpre_ref, wq_ref, wf_ref, wv_ref, wz_ref, lbl_ref, gon_ref,
                        wout_ref, gpost_ref, s0_ref, *rest):
    y_ref, s1_ref, h_s, qe_s, kd_s, decp_s, v_s, o_s, gate_s = rest[-9:]
    hd, blk = pl.program_id(0), pl.program_id(1)
    n = x_ref.shape[0]
    nseq = s0_ref.shape[0]
    heads = o_s.shape[0]

    @pl.when(jnp.logical_and(hd == 0, blk == 0))
    def _():
        h_s[...] = _rms(x_ref[...], gpre_ref[...]).astype(BF16)

    @pl.when(blk == 0)
    def _():
        h = h_s[...]
        lb = _lower_bound(lbl_ref[...], j)
        q = _silu(_dot(h, wq_ref[...].astype(BF16))) * (HEAD_DIM ** -0.5)
        lf2, k = _forget_gate(_dot(h, wf_ref[...].astype(BF16)), lb)
        v = _dot(h, wv_ref[...].astype(BF16))
        gate_s[hd] = _silu(_dot(h, wz_ref[...].astype(BF16)))
        t = lax.broadcasted_iota(jnp.int32, (n, n), 0)
        s = lax.broadcasted_iota(jnp.int32, (n, n), 1)
        sh = seq.bit_length() - 1
        same_seq_causal = ((t >> sh) == (s >> sh)) & (t >= s)
        row = lax.broadcasted_iota(jnp.int32, (n, HEAD_DIM), 0)
        pos = row & (seq - 1)
        b = _cumsum_rows(lf2, seq)
        b_last = _group_allsum(lf2, seq)
        x_mid = b - _group_allsum(jnp.where(pos < seq // 2, lf2, 0.0), seq)
        a = _dot_nt((q * jnp.exp2(x_mid)).astype(BF16), (k * jnp.exp2(-x_mid)).astype(BF16))
        a = jnp.where(same_seq_causal, a, 0.0)
        o_s[hd] = _dot(a.astype(BF16), v.astype(BF16))
        qe_s[...] = q * jnp.exp2(b)
        kd_s[...] = k * jnp.exp2(b_last - b)
        v_s[...] = v
        dec = jnp.exp2(b_last)
        dsw = jnp.where((row & (SUBLANES - 1)) < seq,
                        pltpu.roll(dec, n - seq, axis=0), pltpu.roll(dec, seq, axis=0))
        hi = dsw.astype(BF16).astype(F32)
        mid = (dsw - hi).astype(BF16).astype(F32)
        lo = (dsw - hi - mid).astype(BF16).astype(F32)
        decp_s[...] = jnp.where(pos == 0, hi, jnp.where(pos == 1, mid, jnp.where(pos == 2, lo, 0.0)))

    row = lax.broadcasted_iota(jnp.int32, (SUBLANES, HEAD_DIM), 0)
    pad = jnp.zeros((HEAD_DIM - SUBLANES, HEAD_DIM), F32)
    pad2 = jnp.zeros((HEAD_DIM - SUBLANES, 2 * HEAD_DIM), F32)
    base = blk * (nseq * seq)

    def pairs_body(it, carry):
        for u in range(SAMPLE_PAIRS_PER_ITER):
            p = it * SAMPLE_PAIRS_PER_ITER + u
            rows = pl.ds(pl.multiple_of(base + p * SUBLANES, SUBLANES), SUBLANES)
            qe8 = qe_s[rows, :].astype(BF16)
            kd8 = kd_s[rows, :]
            dp8 = decp_s[rows, :]
            v8 = v_s[rows, :]
            o8 = o_s[hd, rows, :]
            for half in range(2):
                mine = (row >= seq) if half else (row < seq)
                s0 = s0_ref[2 * p + half]
                o8 = o8 + jnp.where(mine, _dot(qe8, s0.astype(BF16)), 0.0)
                lhs = jnp.concatenate([jnp.where(mine, kd8, dp8), pad], axis=0).astype(BF16)
                ones = jnp.where(mine, 0.0, 1.0)
                rhs = jnp.concatenate(
                    [jnp.concatenate([jnp.where(mine, v8, 0.0), ones], axis=1), pad2], axis=0).astype(BF16)
                upd = _dot_tn(lhs, rhs)
                s1_ref[2 * p + half] = upd[:, HEAD_DIM:] * s0 + upd[:, :HEAD_DIM]
            o_s[hd, rows, :] = o8
        return carry

    lax.fori_loop(0, nseq // (2 * SAMPLE_PAIRS_PER_ITER), pairs_body, 0)

    @pl.when(jnp.logical_and(hd == heads - 1, blk == pl.num_programs(1) - 1))
    def _():
        gon = gon_ref[...]
        gated = jnp.concatenate(
            [(_rms(o_s[i], gon) * gate_s[i]).astype(BF16) for i in range(heads)], axis=1)
        y = _dot(gated, wout_ref[...].astype(BF16))
        y_ref[...] = x_ref[...] + _rms(y, gpost_ref[...])


def _const_spec(shape):
    nd = len(shape)
    return pl.BlockSpec(shape, lambda *_: (0,) * nd)


def _layer_spec(arr, layer):
    nd = arr.ndim - 1
    return pl.BlockSpec((None,) + arr.shape[1:], lambda *_: (layer,) + (0,) * nd)


def _params(sem):
    return pltpu.CompilerParams(dimension_semantics=sem, vmem_limit_bytes=VMEM_LIMIT_BYTES)


def _conv_prompt(i, j, x, gpre, win, cw, wout, gpost):
    bsz, t, d = x.shape
    e = wout.shape[1]
    tm = CONV_PROMPT_TILE
    assert t % tm == 0 and tm % SUBLANES == 0
    return pl.pallas_call(
        _conv_prompt_kernel,
        grid=(bsz, t // tm),
        in_specs=[pl.BlockSpec((1, tm, d), lambda b, s: (b, s, 0)),
                  _layer_spec(gpre, i), _layer_spec(win, i), _layer_spec(cw, j),
                  _layer_spec(wout, i), _layer_spec(gpost, i)],
        out_specs=[pl.BlockSpec((1, tm, d), lambda b, s: (b, s, 0)),
                   pl.BlockSpec((1, CONV_W - 1, e), lambda b, s: (b, 0, 0))],
        out_shape=[jax.ShapeDtypeStruct(x.shape, F32),
                   jax.ShapeDtypeStruct((bsz, CONV_W - 1, e), F32)],
        scratch_shapes=[pltpu.VMEM((SUBLANES, e), F32),
                        pltpu.VMEM(win.shape[1:], BF16), pltpu.VMEM(wout.shape[1:], BF16)],
        compiler_params=_params(("arbitrary", "arbitrary")),
        name="conv_prompt",
    )(x, gpre, win, cw, wout, gpost)


def _conv_sample(i, j, xf, seq, state, gpre, win, cw, wout, gpost):
    n, d = xf.shape
    nb = n // seq
    e = wout.shape[1]
    assert CONV_W == 3 and seq & (seq - 1) == 0 and seq >= CONV_W - 1
    y, new_buf = pl.pallas_call(
        functools.partial(_conv_sample_kernel, seq),
        grid=(1,),
        in_specs=[_const_spec(xf.shape), _layer_spec(state, j),
                  _layer_spec(gpre, i), _layer_spec(win, i), _layer_spec(cw, j),
                  _layer_spec(wout, i), _layer_spec(gpost, i)],
        out_specs=[_const_spec((n, d)), _const_spec((nb, (CONV_W - 1) * e))],
        out_shape=[jax.ShapeDtypeStruct((n, d), F32),
                   jax.ShapeDtypeStruct((nb, (CONV_W - 1) * e), F32)],
        scratch_shapes=[pltpu.VMEM(win.shape[1:], BF16), pltpu.VMEM(wout.shape[1:], BF16),
                        pltpu.VMEM((e // LANES, n, LANES), F32),
                        pltpu.VMEM((e // LANES, n, LANES), F32)],
        compiler_params=_params(("arbitrary",)),
        name="conv_sample",
    )(xf, state, gpre, win, cw, wout, gpost)
    return y, new_buf.reshape(nb, CONV_W - 1, e)


def _hgrn_prompt(i, j, x, gpre, win, lb_logits, gon, wout, gpost):
    bsz, t, d = x.shape
    e = wout.shape[1]
    heads = e // HEAD_DIM
    tm = PROMPT_TILE
    assert t % tm == 0 and tm % HGRN_CHUNK == 0
    nt = t // tm
    n_tiles = bsz * nt
    n_bf16 = 5 + len(_level_sizes(HGRN_CHUNK))
    scratch = ([pltpu.VMEM(win.shape[1:], BF16), pltpu.VMEM(wout.shape[1:], BF16),
                pltpu.VMEM((heads, HEAD_DIM, HEAD_DIM), F32),
                pltpu.VMEM((tm // HGRN_CHUNK, SUBLANES, e), F32),
                pltpu.VMEM((tm, 4 * e), F32),
                pltpu.VMEM((tm, e), F32)]
               + [pltpu.VMEM((tm, e), BF16) for _ in range(n_bf16)])

    def projected(g):
        tile = jnp.minimum(g, n_tiles - 1)
        return tile // nt, tile % nt

    def finished(g):
        tile = jnp.maximum(g - 2, 0)
        return tile // nt, tile % nt

    return pl.pallas_call(
        functools.partial(_hgrn_prompt_kernel, j, nt),
        grid=(n_tiles + 2,),
        in_specs=[pl.BlockSpec((1, tm, d), lambda g: projected(g) + (0,)),
                  pl.BlockSpec((1, tm, d), lambda g: finished(g) + (0,)),
                  _layer_spec(gpre, i), _layer_spec(win, i), _const_spec(lb_logits.shape),
                  _layer_spec(gon, j), _layer_spec(wout, i), _layer_spec(gpost, i)],
        out_specs=[pl.BlockSpec((1, tm, d), lambda g: finished(g) + (0,)),
                   pl.BlockSpec((1, heads, HEAD_DIM, HEAD_DIM), lambda g: (finished(g)[0], 0, 0, 0))],
        out_shape=[jax.ShapeDtypeStruct(x.shape, F32),
                   jax.ShapeDtypeStruct((bsz, heads, HEAD_DIM, HEAD_DIM), F32)],
        scratch_shapes=scratch,
        compiler_params=_params(("arbitrary",)),
        name="hgrn_prompt",
    )(x, x, gpre, win, lb_logits, gon, wout, gpost)


def _hgrn_sample(i, j, xf, seq, state, states_out, gpre, win, lb_logits, gon, wout, gpost):
    n, d = xf.shape
    nb = n // seq
    e = wout.shape[1]
    heads = e // HEAD_DIM
    g = SAMPLE_SEQS_PER_STEP
    assert 2 * seq == SUBLANES and nb % g == 0
    assert g % (2 * SAMPLE_PAIRS_PER_ITER) == 0
    assert seq // 2 * -LOG_F_FLOOR * LOG2_E < F32_MAX_EXP2

    def head_cols(part):
        return pl.BlockSpec((None, d, HEAD_DIM), lambda hd, blk: (i, 0, part * heads + hd))

    st_spec = pl.BlockSpec((None, g, None, HEAD_DIM, HEAD_DIM), lambda hd, blk: (j, blk, hd, 0, 0))
    args = [xf, gpre, win, win, win, win, lb_logits, gon, wout, gpost, state]
    in_specs = [_const_spec(xf.shape), _layer_spec(gpre, i)] + [head_cols(part) for part in range(4)] + [
        pl.BlockSpec((lb_logits.shape[0], HEAD_DIM), lambda hd, blk: (0, hd)),
        _layer_spec(gon, j), _layer_spec(wout, i), _layer_spec(gpost, i), st_spec]
    aliases = {}
    if states_out is not None:
        args.append(states_out)
        in_specs.append(pl.BlockSpec(memory_space=pl.ANY))
        aliases = {len(args) - 1: 1}
    y, states_out = pl.pallas_call(
        functools.partial(_hgrn_sample_kernel, j, seq),
        grid=(heads, nb // g),
        in_specs=in_specs,
        out_specs=[_const_spec((n, d)), st_spec],
        out_shape=[jax.ShapeDtypeStruct((n, d), F32), jax.ShapeDtypeStruct(state.shape, F32)],
        scratch_shapes=[pltpu.VMEM((n, d), BF16)]
                       + [pltpu.VMEM((n, HEAD_DIM), F32) for _ in range(4)]
                       + [pltpu.VMEM((heads, n, HEAD_DIM), F32) for _ in range(2)],
        input_output_aliases=aliases,
        compiler_params=_params(("arbitrary", "arbitrary")),
        name="hgrn_sample",
    )(*args)
    return y, states_out


def kernel(x_prompt, x_sample, state_conv, state_hgrn, norm_pre, w_in, conv_w, hgrn_lb_logits,
           hgrn_onorm, w_out, norm_post):
    depth = w_in.shape[0]
    w_in32, w_out32, lb_logits = w_in, w_out, hgrn_lb_logits
    gpre, gpost, gon = norm_pre[:, None, :], norm_post[:, None, :], hgrn_onorm[:, None, :]
    conv_rows = state_conv.reshape(state_conv.shape[0], state_conv.shape[1], -1)
    nb, seq, d = x_sample.shape
    xp, xs = x_prompt, x_sample.reshape(nb * seq, d)
    conv_p, conv_s, hgrn_p, hgrn_s = [], [], [], None
    for i in range(depth):
        j = i // 2
        if i % 2 == 0:
            xp, bp = _conv_prompt(i, j, xp, gpre, w_in32, conv_w, w_out32, gpost)
            xs, bs = _conv_sample(i, j, xs, seq, conv_rows, gpre, w_in32, conv_w, w_out32, gpost)
            conv_p.append(bp)
            conv_s.append(bs)
        else:
            xp, sp = _hgrn_prompt(i, j, xp, gpre, w_in32, lb_logits, gon, w_out32, gpost)
            xs, hgrn_s = _hgrn_sample(i, j, xs, seq, state_hgrn, hgrn_s, gpre, w_in32, lb_logits, gon,
                                      w_out32, gpost)
            hgrn_p.append(sp)
    return (xp, xs.reshape(nb, seq, d), jnp.stack(conv_p), jnp.stack(conv_s), jnp.stack(hgrn_p), hgrn_s)
```

```python
import functools
import math

import jax
import jax.numpy as jnp
from jax import lax
from jax.experimental import pallas as pl
from jax.experimental.pallas import tpu as pltpu

F32 = jnp.float32
BF16 = jnp.bfloat16

RMS_EPS = 1e-6
LOG_F_FLOOR = -20.0
LOG2_E = math.log2(math.e)
HEAD_DIM = 128
LANES = 128
SUBLANES = 8
MXU_WIDTH = 256
F32_MAX_EXP2 = 127
CONV_W = 3
HGRN_CHUNK = 128
BASE_BLOCK = SUBLANES
assert BASE_BLOCK // 2 * -LOG_F_FLOOR * LOG2_E < F32_MAX_EXP2
SCORE_LOOKAHEAD = 2
PROMPT_TILE = 256
CONV_PROMPT_TILE = 512
CONV_ROW_SPLITS = 2
SAMPLE_SEQS_PER_STEP = 128
SAMPLE_PAIRS_PER_ITER = 8
VMEM_LIMIT_BYTES = 56 * 1024 * 1024


def _rms(x, g):
    ms = jnp.mean(x * x, axis=-1, keepdims=True)
    return x * lax.rsqrt(ms + RMS_EPS) * g


def _silu(x):
    return x * (1.0 / (1.0 + jnp.exp2(x * (-LOG2_E))))


def _dot(a, b):
    return jnp.dot(a, b, preferred_element_type=F32)


def _dot_nt(a, b):
    return lax.dot_general(a, b, (((1,), (1,)), ((), ())), preferred_element_type=F32)


def _dot_tn(a, b):
    return lax.dot_general(a, b, (((0,), (0,)), ((), ())), preferred_element_type=F32)


def _lower_bound(logits, j):
    m = jnp.max(logits, axis=0, keepdims=True)
    e = jnp.exp(logits - m)
    p = e / jnp.sum(e, axis=0, keepdims=True)
    if j == 0:
        return jnp.zeros_like(p[0:1])
    return jnp.sum(p[1:j + 1], axis=0, keepdims=True)


def _forget_gate(fpre, lb):
    e = jnp.exp2(jnp.abs(fpre) * (-LOG2_E))
    r = 1.0 / (1.0 + e)
    er = e * r
    pos = fpre >= 0
    sig = jnp.where(pos, r, er)
    nsig = jnp.where(pos, er, r)
    f = lb + (1.0 - lb) * sig
    log2_f = jnp.maximum(jnp.log(f) * LOG2_E, LOG_F_FLOOR * LOG2_E)
    k = jnp.minimum((1.0 - lb) * nsig, 1.0 - math.exp(LOG_F_FLOOR))
    return log2_f, k


def _cumsum_rows(x, group):
    pos = lax.broadcasted_iota(jnp.int32, x.shape, 0) & (group - 1)
    s = 1
    while s < group:
        x = x + jnp.where(pos >= s, pltpu.roll(x, s, axis=0), 0.0)
        s *= 2
    return x


def _group_allsum(x, group):
    n = x.shape[0]
    pos = lax.broadcasted_iota(jnp.int32, x.shape, 0) & (group - 1)
    s = 1
    while s < group:
        partner = jnp.where((pos & s) == 0, pltpu.roll(x, n - s, axis=0), pltpu.roll(x, s, axis=0))
        x = x + partner
        s *= 2
    return x


def _level_matrix(n):
    t = lax.broadcasted_iota(jnp.int32, (n, n), 0)
    s = lax.broadcasted_iota(jnp.int32, (n, n), 1)
    sh = BASE_BLOCK.bit_length() - 1
    lv = jnp.where(((t >> sh) == (s >> sh)) & (t >= s), 1, 0)
    c, i = 2 * BASE_BLOCK, 2
    while c <= n:
        half, sh = c // 2, sh + 1
        own = ((t >> sh) == (s >> sh)) & ((t & half) != 0) & ((s & half) == 0)
        lv = jnp.where(own, i, lv)
        c, i = 2 * c, i + 1
    return lv


def _level_sizes(chunk):
    sizes, c = [], 2 * BASE_BLOCK
    while c <= chunk:
        sizes.append(c)
        c *= 2
    return sizes


def _chunk_operands(q, k, lf2):
    c = q.shape[0]
    local = _cumsum_rows(lf2, BASE_BLOCK)
    b_parts, x_parts, carry = [], [], None
    for g in range(c // BASE_BLOCK):
        blk = local[g * BASE_BLOCK:(g + 1) * BASE_BLOCK]
        x_parts.append(blk - blk[BASE_BLOCK // 2 - 1:BASE_BLOCK // 2])
        if carry is not None:
            blk = blk + carry
        b_parts.append(blk)
        carry = blk[BASE_BLOCK - 1:BASE_BLOCK]
    b = jnp.concatenate(b_parts, axis=0)
    x = jnp.concatenate(x_parts, axis=0)
    b_last = carry
    qe = (q * jnp.exp2(b)).astype(BF16)
    kd = (k * jnp.exp2(b_last - b)).astype(BF16)
    q8 = (q * jnp.exp2(x)).astype(BF16)
    k8 = (k * jnp.exp2(-x)).astype(BF16)
    w = []
    for size in _level_sizes(c):
        half = size // 2
        parts = []
        for s0 in range(0, c, size):
            r = b[s0 + half - 1:s0 + half]
            lo, hi = slice(s0, s0 + half), slice(s0 + half, s0 + size)
            parts.append(k[lo] * jnp.exp2(r - b[lo]))
            parts.append(q[hi] * jnp.exp2(b[hi] - r))
        w.append(jnp.concatenate(parts, axis=0).astype(BF16))
    return qe, kd, q8, k8, w, jnp.exp2(b_last)


def _conv_project(h, win_ref):
    e = win_ref.shape[1] // 4
    return tuple(_dot(h, win_ref[:, part * e:(part + 1) * e]) for part in range(4))


def _conv_gate(proj, cw_ref, prev1, prev2, pos):
    v, bg, cg, z = proj
    u = cg * v
    u1 = jnp.where(pos >= 1, pltpu.roll(u, 1, axis=0), prev1)
    u2 = jnp.where(pos >= 2, pltpu.roll(u, 2, axis=0), prev2)
    conv = cw_ref[0:1, :] * u2 + cw_ref[1:2, :] * u1 + cw_ref[2:3, :] * u
    return bg * conv * _silu(z), u


def _conv_prompt_kernel(x_ref, gpre_ref, win_in_ref, cw_ref, wout_in_ref, gpost_ref,
                        y_ref, buf_ref, tail_ref, win_ref, wout_ref):
    @pl.when(jnp.logical_and(pl.program_id(0) == 0, pl.program_id(1) == 0))
    def _():
        win_ref[...] = win_in_ref[...].astype(BF16)
        wout_ref[...] = wout_in_ref[...].astype(BF16)

    @pl.when(pl.program_id(1) == 0)
    def _():
        tail_ref[...] = jnp.zeros_like(tail_ref)

    tm = x_ref.shape[1]
    rows = tm // CONV_ROW_SPLITS
    e = win_ref.shape[1] // 4
    pos = lax.broadcasted_iota(jnp.int32, (rows, e), 0)
    tail = tail_ref[...]
    xs = [x_ref[0, k * rows:(k + 1) * rows, :] for k in range(CONV_ROW_SPLITS)]
    proj = _conv_project(_rms(xs[0], gpre_ref[...]).astype(BF16), win_ref)
    for k in range(CONV_ROW_SPLITS):
        if k + 1 < CONV_ROW_SPLITS:
            next_proj = _conv_project(_rms(xs[k + 1], gpre_ref[...]).astype(BF16), win_ref)
        prev1 = tail[SUBLANES - 1:SUBLANES, :]
        prev2 = jnp.where(pos == 0, tail[SUBLANES - 2:SUBLANES - 1, :], prev1)
        g, u = _conv_gate(proj, cw_ref, prev1, prev2, pos)
        y = _dot(g.astype(BF16), wout_ref[...])
        y_ref[0, k * rows:(k + 1) * rows, :] = xs[k] + _rms(y, gpost_ref[...])
        tail = u[rows - SUBLANES:rows, :]
        if k + 1 < CONV_ROW_SPLITS:
            proj = next_proj
    tail_ref[...] = tail
    buf_ref[0] = tail[SUBLANES - (CONV_W - 1):SUBLANES, :]


def _conv_sample_kernel(seq, x_ref, buf_ref, gpre_ref, win_in_ref, cw_ref, wout_in_ref, gpost_ref,
                        y_ref, new_buf_ref, win_ref, wout_ref, p1_s, p2_s):
    win_ref[...] = win_in_ref[...].astype(BF16)
    wout_ref[...] = wout_in_ref[...].astype(BF16)
    x = x_ref[...]
    nb = buf_ref.shape[0]
    lane_tiles, n, lanes = p1_s.shape
    e = lane_tiles * lanes
    h = _rms(x, gpre_ref[...]).astype(BF16)
    pos = lax.broadcasted_iota(jnp.int32, (n, e), 0) & (seq - 1)
    p1_s[...] = jnp.zeros_like(p1_s)
    p2_s[...] = jnp.zeros_like(p2_s)
    for c in range(lane_tiles):
        buf0 = buf_ref[:, c * lanes:(c + 1) * lanes]
        buf1 = buf_ref[:, e + c * lanes:e + (c + 1) * lanes]
        p1_s[c, pl.ds(0, nb, stride=seq), :] = buf1
        p2_s[c, pl.ds(0, nb, stride=seq), :] = buf0
        p2_s[c, pl.ds(1, nb, stride=seq), :] = buf1
    prev1 = jnp.concatenate([p1_s[c] for c in range(lane_tiles)], axis=1)
    prev2 = jnp.concatenate([p2_s[c] for c in range(lane_tiles)], axis=1)
    g, u = _conv_gate(_conv_project(h, win_ref), cw_ref, prev1, prev2, pos)
    y = _dot(g.astype(BF16), wout_ref[...])
    y_ref[...] = x + _rms(y, gpost_ref[...])
    for c in range(lane_tiles):
        p1_s[c] = u[:, c * lanes:(c + 1) * lanes]
        for r in range(CONV_W - 1):
            new_buf_ref[:, r * e + c * lanes:r * e + (c + 1) * lanes] = (
                p1_s[c, pl.ds(seq - (CONV_W - 1) + r, nb, stride=seq), :])


def _hgrn_prompt_kernel(j, nt, xp_ref, xr_ref, gpre_ref, win_in_ref, lbl_ref, gon_ref, wout_in_ref, gpost_ref,
                        y_ref, sout_ref, win_ref, wout_ref, st_s, dec_s, raw_s, gate_s,
                        qe_s, kd_s, q8_s, k8_s, v_s, *w_s):
    g = pl.program_id(0)
    e = win_ref.shape[1] // 4
    heads = e // HEAD_DIM
    tm = xp_ref.shape[1]
    fin = g - 2

    @pl.when(g == 0)
    def _():
        win_ref[...] = win_in_ref[...].astype(BF16)
        wout_ref[...] = wout_in_ref[...].astype(BF16)
        for ref in (dec_s, raw_s, gate_s, qe_s, kd_s, q8_s, k8_s, v_s) + tuple(w_s):
            ref[...] = jnp.zeros_like(ref)

    @pl.when(jnp.logical_or(g < 2, lax.rem(fin, nt) == 0))
    def _():
        st_s[...] = jnp.zeros_like(st_s)

    lv = _level_matrix(HGRN_CHUNK)
    owned = [lv == i + 1 for i in range(len(w_s) + 1)]
    gon = gon_ref[...]
    chunks = tm // HGRN_CHUNK

    def chunk_scores(c, hd):
        rows = slice(c * HGRN_CHUNK, (c + 1) * HGRN_CHUNK)
        cols = slice(hd * HEAD_DIM, (hd + 1) * HEAD_DIM)
        a = jnp.where(owned[0], _dot_nt(q8_s[rows, cols], k8_s[rows, cols]), 0.0)
        for i, w_ref in enumerate(w_s):
            wv = w_ref[rows, cols]
            a = jnp.where(owned[i + 1], _dot_nt(wv, wv), a)
        return a.astype(BF16)

    def chunk_output(c, hd, a):
        rows = slice(c * HGRN_CHUNK, (c + 1) * HGRN_CHUNK)
        cols = slice(hd * HEAD_DIM, (hd + 1) * HEAD_DIM)
        vb = v_s[rows, cols]
        st = st_s[hd]
        o = _dot(a, vb) + _dot_nt(qe_s[rows, cols], st.astype(BF16))
        st3 = st.reshape(HEAD_DIM // SUBLANES, SUBLANES, HEAD_DIM) * dec_s[c, :, cols]
        st_s[hd] = st3.reshape(HEAD_DIM, HEAD_DIM) + _dot_tn(vb, kd_s[rows, cols])
        return (_rms(o, gon) * gate_s[rows, cols]).astype(BF16)

    groups = e // MXU_WIDTH
    heads_per_group = MXU_WIDTH // HEAD_DIM

    def part_cols(cg):
        return [slice(part * e + cg * MXU_WIDTH, part * e + (cg + 1) * MXU_WIDTH) for part in range(4)]

    def stage(lb, cg, c, q_raw, f_raw, v_raw, z_raw):
        gc = slice(cg * MXU_WIDTH, (cg + 1) * MXU_WIDTH)
        rows = slice(c * HGRN_CHUNK, (c + 1) * HGRN_CHUNK)
        q = _silu(q_raw) * (HEAD_DIM ** -0.5)
        lf2, k = _forget_gate(f_raw, lb[:, gc])
        v_s[rows, gc] = v_raw.astype(BF16)
        gate_s[rows, gc] = _silu(z_raw)
        qe, kd, q8, k8, w, decay = _chunk_operands(q, k, lf2)
        qe_s[rows, gc] = qe
        kd_s[rows, gc] = kd
        q8_s[rows, gc] = q8
        k8_s[rows, gc] = k8
        for w_ref, w_val in zip(w_s, w):
            w_ref[rows, gc] = w_val
        dec_s[c, :, gc] = jnp.broadcast_to(decay, (SUBLANES, MXU_WIDTH))

    def step(do_project, do_stage, do_finish):
        out_blocks = [[None] * heads for _ in range(chunks)]
        projections, raws, scores = [], {}, {}
        if do_project:
            h = _rms(xp_ref[0], gpre_ref[...]).astype(BF16)
            projections = [cols for cg in range(groups) for cols in part_cols(cg)]
        if do_stage:
            lb = _lower_bound(lbl_ref[...], j)
            for cg in range(groups):
                for c in range(chunks):
                    rows = slice(c * HGRN_CHUNK, (c + 1) * HGRN_CHUNK)
                    raws[cg, c] = [raw_s[rows, cols] for cols in part_cols(cg)]
        units = [(cg, c, hd) for cg in range(groups) for c in range(chunks)
                 for hd in range(cg * heads_per_group, (cg + 1) * heads_per_group)]
        for n in range(len(units) + SCORE_LOOKAHEAD):
            if do_finish and n < len(units):
                _, c, hd = units[n]
                scores[c, hd] = chunk_scores(c, hd)
            if projections:
                cols = projections.pop(0)
                raw_s[:, cols] = _dot(h, win_ref[:, cols])
            if n >= SCORE_LOOKAHEAD:
                done = n - SCORE_LOOKAHEAD
                cg, c, hd = units[done]
                if do_finish:
                    out_blocks[c][hd] = chunk_output(c, hd, scores.pop((c, hd)))
                if do_stage and (done + 1) % heads_per_group == 0:
                    stage(lb, cg, c, *raws.pop((cg, c)))
        for cols in projections:
            raw_s[:, cols] = _dot(h, win_ref[:, cols])
        if do_finish:
            gated = jnp.concatenate([jnp.concatenate(blocks, axis=1) for blocks in out_blocks], axis=0)
            y = _dot(gated, wout_ref[...])
            y_ref[0] = xr_ref[0] + _rms(y, gpost_ref[...])

    last = pl.num_programs(0) - 1
    pl.when(g == 0)(functools.partial(step, True, False, False))
    pl.when(jnp.logical_and(g > 0, g < last))(functools.partial(step, True, True, True))
    pl.when(g == last)(functools.partial(step, False, False, True))

    @pl.when(jnp.logical_and(g >= 2, lax.rem(fin, nt) == nt - 1))
    def _():
        for hd in range(heads):
            sout_ref[0, hd] = st_s[hd].T


def _hgrn_sample_kernel(j, seq, x_ref, gpre_ref, wq_ref, wf_ref, wv_ref, wz_ref, lbl_ref, gon_ref,
                        wout_ref, gpost_ref, s0_ref, *rest):
    y_ref, s1_ref, h_s, qe_s, kd_s, decp_s, v_s, o_s, gate_s = rest[-9:]
    hd, blk = pl.program_id(0), pl.program_id(1)
    n = x_ref.shape[0]
    nseq = s0_ref.shape[0]
    heads = o_s.shape[0]

    @pl.when(jnp.logical_and(hd == 0, blk == 0))
    def _():
        h_s[...] = _rms(x_ref[...], gpre_ref[...]).astype(BF16)

    @pl.when(blk == 0)
    def _():
        h = h_s[...]
        lb = _lower_bound(lbl_ref[...], j)
        q = _silu(_dot(h, wq_ref[...].astype(BF16))) * (HEAD_DIM ** -0.5)
        lf2, k = _forget_gate(_dot(h, wf_ref[...].astype(BF16)), lb)
        v = _dot(h, wv_ref[...].astype(BF16))
        gate_s[hd] = _silu(_dot(h, wz_ref[...].astype(BF16)))
        t = lax.broadcasted_iota(jnp.int32, (n, n), 0)
        s = lax.broadcasted_iota(jnp.int32, (n, n), 1)
        sh = seq.bit_length() - 1
        same_seq_causal = ((t >> sh) == (s >> sh)) & (t >= s)
        row = lax.broadcasted_iota(jnp.int32, (n, HEAD_DIM), 0)
        pos = row & (seq - 1)
        b = _cumsum_rows(lf2, seq)
        b_last = _group_allsum(lf2, seq)
        x_mid = b - _group_allsum(jnp.where(pos < seq // 2, lf2, 0.0), seq)
        a = _dot_nt((q * jnp.exp2(x_mid)).astype(BF16), (k * jnp.exp2(-x_mid)).astype(BF16))
        a = jnp.where(same_seq_causal, a, 0.0)
        o_s[hd] = _dot(a.astype(BF16), v.astype(BF16))
        qe_s[...] = q * jnp.exp2(b)
        kd_s[...] = k * jnp.exp2(b_last - b)
        v_s[...] = v
        dec = jnp.exp2(b_last)
        dsw = jnp.where((row & (SUBLANES - 1)) < seq,
                        pltpu.roll(dec, n - seq, axis=0), pltpu.roll(dec, seq, axis=0))
        hi = dsw.astype(BF16).astype(F32)
        mid = (dsw - hi).astype(BF16).astype(F32)
        lo = (dsw - hi - mid).astype(BF16).astype(F32)
        decp_s[...] = jnp.where(pos == 0, hi, jnp.where(pos == 1, mid, jnp.where(pos == 2, lo, 0.0)))

    row = lax.broadcasted_iota(jnp.int32, (SUBLANES, HEAD_DIM), 0)
    pad = jnp.zeros((HEAD_DIM - SUBLANES, HEAD_DIM), F32)
    pad2 = jnp.zeros((HEAD_DIM - SUBLANES, 2 * HEAD_DIM), F32)
    base = blk * (nseq * seq)

    def pairs_body(it, carry):
        for u in range(SAMPLE_PAIRS_PER_ITER):
            p = it * SAMPLE_PAIRS_PER_ITER + u
            rows = pl.ds(pl.multiple_of(base + p * SUBLANES, SUBLANES), SUBLANES)
            qe8 = qe_s[rows, :].astype(BF16)
            kd8 = kd_s[rows, :]
            dp8 = decp_s[rows, :]
            v8 = v_s[rows, :]
            o8 = o_s[hd, rows, :]
            for half in range(2):
                mine = (row >= seq) if half else (row < seq)
                s0 = s0_ref[2 * p + half]
                o8 = o8 + jnp.where(mine, _dot(qe8, s0.astype(BF16)), 0.0)
                lhs = jnp.concatenate([jnp.where(mine, kd8, dp8), pad], axis=0).astype(BF16)
                ones = jnp.where(mine, 0.0, 1.0)
                rhs = jnp.concatenate(
                    [jnp.concatenate([jnp.where(mine, v8, 0.0), ones], axis=1), pad2], axis=0).astype(BF16)
                upd = _dot_tn(lhs, rhs)
                s1_ref[2 * p + half] = upd[:, HEAD_DIM:] * s0 + upd[:, :HEAD_DIM]
            o_s[hd, rows, :] = o8
        return carry

    lax.fori_loop(0, nseq // (2 * SAMPLE_PAIRS_PER_ITER), pairs_body, 0)

    @pl.when(jnp.logical_and(hd == heads - 1, blk == pl.num_programs(1) - 1))
    def _():
        gon = gon_ref[...]
        gated = jnp.concatenate(
            [(_rms(o_s[i], gon) * gate_s[i]).astype(BF16) for i in range(heads)], axis=1)
        y = _dot(gated, wout_ref[...].astype(BF16))
        y_ref[...] = x_ref[...] + _rms(y, gpost_ref[...])


def _const_spec(shape):
    nd = len(shape)
    return pl.BlockSpec(shape, lambda *_: (0,) * nd)


def _layer_spec(arr, layer):
    nd = arr.ndim - 1
    return pl.BlockSpec((None,) + arr.shape[1:], lambda *_: (layer,) + (0,) * nd)


def _params(sem):
    return pltpu.CompilerParams(dimension_semantics=sem, vmem_limit_bytes=VMEM_LIMIT_BYTES)


def _conv_prompt(i, j, x, gpre, win, cw, wout, gpost):
    bsz, t, d = x.shape
    e = wout.shape[1]
    tm = CONV_PROMPT_TILE
    assert t % tm == 0 and tm % SUBLANES == 0
    return pl.pallas_call(
        _conv_prompt_kernel,
        grid=(bsz, t // tm),
        in_specs=[pl.BlockSpec((1, tm, d), lambda b, s: (b, s, 0)),
                  _layer_spec(gpre, i), _layer_spec(win, i), _layer_spec(cw, j),
                  _layer_spec(wout, i), _layer_spec(gpost, i)],
        out_specs=[pl.BlockSpec((1, tm, d), lambda b, s: (b, s, 0)),
                   pl.BlockSpec((1, CONV_W - 1, e), lambda b, s: (b, 0, 0))],
        out_shape=[jax.ShapeDtypeStruct(x.shape, F32),
                   jax.ShapeDtypeStruct((bsz, CONV_W - 1, e), F32)],
        scratch_shapes=[pltpu.VMEM((SUBLANES, e), F32),
                        pltpu.VMEM(win.shape[1:], BF16), pltpu.VMEM(wout.shape[1:], BF16)],
        compiler_params=_params(("arbitrary", "arbitrary")),
        name="conv_prompt",
    )(x, gpre, win, cw, wout, gpost)


def _conv_sample(i, j, xf, seq, state, gpre, win, cw, wout, gpost):
    n, d = xf.shape
    nb = n // seq
    e = wout.shape[1]
    assert CONV_W == 3 and seq & (seq - 1) == 0 and seq >= CONV_W - 1
    y, new_buf = pl.pallas_call(
        functools.partial(_conv_sample_kernel, seq),
        grid=(1,),
        in_specs=[_const_spec(xf.shape), _layer_spec(state, j),
                  _layer_spec(gpre, i), _layer_spec(win, i), _layer_spec(cw, j),
                  _layer_spec(wout, i), _layer_spec(gpost, i)],
        out_specs=[_const_spec((n, d)), _const_spec((nb, (CONV_W - 1) * e))],
        out_shape=[jax.ShapeDtypeStruct((n, d), F32),
                   jax.ShapeDtypeStruct((nb, (CONV_W - 1) * e), F32)],
        scratch_shapes=[pltpu.VMEM(win.shape[1:], BF16), pltpu.VMEM(wout.shape[1:], BF16),
                        pltpu.VMEM((e // LANES, n, LANES), F32),
                        pltpu.VMEM((e // LANES, n, LANES), F32)],
        compiler_params=_params(("arbitrary",)),
        name="conv_sample",
    )(xf, state, gpre, win, cw, wout, gpost)
    return y, new_buf.reshape(nb, CONV_W - 1, e)


def _hgrn_prompt(i, j, x, gpre, win, lb_logits, gon, wout, gpost):
    bsz, t, d = x.shape
    e = wout.shape[1]
    heads = e // HEAD_DIM
    tm = PROMPT_TILE
    assert t % tm == 0 and tm % HGRN_CHUNK == 0
    nt = t // tm
    n_tiles = bsz * nt
    n_bf16 = 5 + len(_level_sizes(HGRN_CHUNK))
    scratch = ([pltpu.VMEM(win.shape[1:], BF16), pltpu.VMEM(wout.shape[1:], BF16),
                pltpu.VMEM((heads, HEAD_DIM, HEAD_DIM), F32),
                pltpu.VMEM((tm // HGRN_CHUNK, SUBLANES, e), F32),
                pltpu.VMEM((tm, 4 * e), F32),
                pltpu.VMEM((tm, e), F32)]
               + [pltpu.VMEM((tm, e), BF16) for _ in range(n_bf16)])

    def projected(g):
        tile = jnp.minimum(g, n_tiles - 1)
        return tile // nt, tile % nt

    def finished(g):
        tile = jnp.maximum(g - 2, 0)
        return tile // nt, tile % nt

    return pl.pallas_call(
        functools.partial(_hgrn_prompt_kernel, j, nt),
        grid=(n_tiles + 2,),
        in_specs=[pl.BlockSpec((1, tm, d), lambda g: projected(g) + (0,)),
                  pl.BlockSpec((1, tm, d), lambda g: finished(g) + (0,)),
                  _layer_spec(gpre, i), _layer_spec(win, i), _const_spec(lb_logits.shape),
                  _layer_spec(gon, j), _layer_spec(wout, i), _layer_spec(gpost, i)],
        out_specs=[pl.BlockSpec((1, tm, d), lambda g: finished(g) + (0,)),
                   pl.BlockSpec((1, heads, HEAD_DIM, HEAD_DIM), lambda g: (finished(g)[0], 0, 0, 0))],
        out_shape=[jax.ShapeDtypeStruct(x.shape, F32),
                   jax.ShapeDtypeStruct((bsz, heads, HEAD_DIM, HEAD_DIM), F32)],
        scratch_shapes=scratch,
        compiler_params=_params(("arbitrary",)),
        name="hgrn_prompt",
    )(x, x, gpre, win, lb_logits, gon, wout, gpost)


def _hgrn_sample(i, j, xf, seq, state, states_out, gpre, win, lb_logits, gon, wout, gpost):
    n, d = xf.shape
    nb = n // seq
    e = wout.shape[1]
    heads = e // HEAD_DIM
    g = SAMPLE_SEQS_PER_STEP
    assert 2 * seq == SUBLANES and nb % g == 0
    assert g % (2 * SAMPLE_PAIRS_PER_ITER) == 0
    assert seq // 2 * -LOG_F_FLOOR * LOG2_E < F32_MAX_EXP2

    def head_cols(part):
        return pl.BlockSpec((None, d, HEAD_DIM), lambda hd, blk: (i, 0, part * heads + hd))

    st_spec = pl.BlockSpec((None, g, None, HEAD_DIM, HEAD_DIM), lambda hd, blk: (j, blk, hd, 0, 0))
    args = [xf, gpre, win, win, win, win, lb_logits, gon, wout, gpost, state]
    in_specs = [_const_spec(xf.shape), _layer_spec(gpre, i)] + [head_cols(part) for part in range(4)] + [
        pl.BlockSpec((lb_logits.shape[0], HEAD_DIM), lambda hd, blk: (0, hd)),
        _layer_spec(gon, j), _layer_spec(wout, i), _layer_spec(gpost, i), st_spec]
    aliases = {}
    if states_out is not None:
        args.append(states_out)
        in_specs.append(pl.BlockSpec(memory_space=pl.ANY))
        aliases = {len(args) - 1: 1}
    y, states_out = pl.pallas_call(
        functools.partial(_hgrn_sample_kernel, j, seq),
        grid=(heads, nb // g),
        in_specs=in_specs,
        out_specs=[_const_spec((n, d)), st_spec],
        out_shape=[jax.ShapeDtypeStruct((n, d), F32), jax.ShapeDtypeStruct(state.shape, F32)],
        scratch_shapes=[pltpu.VMEM((n, d), BF16)]
                       + [pltpu.VMEM((n, HEAD_DIM), F32) for _ in range(4)]
                       + [pltpu.VMEM((heads, n, HEAD_DIM), F32) for _ in range(2)],
        input_output_aliases=aliases,
        compiler_params=_params(("arbitrary", "arbitrary")),
        name="hgrn_sample",
    )(*args)
    return y, states_out


def kernel(x_prompt, x_sample, state_conv, state_hgrn, norm_pre, w_in, conv_w, hgrn_lb_logits,
           hgrn_onorm, w_out, norm_post):
    depth = w_in.shape[0]
    w_in32, w_out32, lb_logits = w_in, w_out, hgrn_lb_logits
    gpre, gpost, gon = norm_pre[:, None, :], norm_post[:, None, :], hgrn_onorm[:, None, :]
    conv_rows = state_conv.reshape(state_conv.shape[0], state_conv.shape[1], -1)
    nb, seq, d = x_sample.shape
    xp, xs = x_prompt, x_sample.reshape(nb * seq, d)
    conv_p, conv_s, hgrn_p, hgrn_s = [], [], [], None
    for i in range(depth):
        j = i // 2
        if i % 2 == 0:
            xp, bp = _conv_prompt(i, j, xp, gpre, w_in32, conv_w, w_out32, gpost)
            xs, bs = _conv_sample(i, j, xs, seq, conv_rows, gpre, w_in32, conv_w, w_out32, gpost)
            conv_p.append(bp)
            conv_s.append(bs)
        else:
            xp, sp = _hgrn_prompt(i, j, xp, gpre, w_in32, lb_logits, gon, w_out32, gpost)
            xs, hgrn_s = _hgrn_sample(i, j, xs, seq, state_hgrn, hgrn_s, gpre, w_in32, lb_logits, gon,
                                      w_out32, gpost)
            hgrn_p.append(sp)
    return (xp, xs.reshape(nb, seq, d), jnp.stack(conv_p), jnp.stack(conv_s), jnp.stack(hgrn_p), hgrn_s)
```

```python
import functools
import math

import jax
import jax.numpy as jnp
from jax import lax
from jax.experimental import pallas as pl
from jax.experimental.pallas import tpu as pltpu

F32 = jnp.float32
BF16 = jnp.bfloat16

RMS_EPS = 1e-6
LOG_F_FLOOR = -20.0
LOG2_E = math.log2(math.e)
HEAD_DIM = 128
LANES = 128
SUBLANES = 8
MXU_WIDTH = 256
F32_MAX_EXP2 = 127
CONV_W = 3
HGRN_CHUNK = 128
BASE_BLOCK = SUBLANES
assert BASE_BLOCK // 2 * -LOG_F_FLOOR * LOG2_E < F32_MAX_EXP2
SCORE_LOOKAHEAD = 2
PROMPT_TILE = 256
CONV_PROMPT_TILE = 512
CONV_ROW_SPLITS = 2
SAMPLE_SEQS_PER_STEP = 128
SAMPLE_PAIRS_PER_ITER = 8
VMEM_LIMIT_BYTES = 56 * 1024 * 1024


def _rms(x, g):
    ms = jnp.mean(x * x, axis=-1, keepdims=True)
    return x * lax.rsqrt(ms + RMS_EPS) * g


def _silu(x):
    return x * (1.0 / (1.0 + jnp.exp2(x * (-LOG2_E))))


def _dot(a, b):
    return jnp.dot(a, b, preferred_element_type=F32)


def _dot_nt(a, b):
    return lax.dot_general(a, b, (((1,), (1,)), ((), ())), preferred_element_type=F32)


def _dot_tn(a, b):
    return lax.dot_general(a, b, (((0,), (0,)), ((), ())), preferred_element_type=F32)


def _lower_bound(logits, j):
    m = jnp.max(logits, axis=0, keepdims=True)
    e = jnp.exp(logits - m)
    p = e / jnp.sum(e, axis=0, keepdims=True)
    if j == 0:
        return jnp.zeros_like(p[0:1])
    return jnp.sum(p[1:j + 1], axis=0, keepdims=True)


def _forget_gate(fpre, lb):
    e = jnp.exp2(jnp.abs(fpre) * (-LOG2_E))
    r = 1.0 / (1.0 + e)
    er = e * r
    pos = fpre >= 0
    sig = jnp.where(pos, r, er)
    nsig = jnp.where(pos, er, r)
    f = lb + (1.0 - lb) * sig
    log2_f = jnp.maximum(jnp.log(f) * LOG2_E, LOG_F_FLOOR * LOG2_E)
    k = jnp.minimum((1.0 - lb) * nsig, 1.0 - math.exp(LOG_F_FLOOR))
    return log2_f, k


def _cumsum_rows(x, group):
    pos = lax.broadcasted_iota(jnp.int32, x.shape, 0) & (group - 1)
    s = 1
    while s < group:
        x = x + jnp.where(pos >= s, pltpu.roll(x, s, axis=0), 0.0)
        s *= 2
    return x


def _group_allsum(x, group):
    n = x.shape[0]
    pos = lax.broadcasted_iota(jnp.int32, x.shape, 0) & (group - 1)
    s = 1
    while s < group:
        partner = jnp.where((pos & s) == 0, pltpu.roll(x, n - s, axis=0), pltpu.roll(x, s, axis=0))
        x = x + partner
        s *= 2
    return x


def _level_matrix(n):
    t = lax.broadcasted_iota(jnp.int32, (n, n), 0)
    s = lax.broadcasted_iota(jnp.int32, (n, n), 1)
    sh = BASE_BLOCK.bit_length() - 1
    lv = jnp.where(((t >> sh) == (s >> sh)) & (t >= s), 1, 0)
    c, i = 2 * BASE_BLOCK, 2
    while c <= n:
        half, sh = c // 2, sh + 1
        own = ((t >> sh) == (s >> sh)) & ((t & half) != 0) & ((s & half) == 0)
        lv = jnp.where(own, i, lv)
        c, i = 2 * c, i + 1
    return lv


def _level_sizes(chunk):
    sizes, c = [], 2 * BASE_BLOCK
    while c <= chunk:
        sizes.append(c)
        c *= 2
    return sizes


def _chunk_operands(q, k, lf2):
    c = q.shape[0]
    local = _cumsum_rows(lf2, BASE_BLOCK)
    b_parts, x_parts, carry = [], [], None
    for g in range(c // BASE_BLOCK):
        blk = local[g * BASE_BLOCK:(g + 1) * BASE_BLOCK]
        x_parts.append(blk - blk[BASE_BLOCK // 2 - 1:BASE_BLOCK // 2])
        if carry is not None:
            blk = blk + carry
        b_parts.append(blk)
        carry = blk[BASE_BLOCK - 1:BASE_BLOCK]
    b = jnp.concatenate(b_parts, axis=0)
    x = jnp.concatenate(x_parts, axis=0)
    b_last = carry
    qe = (q * jnp.exp2(b)).astype(BF16)
    kd = (k * jnp.exp2(b_last - b)).astype(BF16)
    q8 = (q * jnp.exp2(x)).astype(BF16)
    k8 = (k * jnp.exp2(-x)).astype(BF16)
    w = []
    for size in _level_sizes(c):
        half = size // 2
        parts = []
        for s0 in range(0, c, size):
            r = b[s0 + half - 1:s0 + half]
            lo, hi = slice(s0, s0 + half), slice(s0 + half, s0 + size)
            parts.append(k[lo] * jnp.exp2(r - b[lo]))
            parts.append(q[hi] * jnp.exp2(b[hi] - r))
        w.append(jnp.concatenate(parts, axis=0).astype(BF16))
    return qe, kd, q8, k8, w, jnp.exp2(b_last)


def _conv_project(h, win_ref):
    e = win_ref.shape[1] // 4
    return tuple(_dot(h, win_ref[:, part * e:(part + 1) * e]) for part in range(4))


def _conv_gate(proj, cw_ref, prev1, prev2, pos):
    v, bg, cg, z = proj
    u = cg * v
    u1 = jnp.where(pos >= 1, pltpu.roll(u, 1, axis=0), prev1)
    u2 = jnp.where(pos >= 2, pltpu.roll(u, 2, axis=0), prev2)
    conv = cw_ref[0:1, :] * u2 + cw_ref[1:2, :] * u1 + cw_ref[2:3, :] * u
    return bg * conv * _silu(z), u


def _conv_prompt_kernel(x_ref, gpre_ref, win_in_ref, cw_ref, wout_in_ref, gpost_ref,
                        y_ref, buf_ref, tail_ref, win_ref, wout_ref):
    @pl.when(jnp.logical_and(pl.program_id(0) == 0, pl.program_id(1) == 0))
    def _():
        win_ref[...] = win_in_ref[...].astype(BF16)
        wout_ref[...] = wout_in_ref[...].astype(BF16)

    @pl.when(pl.program_id(1) == 0)
    def _():
        tail_ref[...] = jnp.zeros_like(tail_ref)

    tm = x_ref.shape[1]
    rows = tm // CONV_ROW_SPLITS
    e = win_ref.shape[1] // 4
    pos = lax.broadcasted_iota(jnp.int32, (rows, e), 0)
    tail = tail_ref[...]
    xs = [x_ref[0, k * rows:(k + 1) * rows, :] for k in range(CONV_ROW_SPLITS)]
    proj = _conv_project(_rms(xs[0], gpre_ref[...]).astype(BF16), win_ref)
    for k in range(CONV_ROW_SPLITS):
        if k + 1 < CONV_ROW_SPLITS:
            next_proj = _conv_project(_rms(xs[k + 1], gpre_ref[...]).astype(BF16), win_ref)
        prev1 = tail[SUBLANES - 1:SUBLANES, :]
        prev2 = jnp.where(pos == 0, tail[SUBLANES - 2:SUBLANES - 1, :], prev1)
        g, u = _conv_gate(proj, cw_ref, prev1, prev2, pos)
        y = _dot(g.astype(BF16), wout_ref[...])
        y_ref[0, k * rows:(k + 1) * rows, :] = xs[k] + _rms(y, gpost_ref[...])
        tail = u[rows - SUBLANES:rows, :]
        if k + 1 < CONV_ROW_SPLITS:
            proj = next_proj
    tail_ref[...] = tail
    buf_ref[0] = tail[SUBLANES - (CONV_W - 1):SUBLANES, :]


def _conv_sample_kernel(seq, x_ref, buf_ref, gpre_ref, win_in_ref, cw_ref, wout_in_ref, gpost_ref,
                        y_ref, new_buf_ref, win_ref, wout_ref, p1_s, p2_s):
    win_ref[...] = win_in_ref[...].astype(BF16)
    wout_ref[...] = wout_in_ref[...].astype(BF16)
    x = x_ref[...]
    nb = buf_ref.shape[0]
    lane_tiles, n, lanes = p1_s.shape
    e = lane_tiles * lanes
    h = _rms(x, gpre_ref[...]).astype(BF16)
    pos = lax.broadcasted_iota(jnp.int32, (n, e), 0) & (seq - 1)
    p1_s[...] = jnp.zeros_like(p1_s)
    p2_s[...] = jnp.zeros_like(p2_s)
    for c in range(lane_tiles):
        buf0 = buf_ref[:, c * lanes:(c + 1) * lanes]
        buf1 = buf_ref[:, e + c * lanes:e + (c + 1) * lanes]
        p1_s[c, pl.ds(0, nb, stride=seq), :] = buf1
        p2_s[c, pl.ds(0, nb, stride=seq), :] = buf0
        p2_s[c, pl.ds(1, nb, stride=seq), :] = buf1
    prev1 = jnp.concatenate([p1_s[c] for c in range(lane_tiles)], axis=1)
    prev2 = jnp.concatenate([p2_s[c] for c in range(lane_tiles)], axis=1)
    g, u = _conv_gate(_conv_project(h, win_ref), cw_ref, prev1, prev2, pos)
    y = _dot(g.astype(BF16), wout_ref[...])
    y_ref[...] = x + _rms(y, gpost_ref[...])
    for c in range(lane_tiles):
        p1_s[c] = u[:, c * lanes:(c + 1) * lanes]
        for r in range(CONV_W - 1):
            new_buf_ref[:, r * e + c * lanes:r * e + (c + 1) * lanes] = (
                p1_s[c, pl.ds(seq - (CONV_W - 1) + r, nb, stride=seq), :])


def _hgrn_prompt_kernel(j, nt, xp_ref, xr_ref, gpre_ref, win_in_ref, lbl_ref, gon_ref, wout_in_ref, gpost_ref,
                        y_ref, sout_ref, win_ref, wout_ref, st_s, dec_s, raw_s, gate_s,
                        qe_s, kd_s, q8_s, k8_s, v_s, *w_s):
    g = pl.program_id(0)
    e = win_ref.shape[1] // 4
    heads = e // HEAD_DIM
    tm = xp_ref.shape[1]
    fin = g - 2

    @pl.when(g == 0)
    def _():
        win_ref[...] = win_in_ref[...].astype(BF16)
        wout_ref[...] = wout_in_ref[...].astype(BF16)

    @pl.when(jnp.logical_or(g < 2, lax.rem(fin, nt) == 0))
    def _():
        st_s[...] = jnp.zeros_like(st_s)

    lv = _level_matrix(HGRN_CHUNK)
    owned = [lv == i + 1 for i in range(len(w_s) + 1)]
    gon = gon_ref[...]
    chunks = tm // HGRN_CHUNK

    def chunk_scores(c, hd):
        rows = slice(c * HGRN_CHUNK, (c + 1) * HGRN_CHUNK)
        cols = slice(hd * HEAD_DIM, (hd + 1) * HEAD_DIM)
        a = jnp.where(owned[0], _dot_nt(q8_s[rows, cols], k8_s[rows, cols]), 0.0)
        for i, w_ref in enumerate(w_s):
            wv = w_ref[rows, cols]
            a = jnp.where(owned[i + 1], _dot_nt(wv, wv), a)
        return a.astype(BF16)

    def chunk_output(c, hd, a):
        rows = slice(c * HGRN_CHUNK, (c + 1) * HGRN_CHUNK)
        cols = slice(hd * HEAD_DIM, (hd + 1) * HEAD_DIM)
        vb = v_s[rows, cols]
        st = st_s[hd]
        o = _dot(a, vb) + _dot_nt(qe_s[rows, cols], st.astype(BF16))
        st3 = st.reshape(HEAD_DIM // SUBLANES, SUBLANES, HEAD_DIM) * dec_s[c, :, cols]
        st_s[hd] = st3.reshape(HEAD_DIM, HEAD_DIM) + _dot_tn(vb, kd_s[rows, cols])
        return (_rms(o, gon) * gate_s[rows, cols]).astype(BF16)

    groups = e // MXU_WIDTH
    heads_per_group = MXU_WIDTH // HEAD_DIM

    def part_cols(cg):
        return [slice(part * e + cg * MXU_WIDTH, part * e + (cg + 1) * MXU_WIDTH) for part in range(4)]

    def stage(lb, cg, c, q_raw, f_raw, v_raw, z_raw):
        gc = slice(cg * MXU_WIDTH, (cg + 1) * MXU_WIDTH)
        rows = slice(c * HGRN_CHUNK, (c + 1) * HGRN_CHUNK)
        q = _silu(q_raw) * (HEAD_DIM ** -0.5)
        lf2, k = _forget_gate(f_raw, lb[:, gc])
        v_s[rows, gc] = v_raw.astype(BF16)
        gate_s[rows, gc] = _silu(z_raw)
        qe, kd, q8, k8, w, decay = _chunk_operands(q, k, lf2)
        qe_s[rows, gc] = qe
        kd_s[rows, gc] = kd
        q8_s[rows, gc] = q8
        k8_s[rows, gc] = k8
        for w_ref, w_val in zip(w_s, w):
            w_ref[rows, gc] = w_val
        dec_s[c, :, gc] = jnp.broadcast_to(decay, (SUBLANES, MXU_WIDTH))

    def step(do_project, do_stage, do_finish):
        out_blocks = [[None] * heads for _ in range(chunks)]
        projections, raws, scores = [], {}, {}
        if do_project:
            h = _rms(xp_ref[0], gpre_ref[...]).astype(BF16)
            projections = [cols for cg in range(groups) for cols in part_cols(cg)]
        if do_stage:
            lb = _lower_bound(lbl_ref[...], j)
            for cg in range(groups):
                for c in range(chunks):
                    rows = slice(c * HGRN_CHUNK, (c + 1) * HGRN_CHUNK)
                    raws[cg, c] = [raw_s[rows, cols] for cols in part_cols(cg)]
        units = [(cg, c, hd) for cg in range(groups) for c in range(chunks)
                 for hd in range(cg * heads_per_group, (cg + 1) * heads_per_group)]
        for n in range(len(units) + SCORE_LOOKAHEAD):
            if do_finish and n < len(units):
                _, c, hd = units[n]
                scores[c, hd] = chunk_scores(c, hd)
            if projections:
                cols = projections.pop(0)
                raw_s[:, cols] = _dot(h, win_ref[:, cols])
            if n >= SCORE_LOOKAHEAD:
                done = n - SCORE_LOOKAHEAD
                cg, c, hd = units[done]
                if do_finish:
                    out_blocks[c][hd] = chunk_output(c, hd, scores.pop((c, hd)))
                if do_stage and (done + 1) % heads_per_group == 0:
                    stage(lb, cg, c, *raws.pop((cg, c)))
        for cols in projections:
            raw_s[:, cols] = _dot(h, win_ref[:, cols])
        if do_finish:
            gated = jnp.concatenate([jnp.concatenate(blocks, axis=1) for blocks in out_blocks], axis=0)
            y = _dot(gated, wout_ref[...])
            y_ref[0] = xr_ref[0] + _rms(y, gpost_ref[...])

    last = pl.num_programs(0) - 1
    pl.when(g == 0)(functools.partial(step, True, False, False))
    pl.when(g == 1)(functools.partial(step, True, True, False))
    pl.when(jnp.logical_and(g > 1, g < last - 1))(functools.partial(step, True, True, True))
    pl.when(g == last - 1)(functools.partial(step, False, True, True))
    pl.when(g == last)(functools.partial(step, False, False, True))

    @pl.when(jnp.logical_and(g >= 2, lax.rem(fin, nt) == nt - 1))
    def _():
        for hd in range(heads):
            sout_ref[0, hd] = st_s[hd].T


def _hgrn_sample_kernel(j, seq, x_ref, gpre_ref, wq_ref, wf_ref, wv_ref, wz_ref, lbl_ref, gon_ref,
                        wout_ref, gpost_ref, s0_ref, *rest):
    y_ref, s1_ref, h_s, qe_s, kd_s, decp_s, v_s, o_s, gate_s = rest[-9:]
    hd, blk = pl.program_id(0), pl.program_id(1)
    n = x_ref.shape[0]
    nseq = s0_ref.shape[0]
    heads = o_s.shape[0]

    @pl.when(jnp.logical_and(hd == 0, blk == 0))
    def _():
        h_s[...] = _rms(x_ref[...], gpre_ref[...]).astype(BF16)

    @pl.when(blk == 0)
    def _():
        h = h_s[...]
        lb = _lower_bound(lbl_ref[...], j)
        q = _silu(_dot(h, wq_ref[...].astype(BF16))) * (HEAD_DIM ** -0.5)
        lf2, k = _forget_gate(_dot(h, wf_ref[...].astype(BF16)), lb)
        v = _dot(h, wv_ref[...].astype(BF16))
        gate_s[hd] = _silu(_dot(h, wz_ref[...].astype(BF16)))
        t = lax.broadcasted_iota(jnp.int32, (n, n), 0)
        s = lax.broadcasted_iota(jnp.int32, (n, n), 1)
        sh = seq.bit_length() - 1
        same_seq_causal = ((t >> sh) == (s >> sh)) & (t >= s)
        row = lax.broadcasted_iota(jnp.int32, (n, HEAD_DIM), 0)
        pos = row & (seq - 1)
        b = _cumsum_rows(lf2, seq)
        b_last = _group_allsum(lf2, seq)
        x_mid = b - _group_allsum(jnp.where(pos < seq // 2, lf2, 0.0), seq)
        a = _dot_nt((q * jnp.exp2(x_mid)).astype(BF16), (k * jnp.exp2(-x_mid)).astype(BF16))
        a = jnp.where(same_seq_causal, a, 0.0)
        o_s[hd] = _dot(a.astype(BF16), v.astype(BF16))
        qe_s[...] = q * jnp.exp2(b)
        kd_s[...] = k * jnp.exp2(b_last - b)
        v_s[...] = v
        dec = jnp.exp2(b_last)
        dsw = jnp.where((row & (SUBLANES - 1)) < seq,
                        pltpu.roll(dec, n - seq, axis=0), pltpu.roll(dec, seq, axis=0))
        hi = dsw.astype(BF16).astype(F32)
        mid = (dsw - hi).astype(BF16).astype(F32)
        lo = (dsw - hi - mid).astype(BF16).astype(F32)
        decp_s[...] = jnp.where(pos == 0, hi, jnp.where(pos == 1, mid, jnp.where(pos == 2, lo, 0.0)))

    row = lax.broadcasted_iota(jnp.int32, (SUBLANES, HEAD_DIM), 0)
    pad = jnp.zeros((HEAD_DIM - SUBLANES, HEAD_DIM), F32)
    pad2 = jnp.zeros((HEAD_DIM - SUBLANES, 2 * HEAD_DIM), F32)
    base = blk * (nseq * seq)

    def pairs_body(it, carry):
        for u in range(SAMPLE_PAIRS_PER_ITER):
            p = it * SAMPLE_PAIRS_PER_ITER + u
            rows = pl.ds(pl.multiple_of(base + p * SUBLANES, SUBLANES), SUBLANES)
            qe8 = qe_s[rows, :].astype(BF16)
            kd8 = kd_s[rows, :]
            dp8 = decp_s[rows, :]
            v8 = v_s[rows, :]
            o8 = o_s[hd, rows, :]
            for half in range(2):
                mine = (row >= seq) if half else (row < seq)
                s0 = s0_ref[2 * p + half]
                o8 = o8 + jnp.where(mine, _dot(qe8, s0.astype(BF16)), 0.0)
                lhs = jnp.concatenate([jnp.where(mine, kd8, dp8), pad], axis=0).astype(BF16)
                ones = jnp.where(mine, 0.0, 1.0)
                rhs = jnp.concatenate(
                    [jnp.concatenate([jnp.where(mine, v8, 0.0), ones], axis=1), pad2], axis=0).astype(BF16)
                upd = _dot_tn(lhs, rhs)
                s1_ref[2 * p + half] = upd[:, HEAD_DIM:] * s0 + upd[:, :HEAD_DIM]
            o_s[hd, rows, :] = o8
        return carry

    lax.fori_loop(0, nseq // (2 * SAMPLE_PAIRS_PER_ITER), pairs_body, 0)

    @pl.when(jnp.logical_and(hd == heads - 1, blk == pl.num_programs(1) - 1))
    def _():
        gon = gon_ref[...]
        gated = jnp.concatenate(
            [(_rms(o_s[i], gon) * gate_s[i]).astype(BF16) for i in range(heads)], axis=1)
        y = _dot(gated, wout_ref[...].astype(BF16))
        y_ref[...] = x_ref[...] + _rms(y, gpost_ref[...])


def _const_spec(shape):
    nd = len(shape)
    return pl.BlockSpec(shape, lambda *_: (0,) * nd)


def _layer_spec(arr, layer):
    nd = arr.ndim - 1
    return pl.BlockSpec((None,) + arr.shape[1:], lambda *_: (layer,) + (0,) * nd)


def _params(sem):
    return pltpu.CompilerParams(dimension_semantics=sem, vmem_limit_bytes=VMEM_LIMIT_BYTES)


def _conv_prompt(i, j, x, gpre, win, cw, wout, gpost):
    bsz, t, d = x.shape
    e = wout.shape[1]
    tm = CONV_PROMPT_TILE
    assert t % tm == 0 and tm % SUBLANES == 0
    return pl.pallas_call(
        _conv_prompt_kernel,
        grid=(bsz, t // tm),
        in_specs=[pl.BlockSpec((1, tm, d), lambda b, s: (b, s, 0)),
                  _layer_spec(gpre, i), _layer_spec(win, i), _layer_spec(cw, j),
                  _layer_spec(wout, i), _layer_spec(gpost, i)],
        out_specs=[pl.BlockSpec((1, tm, d), lambda b, s: (b, s, 0)),
                   pl.BlockSpec((1, CONV_W - 1, e), lambda b, s: (b, 0, 0))],
        out_shape=[jax.ShapeDtypeStruct(x.shape, F32),
                   jax.ShapeDtypeStruct((bsz, CONV_W - 1, e), F32)],
        scratch_shapes=[pltpu.VMEM((SUBLANES, e), F32),
                        pltpu.VMEM(win.shape[1:], BF16), pltpu.VMEM(wout.shape[1:], BF16)],
        compiler_params=_params(("arbitrary", "arbitrary")),
        name="conv_prompt",
    )(x, gpre, win, cw, wout, gpost)


def _conv_sample(i, j, xf, seq, state, gpre, win, cw, wout, gpost):
    n, d = xf.shape
    nb = n // seq
    e = wout.shape[1]
    assert CONV_W == 3 and seq & (seq - 1) == 0 and seq >= CONV_W - 1
    y, new_buf = pl.pallas_call(
        functools.partial(_conv_sample_kernel, seq),
        grid=(1,),
        in_specs=[_const_spec(xf.shape), _layer_spec(state, j),
                  _layer_spec(gpre, i), _layer_spec(win, i), _layer_spec(cw, j),
                  _layer_spec(wout, i), _layer_spec(gpost, i)],
        out_specs=[_const_spec((n, d)), _const_spec((nb, (CONV_W - 1) * e))],
        out_shape=[jax.ShapeDtypeStruct((n, d), F32),
                   jax.ShapeDtypeStruct((nb, (CONV_W - 1) * e), F32)],
        scratch_shapes=[pltpu.VMEM(win.shape[1:], BF16), pltpu.VMEM(wout.shape[1:], BF16),
                        pltpu.VMEM((e // LANES, n, LANES), F32),
                        pltpu.VMEM((e // LANES, n, LANES), F32)],
        compiler_params=_params(("arbitrary",)),
        name="conv_sample",
    )(xf, state, gpre, win, cw, wout, gpost)
    return y, new_buf.reshape(nb, CONV_W - 1, e)


def _hgrn_prompt(i, j, x, gpre, win, lb_logits, gon, wout, gpost):
    bsz, t, d = x.shape
    e = wout.shape[1]
    heads = e // HEAD_DIM
    tm = PROMPT_TILE
    assert t % tm == 0 and tm % HGRN_CHUNK == 0
    nt = t // tm
    n_tiles = bsz * nt
    n_bf16 = 5 + len(_level_sizes(HGRN_CHUNK))
    scratch = ([pltpu.VMEM(win.shape[1:], BF16), pltpu.VMEM(wout.shape[1:], BF16),
                pltpu.VMEM((heads, HEAD_DIM, HEAD_DIM), F32),
                pltpu.VMEM((tm // HGRN_CHUNK, SUBLANES, e), F32),
                pltpu.VMEM((tm, 4 * e), F32),
                pltpu.VMEM((tm, e), F32)]
               + [pltpu.VMEM((tm, e), BF16) for _ in range(n_bf16)])

    def projected(g):
        tile = jnp.minimum(g, n_tiles - 1)
        return tile // nt, tile % nt

    def finished(g):
        tile = jnp.maximum(g - 2, 0)
        return tile // nt, tile % nt

    return pl.pallas_call(
        functools.partial(_hgrn_prompt_kernel, j, nt),
        grid=(n_tiles + 2,),
        in_specs=[pl.BlockSpec((1, tm, d), lambda g: projected(g) + (0,)),
                  pl.BlockSpec((1, tm, d), lambda g: finished(g) + (0,)),
                  _layer_spec(gpre, i), _layer_spec(win, i), _const_spec(lb_logits.shape),
                  _layer_spec(gon, j), _layer_spec(wout, i), _layer_spec(gpost, i)],
        out_specs=[pl.BlockSpec((1, tm, d), lambda g: finished(g) + (0,)),
                   pl.BlockSpec((1, heads, HEAD_DIM, HEAD_DIM), lambda g: (finished(g)[0], 0, 0, 0))],
        out_shape=[jax.ShapeDtypeStruct(x.shape, F32),
                   jax.ShapeDtypeStruct((bsz, heads, HEAD_DIM, HEAD_DIM), F32)],
        scratch_shapes=scratch,
        compiler_params=_params(("arbitrary",)),
        name="hgrn_prompt",
    )(x, x, gpre, win, lb_logits, gon, wout, gpost)


def _hgrn_sample(i, j, xf, seq, state, states_out, gpre, win, lb_logits, gon, wout, gpost):
    n, d = xf.shape
    nb = n // seq
    e = wout.shape[1]
    heads = e // HEAD_DIM
    g = SAMPLE_SEQS_PER_STEP
    assert 2 * seq == SUBLANES and nb % g == 0
    assert g % (2 * SAMPLE_PAIRS_PER_ITER) == 0
    assert seq // 2 * -LOG_F_FLOOR * LOG2_E < F32_MAX_EXP2

    def head_cols(part):
        return pl.BlockSpec((None, d, HEAD_DIM), lambda hd, blk: (i, 0, part * heads + hd))

    st_spec = pl.BlockSpec((None, g, None, HEAD_DIM, HEAD_DIM), lambda hd, blk: (j, blk, hd, 0, 0))
    args = [xf, gpre, win, win, win, win, lb_logits, gon, wout, gpost, state]
    in_specs = [_const_spec(xf.shape), _layer_spec(gpre, i)] + [head_cols(part) for part in range(4)] + [
        pl.BlockSpec((lb_logits.shape[0], HEAD_DIM), lambda hd, blk: (0, hd)),
        _layer_spec(gon, j), _layer_spec(wout, i), _layer_spec(gpost, i), st_spec]
    aliases = {}
    if states_out is not None:
        args.append(states_out)
        in_specs.append(pl.BlockSpec(memory_space=pl.ANY))
        aliases = {len(args) - 1: 1}
    y, states_out = pl.pallas_call(
        functools.partial(_hgrn_sample_kernel, j, seq),
        grid=(heads, nb // g),
        in_specs=in_specs,
        out_specs=[_const_spec((n, d)), st_spec],
        out_shape=[jax.ShapeDtypeStruct((n, d), F32), jax.ShapeDtypeStruct(state.shape, F32)],
        scratch_shapes=[pltpu.VMEM((n, d), BF16)]
                       + [pltpu.VMEM((n, HEAD_DIM), F32) for _ in range(4)]
                       + [pltpu.VMEM((heads, n, HEAD_DIM), F32) for _ in range(2)],
        input_output_aliases=aliases,
        compiler_params=_params(("arbitrary", "arbitrary")),
        name="hgrn_sample",
    )(*args)
    return y, states_out


def kernel(x_prompt, x_sample, state_conv, state_hgrn, norm_pre, w_in, conv_w, hgrn_lb_logits,
           hgrn_onorm, w_out, norm_post):
    depth = w_in.shape[0]
    w_in32, w_out32, lb_logits = w_in, w_out, hgrn_lb_logits
    gpre, gpost, gon = norm_pre[:, None, :], norm_post[:, None, :], hgrn_onorm[:, None, :]
    conv_rows = state_conv.reshape(state_conv.shape[0], state_conv.shape[1], -1)
    nb, seq, d = x_sample.shape
    xp, xs = x_prompt, x_sample.reshape(nb * seq, d)
    conv_p, conv_s, hgrn_p, hgrn_s = [], [], [], None
    for i in range(depth):
        j = i // 2
        if i % 2 == 0:
            xp, bp = _conv_prompt(i, j, xp, gpre, w_in32, conv_w, w_out32, gpost)
            xs, bs = _conv_sample(i, j, xs, seq, conv_rows, gpre, w_in32, conv_w, w_out32, gpost)
            conv_p.append(bp)
            conv_s.append(bs)
        else:
            xp, sp = _hgrn_prompt(i, j, xp, gpre, w_in32, lb_logits, gon, w_out32, gpost)
            xs, hgrn_s = _hgrn_sample(i, j, xs, seq, state_hgrn, hgrn_s, gpre, w_in32, lb_logits, gon,
                                      w_out32, gpost)
            hgrn_p.append(sp)
    return (xp, xs.reshape(nb, seq, d), jnp.stack(conv_p), jnp.stack(conv_s), jnp.stack(hgrn_p), hgrn_s)
```

```python
import functools
import math

import jax
import jax.numpy as jnp
from jax import lax
from jax.experimental import pallas as pl
from jax.experimental.pallas import tpu as pltpu

F32 = jnp.float32
BF16 = jnp.bfloat16

RMS_EPS = 1e-6
LOG_F_FLOOR = -20.0
LOG2_E = math.log2(math.e)
HEAD_DIM = 128
LANES = 128
SUBLANES = 8
MXU_WIDTH = 256
F32_MAX_EXP2 = 127
CONV_W = 3
HGRN_CHUNK = 128
BASE_BLOCK = SUBLANES
assert BASE_BLOCK // 2 * -LOG_F_FLOOR * LOG2_E < F32_MAX_EXP2
SCORE_LOOKAHEAD = 2
PROMPT_TILE = 256
CONV_PROMPT_TILE = 512
CONV_ROW_SPLITS = 2
SAMPLE_SEQS_PER_STEP = 128
SAMPLE_PAIRS_PER_ITER = 8
VMEM_LIMIT_BYTES = 56 * 1024 * 1024
CONV_VMEM_LIMIT_BYTES = 60 * 1024 * 1024


def _rms(x, g):
    ms = jnp.mean(x * x, axis=-1, keepdims=True)
    return x * lax.rsqrt(ms + RMS_EPS) * g


def _silu(x):
    return x * (1.0 / (1.0 + jnp.exp2(x * (-LOG2_E))))


def _dot(a, b):
    return jnp.dot(a, b, preferred_element_type=F32)


def _dot_nt(a, b):
    return lax.dot_general(a, b, (((1,), (1,)), ((), ())), preferred_element_type=F32)


def _dot_tn(a, b):
    return lax.dot_general(a, b, (((0,), (0,)), ((), ())), preferred_element_type=F32)


def _lower_bound(logits, j):
    m = jnp.max(logits, axis=0, keepdims=True)
    e = jnp.exp(logits - m)
    p = e / jnp.sum(e, axis=0, keepdims=True)
    if j == 0:
        return jnp.zeros_like(p[0:1])
    return jnp.sum(p[1:j + 1], axis=0, keepdims=True)


def _forget_gate(fpre, lb):
    e = jnp.exp2(jnp.abs(fpre) * (-LOG2_E))
    r = 1.0 / (1.0 + e)
    er = e * r
    pos = fpre >= 0
    sig = jnp.where(pos, r, er)
    nsig = jnp.where(pos, er, r)
    f = lb + (1.0 - lb) * sig
    log2_f = jnp.maximum(jnp.log(f) * LOG2_E, LOG_F_FLOOR * LOG2_E)
    k = jnp.minimum((1.0 - lb) * nsig, 1.0 - math.exp(LOG_F_FLOOR))
    return log2_f, k


def _cumsum_rows(x, group):
    pos = lax.broadcasted_iota(jnp.int32, x.shape, 0) & (group - 1)
    s = 1
    while s < group:
        x = x + jnp.where(pos >= s, pltpu.roll(x, s, axis=0), 0.0)
        s *= 2
    return x


def _group_allsum(x, group):
    n = x.shape[0]
    pos = lax.broadcasted_iota(jnp.int32, x.shape, 0) & (group - 1)
    s = 1
    while s < group:
        partner = jnp.where((pos & s) == 0, pltpu.roll(x, n - s, axis=0), pltpu.roll(x, s, axis=0))
        x = x + partner
        s *= 2
    return x


def _level_matrix(n):
    t = lax.broadcasted_iota(jnp.int32, (n, n), 0)
    s = lax.broadcasted_iota(jnp.int32, (n, n), 1)
    sh = BASE_BLOCK.bit_length() - 1
    lv = jnp.where(((t >> sh) == (s >> sh)) & (t >= s), 1, 0)
    c, i = 2 * BASE_BLOCK, 2
    while c <= n:
        half, sh = c // 2, sh + 1
        own = ((t >> sh) == (s >> sh)) & ((t & half) != 0) & ((s & half) == 0)
        lv = jnp.where(own, i, lv)
        c, i = 2 * c, i + 1
    return lv


def _level_sizes(chunk):
    sizes, c = [], 2 * BASE_BLOCK
    while c <= chunk:
        sizes.append(c)
        c *= 2
    return sizes


def _chunk_operands(q, k, lf2):
    c = q.shape[0]
    local = _cumsum_rows(lf2, BASE_BLOCK)
    b_parts, x_parts, carry = [], [], None
    for g in range(c // BASE_BLOCK):
        blk = local[g * BASE_BLOCK:(g + 1) * BASE_BLOCK]
        x_parts.append(blk - blk[BASE_BLOCK // 2 - 1:BASE_BLOCK // 2])
        if carry is not None:
            blk = blk + carry
        b_parts.append(blk)
        carry = blk[BASE_BLOCK - 1:BASE_BLOCK]
    b = jnp.concatenate(b_parts, axis=0)
    x = jnp.concatenate(x_parts, axis=0)
    b_last = carry
    qe = (q * jnp.exp2(b)).astype(BF16)
    kd = (k * jnp.exp2(b_last - b)).astype(BF16)
    q8 = (q * jnp.exp2(x)).astype(BF16)
    k8 = (k * jnp.exp2(-x)).astype(BF16)
    w = []
    for size in _level_sizes(c):
        half = size // 2
        parts = []
        for s0 in range(0, c, size):
            r = b[s0 + half - 1:s0 + half]
            lo, hi = slice(s0, s0 + half), slice(s0 + half, s0 + size)
            parts.append(k[lo] * jnp.exp2(r - b[lo]))
            parts.append(q[hi] * jnp.exp2(b[hi] - r))
        w.append(jnp.concatenate(parts, axis=0).astype(BF16))
    return qe, kd, q8, k8, w, jnp.exp2(b_last)


def _conv_project(h, win_ref):
    e = win_ref.shape[1] // 4
    return tuple(_dot(h, win_ref[:, part * e:(part + 1) * e]) for part in range(4))


def _conv_gate(proj, cw_ref, prev1, prev2, pos):
    v, bg, cg, z = proj
    u = cg * v
    u1 = jnp.where(pos >= 1, pltpu.roll(u, 1, axis=0), prev1)
    u2 = jnp.where(pos >= 2, pltpu.roll(u, 2, axis=0), prev2)
    conv = cw_ref[0:1, :] * u2 + cw_ref[1:2, :] * u1 + cw_ref[2:3, :] * u
    return bg * conv * _silu(z), u


def _conv_kernel(nt, seq, x_ref, xs_ref, sbuf_ref, gpre_ref, win_in_ref, cw_ref, wout_in_ref, gpost_ref,
                 y_ref, buf_ref, ys_ref, new_sbuf_ref, tail_ref, win_ref, wout_ref, p1_s, p2_s):
    g = pl.program_id(0)
    n_tiles = pl.num_programs(0) - 1
    e = win_ref.shape[1] // 4

    @pl.when(g == 0)
    def _():
        win_ref[...] = win_in_ref[...].astype(BF16)
        wout_ref[...] = wout_in_ref[...].astype(BF16)

    @pl.when(lax.rem(g, nt) == 0)
    def _():
        tail_ref[...] = jnp.zeros_like(tail_ref)

    @pl.when(g < n_tiles)
    def _():
        tm = x_ref.shape[1]
        rows = tm // CONV_ROW_SPLITS
        pos = lax.broadcasted_iota(jnp.int32, (rows, e), 0)
        tail = tail_ref[...]
        xs = [x_ref[0, k * rows:(k + 1) * rows, :] for k in range(CONV_ROW_SPLITS)]
        proj = _conv_project(_rms(xs[0], gpre_ref[...]).astype(BF16), win_ref)
        for k in range(CONV_ROW_SPLITS):
            if k + 1 < CONV_ROW_SPLITS:
                next_proj = _conv_project(_rms(xs[k + 1], gpre_ref[...]).astype(BF16), win_ref)
            prev1 = tail[SUBLANES - 1:SUBLANES, :]
            prev2 = jnp.where(pos == 0, tail[SUBLANES - 2:SUBLANES - 1, :], prev1)
            gated, u = _conv_gate(proj, cw_ref, prev1, prev2, pos)
            y = _dot(gated.astype(BF16), wout_ref[...])
            y_ref[0, k * rows:(k + 1) * rows, :] = xs[k] + _rms(y, gpost_ref[...])
            tail = u[rows - SUBLANES:rows, :]
            if k + 1 < CONV_ROW_SPLITS:
                proj = next_proj
        tail_ref[...] = tail
        buf_ref[0] = tail[SUBLANES - (CONV_W - 1):SUBLANES, :]

    @pl.when(g == n_tiles)
    def _():
        x = xs_ref[...]
        nb = sbuf_ref.shape[0]
        lane_tiles, n, lanes = p1_s.shape
        h = _rms(x, gpre_ref[...]).astype(BF16)
        pos = lax.broadcasted_iota(jnp.int32, (n, e), 0) & (seq - 1)
        p1_s[...] = jnp.zeros_like(p1_s)
        p2_s[...] = jnp.zeros_like(p2_s)
        for c in range(lane_tiles):
            buf0 = sbuf_ref[:, c * lanes:(c + 1) * lanes]
            buf1 = sbuf_ref[:, e + c * lanes:e + (c + 1) * lanes]
            p1_s[c, pl.ds(0, nb, stride=seq), :] = buf1
            p2_s[c, pl.ds(0, nb, stride=seq), :] = buf0
            p2_s[c, pl.ds(1, nb, stride=seq), :] = buf1
        prev1 = jnp.concatenate([p1_s[c] for c in range(lane_tiles)], axis=1)
        prev2 = jnp.concatenate([p2_s[c] for c in range(lane_tiles)], axis=1)
        gated, u = _conv_gate(_conv_project(h, win_ref), cw_ref, prev1, prev2, pos)
        y = _dot(gated.astype(BF16), wout_ref[...])
        ys_ref[...] = x + _rms(y, gpost_ref[...])
        for c in range(lane_tiles):
            p1_s[c] = u[:, c * lanes:(c + 1) * lanes]
            for r in range(CONV_W - 1):
                new_sbuf_ref[:, r * e + c * lanes:r * e + (c + 1) * lanes] = (
                    p1_s[c, pl.ds(seq - (CONV_W - 1) + r, nb, stride=seq), :])


def _hgrn_prompt_kernel(j, nt, xp_ref, xr_ref, gpre_ref, win_in_ref, lbl_ref, gon_ref, wout_in_ref, gpost_ref,
                        y_ref, sout_ref, win_ref, wout_ref, st_s, dec_s, raw_s, gate_s,
                        qe_s, kd_s, q8_s, k8_s, v_s, *w_s):
    g = pl.program_id(0)
    e = win_ref.shape[1] // 4
    heads = e // HEAD_DIM
    tm = xp_ref.shape[1]
    fin = g - 2

    @pl.when(g == 0)
    def _():
        win_ref[...] = win_in_ref[...].astype(BF16)
        wout_ref[...] = wout_in_ref[...].astype(BF16)
        for ref in (dec_s, raw_s, gate_s, qe_s, kd_s, q8_s, k8_s, v_s) + tuple(w_s):
            ref[...] = jnp.zeros_like(ref)

    @pl.when(jnp.logical_or(g < 2, lax.rem(fin, nt) == 0))
    def _():
        st_s[...] = jnp.zeros_like(st_s)

    lv = _level_matrix(HGRN_CHUNK)
    owned = [lv == i + 1 for i in range(len(w_s) + 1)]
    gon = gon_ref[...]
    chunks = tm // HGRN_CHUNK

    def chunk_scores(c, hd):
        rows = slice(c * HGRN_CHUNK, (c + 1) * HGRN_CHUNK)
        cols = slice(hd * HEAD_DIM, (hd + 1) * HEAD_DIM)
        a = jnp.where(owned[0], _dot_nt(q8_s[rows, cols], k8_s[rows, cols]), 0.0)
        for i, w_ref in enumerate(w_s):
            wv = w_ref[rows, cols]
            a = jnp.where(owned[i + 1], _dot_nt(wv, wv), a)
        return a.astype(BF16)

    def chunk_output(c, hd, a):
        rows = slice(c * HGRN_CHUNK, (c + 1) * HGRN_CHUNK)
        cols = slice(hd * HEAD_DIM, (hd + 1) * HEAD_DIM)
        vb = v_s[rows, cols]
        st = st_s[hd]
        o = _dot(a, vb) + _dot_nt(qe_s[rows, cols], st.astype(BF16))
        st3 = st.reshape(HEAD_DIM // SUBLANES, SUBLANES, HEAD_DIM) * dec_s[c, :, cols]
        st_s[hd] = st3.reshape(HEAD_DIM, HEAD_DIM) + _dot_tn(vb, kd_s[rows, cols])
        return (_rms(o, gon) * gate_s[rows, cols]).astype(BF16)

    groups = e // MXU_WIDTH
    heads_per_group = MXU_WIDTH // HEAD_DIM

    def part_cols(cg):
        return [slice(part * e + cg * MXU_WIDTH, part * e + (cg + 1) * MXU_WIDTH) for part in range(4)]

    def stage(lb, cg, c, q_raw, f_raw, v_raw, z_raw):
        gc = slice(cg * MXU_WIDTH, (cg + 1) * MXU_WIDTH)
        rows = slice(c * HGRN_CHUNK, (c + 1) * HGRN_CHUNK)
        q = _silu(q_raw) * (HEAD_DIM ** -0.5)
        lf2, k = _forget_gate(f_raw, lb[:, gc])
        v_s[rows, gc] = v_raw.astype(BF16)
        gate_s[rows, gc] = _silu(z_raw)
        qe, kd, q8, k8, w, decay = _chunk_operands(q, k, lf2)
        qe_s[rows, gc] = qe
        kd_s[rows, gc] = kd
        q8_s[rows, gc] = q8
        k8_s[rows, gc] = k8
        for w_ref, w_val in zip(w_s, w):
            w_ref[rows, gc] = w_val
        dec_s[c, :, gc] = jnp.broadcast_to(decay, (SUBLANES, MXU_WIDTH))

    def step(do_project, do_stage, do_finish):
        out_blocks = [[None] * heads for _ in range(chunks)]
        projections, raws, scores = [], {}, {}
        if do_project:
            h = _rms(xp_ref[0], gpre_ref[...]).astype(BF16)
            projections = [cols for cg in range(groups) for cols in part_cols(cg)]
        if do_stage:
            lb = _lower_bound(lbl_ref[...], j)
            for cg in range(groups):
                for c in range(chunks):
                    rows = slice(c * HGRN_CHUNK, (c + 1) * HGRN_CHUNK)
                    raws[cg, c] = [raw_s[rows, cols] for cols in part_cols(cg)]
        units = [(cg, c, hd) for cg in range(groups) for c in range(chunks)
                 for hd in range(cg * heads_per_group, (cg + 1) * heads_per_group)]
        for n in range(len(units) + SCORE_LOOKAHEAD):
            if do_finish and n < len(units):
                _, c, hd = units[n]
                scores[c, hd] = chunk_scores(c, hd)
            if projections:
                cols = projections.pop(0)
                raw_s[:, cols] = _dot(h, win_ref[:, cols])
            if n >= SCORE_LOOKAHEAD:
                done = n - SCORE_LOOKAHEAD
                cg, c, hd = units[done]
                if do_finish:
                    out_blocks[c][hd] = chunk_output(c, hd, scores.pop((c, hd)))
                if do_stage and (done + 1) % heads_per_group == 0:
                    stage(lb, cg, c, *raws.pop((cg, c)))
        for cols in projections:
            raw_s[:, cols] = _dot(h, win_ref[:, cols])
        if do_finish:
            gated = jnp.concatenate([jnp.concatenate(blocks, axis=1) for blocks in out_blocks], axis=0)
            y = _dot(gated, wout_ref[...])
            y_ref[0] = xr_ref[0] + _rms(y, gpost_ref[...])

    last = pl.num_programs(0) - 1
    pl.when(g == 0)(functools.partial(step, True, False, False))
    pl.when(jnp.logical_and(g > 0, g < last))(functools.partial(step, True, True, True))
    pl.when(g == last)(functools.partial(step, False, False, True))

    @pl.when(jnp.logical_and(g >= 2, lax.rem(fin, nt) == nt - 1))
    def _():
        for hd in range(heads):
            sout_ref[0, hd] = st_s[hd].T


def _hgrn_sample_kernel(j, seq, x_ref, gpre_ref, wq_ref, wf_ref, wv_ref, wz_ref, lbl_ref, gon_ref,
                        wout_ref, gpost_ref, s0_ref, *rest):
    y_ref, s1_ref, h_s, qe_s, kd_s, decp_s, v_s, o_s, gate_s = rest[-9:]
    hd, blk = pl.program_id(0), pl.program_id(1)
    n = x_ref.shape[0]
    nseq = s0_ref.shape[0]
    heads = o_s.shape[0]

    @pl.when(jnp.logical_and(hd == 0, blk == 0))
    def _():
        h_s[...] = _rms(x_ref[...], gpre_ref[...]).astype(BF16)

    @pl.when(blk == 0)
    def _():
        h = h_s[...]
        lb = _lower_bound(lbl_ref[...], j)
        q = _silu(_dot(h, wq_ref[...].astype(BF16))) * (HEAD_DIM ** -0.5)
        lf2, k = _forget_gate(_dot(h, wf_ref[...].astype(BF16)), lb)
        v = _dot(h, wv_ref[...].astype(BF16))
        gate_s[hd] = _silu(_dot(h, wz_ref[...].astype(BF16)))
        t = lax.broadcasted_iota(jnp.int32, (n, n), 0)
        s = lax.broadcasted_iota(jnp.int32, (n, n), 1)
        sh = seq.bit_length() - 1
        same_seq_causal = ((t >> sh) == (s >> sh)) & (t >= s)
        row = lax.broadcasted_iota(jnp.int32, (n, HEAD_DIM), 0)
        pos = row & (seq - 1)
        b = _cumsum_rows(lf2, seq)
        b_last = _group_allsum(lf2, seq)
        x_mid = b - _group_allsum(jnp.where(pos < seq // 2, lf2, 0.0), seq)
        a = _dot_nt((q * jnp.exp2(x_mid)).astype(BF16), (k * jnp.exp2(-x_mid)).astype(BF16))
        a = jnp.where(same_seq_causal, a, 0.0)
        o_s[hd] = _dot(a.astype(BF16), v.astype(BF16))
        qe_s[...] = q * jnp.exp2(b)
        kd_s[...] = k * jnp.exp2(b_last - b)
        v_s[...] = v
        dec = jnp.exp2(b_last)
        dsw = jnp.where((row & (SUBLANES - 1)) < seq,
                        pltpu.roll(dec, n - seq, axis=0), pltpu.roll(dec, seq, axis=0))
        hi = dsw.astype(BF16).astype(F32)
        mid = (dsw - hi).astype(BF16).astype(F32)
        lo = (dsw - hi - mid).astype(BF16).astype(F32)
        decp_s[...] = jnp.where(pos == 0, hi, jnp.where(pos == 1, mid, jnp.where(pos == 2, lo, 0.0)))

    row = lax.broadcasted_iota(jnp.int32, (SUBLANES, HEAD_DIM), 0)
    pad = jnp.zeros((HEAD_DIM - SUBLANES, HEAD_DIM), F32)
    pad2 = jnp.zeros((HEAD_DIM - SUBLANES, 2 * HEAD_DIM), F32)
    base = blk * (nseq * seq)

    def pairs_body(it, carry):
        for u in range(SAMPLE_PAIRS_PER_ITER):
            p = it * SAMPLE_PAIRS_PER_ITER + u
            rows = pl.ds(pl.multiple_of(base + p * SUBLANES, SUBLANES), SUBLANES)
            qe8 = qe_s[rows, :].astype(BF16)
            kd8 = kd_s[rows, :]
            dp8 = decp_s[rows, :]
            v8 = v_s[rows, :]
            o8 = o_s[hd, rows, :]
            for half in range(2):
                mine = (row >= seq) if half else (row < seq)
                s0 = s0_ref[2 * p + half]
                o8 = o8 + jnp.where(mine, _dot(qe8, s0.astype(BF16)), 0.0)
                lhs = jnp.concatenate([jnp.where(mine, kd8, dp8), pad], axis=0).astype(BF16)
                ones = jnp.where(mine, 0.0, 1.0)
                rhs = jnp.concatenate(
                    [jnp.concatenate([jnp.where(mine, v8, 0.0), ones], axis=1), pad2], axis=0).astype(BF16)
                upd = _dot_tn(lhs, rhs)
                s1_ref[2 * p + half] = upd[:, HEAD_DIM:] * s0 + upd[:, :HEAD_DIM]
            o_s[hd, rows, :] = o8
        return carry

    lax.fori_loop(0, nseq // (2 * SAMPLE_PAIRS_PER_ITER), pairs_body, 0)

    @pl.when(jnp.logical_and(hd == heads - 1, blk == pl.num_programs(1) - 1))
    def _():
        gon = gon_ref[...]
        gated = jnp.concatenate(
            [(_rms(o_s[i], gon) * gate_s[i]).astype(BF16) for i in range(heads)], axis=1)
        y = _dot(gated, wout_ref[...].astype(BF16))
        y_ref[...] = x_ref[...] + _rms(y, gpost_ref[...])


def _const_spec(shape):
    nd = len(shape)
    return pl.BlockSpec(shape, lambda *_: (0,) * nd)


def _layer_spec(arr, layer):
    nd = arr.ndim - 1
    return pl.BlockSpec((None,) + arr.shape[1:], lambda *_: (layer,) + (0,) * nd)


def _params(sem):
    return pltpu.CompilerParams(dimension_semantics=sem, vmem_limit_bytes=VMEM_LIMIT_BYTES)


def _conv_layer(i, j, x, xf, seq, state, gpre, win, cw, wout, gpost):
    bsz, t, d = x.shape
    n = xf.shape[0]
    nb = n // seq
    e = wout.shape[1]
    tm = CONV_PROMPT_TILE
    assert t % tm == 0 and (tm // CONV_ROW_SPLITS) % SUBLANES == 0
    assert CONV_W == 3 and seq & (seq - 1) == 0 and seq >= CONV_W - 1
    nt = t // tm
    n_tiles = bsz * nt

    def tile(g):
        g = jnp.minimum(g, n_tiles - 1)
        return g // nt, g % nt

    y, buf, ys, new_sbuf = pl.pallas_call(
        functools.partial(_conv_kernel, nt, seq),
        grid=(n_tiles + 1,),
        in_specs=[pl.BlockSpec((1, tm, d), lambda g: tile(g) + (0,)),
                  _const_spec(xf.shape), _layer_spec(state, j),
                  _layer_spec(gpre, i), _layer_spec(win, i), _layer_spec(cw, j),
                  _layer_spec(wout, i), _layer_spec(gpost, i)],
        out_specs=[pl.BlockSpec((1, tm, d), lambda g: tile(g) + (0,)),
                   pl.BlockSpec((1, CONV_W - 1, e), lambda g: (tile(g)[0], 0, 0)),
                   _const_spec((n, d)), _const_spec((nb, (CONV_W - 1) * e))],
        out_shape=[jax.ShapeDtypeStruct(x.shape, F32),
                   jax.ShapeDtypeStruct((bsz, CONV_W - 1, e), F32),
                   jax.ShapeDtypeStruct((n, d), F32),
                   jax.ShapeDtypeStruct((nb, (CONV_W - 1) * e), F32)],
        scratch_shapes=[pltpu.VMEM((SUBLANES, e), F32),
                        pltpu.VMEM(win.shape[1:], BF16), pltpu.VMEM(wout.shape[1:], BF16),
                        pltpu.VMEM((e // LANES, n, LANES), F32),
                        pltpu.VMEM((e // LANES, n, LANES), F32)],
        compiler_params=pltpu.CompilerParams(dimension_semantics=("arbitrary",),
                                             vmem_limit_bytes=CONV_VMEM_LIMIT_BYTES),
        name="conv_layer",
    )(x, xf, state, gpre, win, cw, wout, gpost)
    return y, buf, ys, new_sbuf.reshape(nb, CONV_W - 1, e)


def _hgrn_prompt(i, j, x, gpre, win, lb_logits, gon, wout, gpost):
    bsz, t, d = x.shape
    e = wout.shape[1]
    heads = e // HEAD_DIM
    tm = PROMPT_TILE
    assert t % tm == 0 and tm % HGRN_CHUNK == 0
    nt = t // tm
    n_tiles = bsz * nt
    n_bf16 = 5 + len(_level_sizes(HGRN_CHUNK))
    scratch = ([pltpu.VMEM(win.shape[1:], BF16), pltpu.VMEM(wout.shape[1:], BF16),
                pltpu.VMEM((heads, HEAD_DIM, HEAD_DIM), F32),
                pltpu.VMEM((tm // HGRN_CHUNK, SUBLANES, e), F32),
                pltpu.VMEM((tm, 4 * e), F32),
                pltpu.VMEM((tm, e), F32)]
               + [pltpu.VMEM((tm, e), BF16) for _ in range(n_bf16)])

    def projected(g):
        tile = jnp.minimum(g, n_tiles - 1)
        return tile // nt, tile % nt

    def finished(g):
        tile = jnp.maximum(g - 2, 0)
        return tile // nt, tile % nt

    return pl.pallas_call(
        functools.partial(_hgrn_prompt_kernel, j, nt),
        grid=(n_tiles + 2,),
        in_specs=[pl.BlockSpec((1, tm, d), lambda g: projected(g) + (0,)),
                  pl.BlockSpec((1, tm, d), lambda g: finished(g) + (0,)),
                  _layer_spec(gpre, i), _layer_spec(win, i), _const_spec(lb_logits.shape),
                  _layer_spec(gon, j), _layer_spec(wout, i), _layer_spec(gpost, i)],
        out_specs=[pl.BlockSpec((1, tm, d), lambda g: finished(g) + (0,)),
                   pl.BlockSpec((1, heads, HEAD_DIM, HEAD_DIM), lambda g: (finished(g)[0], 0, 0, 0))],
        out_shape=[jax.ShapeDtypeStruct(x.shape, F32),
                   jax.ShapeDtypeStruct((bsz, heads, HEAD_DIM, HEAD_DIM), F32)],
        scratch_shapes=scratch,
        compiler_params=_params(("arbitrary",)),
        name="hgrn_prompt",
    )(x, x, gpre, win, lb_logits, gon, wout, gpost)


def _hgrn_sample(i, j, xf, seq, state, states_out, gpre, win, lb_logits, gon, wout, gpost):
    n, d = xf.shape
    nb = n // seq
    e = wout.shape[1]
    heads = e // HEAD_DIM
    g = SAMPLE_SEQS_PER_STEP
    assert 2 * seq == SUBLANES and nb % g == 0
    assert g % (2 * SAMPLE_PAIRS_PER_ITER) == 0
    assert seq // 2 * -LOG_F_FLOOR * LOG2_E < F32_MAX_EXP2

    def head_cols(part):
        return pl.BlockSpec((None, d, HEAD_DIM), lambda hd, blk: (i, 0, part * heads + hd))

    st_spec = pl.BlockSpec((None, g, None, HEAD_DIM, HEAD_DIM), lambda hd, blk: (j, blk, hd, 0, 0))
    args = [xf, gpre, win, win, win, win, lb_logits, gon, wout, gpost, state]
    in_specs = [_const_spec(xf.shape), _layer_spec(gpre, i)] + [head_cols(part) for part in range(4)] + [
        pl.BlockSpec((lb_logits.shape[0], HEAD_DIM), lambda hd, blk: (0, hd)),
        _layer_spec(gon, j), _layer_spec(wout, i), _layer_spec(gpost, i), st_spec]
    aliases = {}
    if states_out is not None:
        args.append(states_out)
        in_specs.append(pl.BlockSpec(memory_space=pl.ANY))
        aliases = {len(args) - 1: 1}
    y, states_out = pl.pallas_call(
        functools.partial(_hgrn_sample_kernel, j, seq),
        grid=(heads, nb // g),
        in_specs=in_specs,
        out_specs=[_const_spec((n, d)), st_spec],
        out_shape=[jax.ShapeDtypeStruct((n, d), F32), jax.ShapeDtypeStruct(state.shape, F32)],
        scratch_shapes=[pltpu.VMEM((n, d), BF16)]
                       + [pltpu.VMEM((n, HEAD_DIM), F32) for _ in range(4)]
                       + [pltpu.VMEM((heads, n, HEAD_DIM), F32) for _ in range(2)],
        input_output_aliases=aliases,
        compiler_params=_params(("arbitrary", "arbitrary")),
        name="hgrn_sample",
    )(*args)
    return y, states_out


def kernel(x_prompt, x_sample, state_conv, state_hgrn, norm_pre, w_in, conv_w, hgrn_lb_logits,
           hgrn_onorm, w_out, norm_post):
    depth = w_in.shape[0]
    w_in32, w_out32, lb_logits = w_in, w_out, hgrn_lb_logits
    gpre, gpost, gon = norm_pre[:, None, :], norm_post[:, None, :], hgrn_onorm[:, None, :]
    conv_rows = state_conv.reshape(state_conv.shape[0], state_conv.shape[1], -1)
    nb, seq, d = x_sample.shape
    xp, xs = x_prompt, x_sample.reshape(nb * seq, d)
    conv_p, conv_s, hgrn_p, hgrn_s = [], [], [], None
    for i in range(depth):
        j = i // 2
        if i % 2 == 0:
            xp, bp, xs, bs = _conv_layer(i, j, xp, xs, seq, conv_rows, gpre, w_in32, conv_w, w_out32, gpost)
            conv_p.append(bp)
            conv_s.append(bs)
        else:
            xp, sp = _hgrn_prompt(i, j, xp, gpre, w_in32, lb_logits, gon, w_out32, gpost)
            xs, hgrn_s = _hgrn_sample(i, j, xs, seq, state_hgrn, hgrn_s, gpre, w_in32, lb_logits, gon,
                                      w_out32, gpost)
            hgrn_p.append(sp)
    return (xp, xs.reshape(nb, seq, d), jnp.stack(conv_p), jnp.stack(conv_s), jnp.stack(hgrn_p), hgrn_s)
```

```python
import functools
import math

import jax
import jax.numpy as jnp
from jax import lax
from jax.experimental import pallas as pl
from jax.experimental.pallas import tpu as pltpu

F32 = jnp.float32
BF16 = jnp.bfloat16

RMS_EPS = 1e-6
LOG_F_FLOOR = -20.0
LOG2_E = math.log2(math.e)
HEAD_DIM = 128
LANES = 128
SUBLANES = 8
MXU_WIDTH = 256
F32_MAX_EXP2 = 127
CONV_W = 3
HGRN_CHUNK = 128
BASE_BLOCK = SUBLANES
assert BASE_BLOCK // 2 * -LOG_F_FLOOR * LOG2_E < F32_MAX_EXP2
SCORE_LOOKAHEAD = 2
PROMPT_TILE = 256
CONV_PROMPT_TILE = 512
CONV_ROW_SPLITS = 2
SAMPLE_SEQS_PER_STEP = 128
SAMPLE_PAIRS_PER_ITER = 8
VMEM_LIMIT_BYTES = 56 * 1024 * 1024
CONV_VMEM_LIMIT_BYTES = 60 * 1024 * 1024


def _rms(x, g):
    ms = jnp.mean(x * x, axis=-1, keepdims=True)
    return x * lax.rsqrt(ms + RMS_EPS) * g


def _silu(x):
    return x * (1.0 / (1.0 + jnp.exp2(x * (-LOG2_E))))


def _dot(a, b):
    return jnp.dot(a, b, preferred_element_type=F32)


def _dot_nt(a, b):
    return lax.dot_general(a, b, (((1,), (1,)), ((), ())), preferred_element_type=F32)


def _dot_tn(a, b):
    return lax.dot_general(a, b, (((0,), (0,)), ((), ())), preferred_element_type=F32)


def _lower_bound(logits, j):
    m = jnp.max(logits, axis=0, keepdims=True)
    e = jnp.exp(logits - m)
    p = e / jnp.sum(e, axis=0, keepdims=True)
    if j == 0:
        return jnp.zeros_like(p[0:1])
    return jnp.sum(p[1:j + 1], axis=0, keepdims=True)


def _forget_gate(fpre, lb):
    e = jnp.exp2(jnp.abs(fpre) * (-LOG2_E))
    r = 1.0 / (1.0 + e)
    er = e * r
    pos = fpre >= 0
    sig = jnp.where(pos, r, er)
    nsig = jnp.where(pos, er, r)
    f = lb + (1.0 - lb) * sig
    log2_f = jnp.maximum(jnp.log(f) * LOG2_E, LOG_F_FLOOR * LOG2_E)
    k = jnp.minimum((1.0 - lb) * nsig, 1.0 - math.exp(LOG_F_FLOOR))
    return log2_f, k


def _cumsum_rows(x, group):
    pos = lax.broadcasted_iota(jnp.int32, x.shape, 0) & (group - 1)
    s = 1
    while s < group:
        x = x + jnp.where(pos >= s, pltpu.roll(x, s, axis=0), 0.0)
        s *= 2
    return x


def _group_allsum(x, group):
    n = x.shape[0]
    pos = lax.broadcasted_iota(jnp.int32, x.shape, 0) & (group - 1)
    s = 1
    while s < group:
        partner = jnp.where((pos & s) == 0, pltpu.roll(x, n - s, axis=0), pltpu.roll(x, s, axis=0))
        x = x + partner
        s *= 2
    return x


def _level_matrix(n):
    t = lax.broadcasted_iota(jnp.int32, (n, n), 0)
    s = lax.broadcasted_iota(jnp.int32, (n, n), 1)
    sh = BASE_BLOCK.bit_length() - 1
    lv = jnp.where(((t >> sh) == (s >> sh)) & (t >= s), 1, 0)
    c, i = 2 * BASE_BLOCK, 2
    while c <= n:
        half, sh = c // 2, sh + 1
        own = ((t >> sh) == (s >> sh)) & ((t & half) != 0) & ((s & half) == 0)
        lv = jnp.where(own, i, lv)
        c, i = 2 * c, i + 1
    return lv


def _level_sizes(chunk):
    sizes, c = [], 2 * BASE_BLOCK
    while c <= chunk:
        sizes.append(c)
        c *= 2
    return sizes


def _chunk_operands(q, k, lf2):
    c = q.shape[0]
    local = _cumsum_rows(lf2, BASE_BLOCK)
    b_parts, x_parts, carry = [], [], None
    for g in range(c // BASE_BLOCK):
        blk = local[g * BASE_BLOCK:(g + 1) * BASE_BLOCK]
        x_parts.append(blk - blk[BASE_BLOCK // 2 - 1:BASE_BLOCK // 2])
        if carry is not None:
            blk = blk + carry
        b_parts.append(blk)
        carry = blk[BASE_BLOCK - 1:BASE_BLOCK]
    b = jnp.concatenate(b_parts, axis=0)
    x = jnp.concatenate(x_parts, axis=0)
    b_last = carry
    qe = (q * jnp.exp2(b)).astype(BF16)
    kd = (k * jnp.exp2(b_last - b)).astype(BF16)
    q8 = (q * jnp.exp2(x)).astype(BF16)
    k8 = (k * jnp.exp2(-x)).astype(BF16)
    w = []
    for size in _level_sizes(c):
        half = size // 2
        parts = []
        for s0 in range(0, c, size):
            r = b[s0 + half - 1:s0 + half]
            lo, hi = slice(s0, s0 + half), slice(s0 + half, s0 + size)
            parts.append(k[lo] * jnp.exp2(r - b[lo]))
            parts.append(q[hi] * jnp.exp2(b[hi] - r))
        w.append(jnp.concatenate(parts, axis=0).astype(BF16))
    return qe, kd, q8, k8, w, jnp.exp2(b_last)


def _conv_project(h, win_ref):
    e = win_ref.shape[1] // 4
    return tuple(_dot(h, win_ref[:, part * e:(part + 1) * e]) for part in range(4))


def _conv_gate(proj, cw_ref, prev1, prev2, pos):
    v, bg, cg, z = proj
    u = cg * v
    u1 = jnp.where(pos >= 1, pltpu.roll(u, 1, axis=0), prev1)
    u2 = jnp.where(pos >= 2, pltpu.roll(u, 2, axis=0), prev2)
    conv = cw_ref[0:1, :] * u2 + cw_ref[1:2, :] * u1 + cw_ref[2:3, :] * u
    return bg * conv * _silu(z), u


def _conv_kernel(nt, seq, x_ref, xs_ref, sbuf_ref, gpre_ref, win_in_ref, cw_ref, wout_in_ref, gpost_ref,
                 y_ref, buf_ref, ys_ref, new_sbuf_ref, tail_ref, win_ref, wout_ref, p1_s, p2_s):
    g = pl.program_id(0)
    n_tiles = pl.num_programs(0) - 1
    e = win_ref.shape[1] // 4

    @pl.when(g == 0)
    def _():
        win_ref[...] = win_in_ref[...].astype(BF16)
        wout_ref[...] = wout_in_ref[...].astype(BF16)

    @pl.when(lax.rem(g, nt) == 0)
    def _():
        tail_ref[...] = jnp.zeros_like(tail_ref)

    @pl.when(g < n_tiles)
    def _():
        tm = x_ref.shape[1]
        rows = tm // CONV_ROW_SPLITS
        pos = lax.broadcasted_iota(jnp.int32, (rows, e), 0)
        tail = tail_ref[...]
        xs = [x_ref[0, k * rows:(k + 1) * rows, :] for k in range(CONV_ROW_SPLITS)]
        proj = _conv_project(_rms(xs[0], gpre_ref[...]).astype(BF16), win_ref)
        for k in range(CONV_ROW_SPLITS):
            if k + 1 < CONV_ROW_SPLITS:
                next_proj = _conv_project(_rms(xs[k + 1], gpre_ref[...]).astype(BF16), win_ref)
            prev1 = tail[SUBLANES - 1:SUBLANES, :]
            prev2 = jnp.where(pos == 0, tail[SUBLANES - 2:SUBLANES - 1, :], prev1)
            gated, u = _conv_gate(proj, cw_ref, prev1, prev2, pos)
            y = _dot(gated.astype(BF16), wout_ref[...])
            y_ref[0, k * rows:(k + 1) * rows, :] = xs[k] + _rms(y, gpost_ref[...])
            tail = u[rows - SUBLANES:rows, :]
            if k + 1 < CONV_ROW_SPLITS:
                proj = next_proj
        tail_ref[...] = tail
        buf_ref[0] = tail[SUBLANES - (CONV_W - 1):SUBLANES, :]

    @pl.when(g == n_tiles)
    def _():
        x = xs_ref[...]
        nb = sbuf_ref.shape[0]
        lane_tiles, n, lanes = p1_s.shape
        h = _rms(x, gpre_ref[...]).astype(BF16)
        pos = lax.broadcasted_iota(jnp.int32, (n, e), 0) & (seq - 1)
        p1_s[...] = jnp.zeros_like(p1_s)
        p2_s[...] = jnp.zeros_like(p2_s)
        for c in range(lane_tiles):
            buf0 = sbuf_ref[:, c * lanes:(c + 1) * lanes]
            buf1 = sbuf_ref[:, e + c * lanes:e + (c + 1) * lanes]
            p1_s[c, pl.ds(0, nb, stride=seq), :] = buf1
            p2_s[c, pl.ds(0, nb, stride=seq), :] = buf0
            p2_s[c, pl.ds(1, nb, stride=seq), :] = buf1
        prev1 = jnp.concatenate([p1_s[c] for c in range(lane_tiles)], axis=1)
        prev2 = jnp.concatenate([p2_s[c] for c in range(lane_tiles)], axis=1)
        gated, u = _conv_gate(_conv_project(h, win_ref), cw_ref, prev1, prev2, pos)
        y = _dot(gated.astype(BF16), wout_ref[...])
        ys_ref[...] = x + _rms(y, gpost_ref[...])
        for c in range(lane_tiles):
            p1_s[c] = u[:, c * lanes:(c + 1) * lanes]
            for r in range(CONV_W - 1):
                new_sbuf_ref[:, r * e + c * lanes:r * e + (c + 1) * lanes] = (
                    p1_s[c, pl.ds(seq - (CONV_W - 1) + r, nb, stride=seq), :])


def _hgrn_prompt_kernel(j, nt, xp_ref, xr_ref, gpre_ref, win_in_ref, lbl_ref, gon_ref, wout_in_ref, gpost_ref,
                        y_ref, sout_ref, win_ref, wout_ref, st_s, dec_s, raw_s, gate_s,
                        qe_s, kd_s, q8_s, k8_s, v_s, *w_s):
    g = pl.program_id(0)
    e = win_ref.shape[1] // 4
    heads = e // HEAD_DIM
    tm = xp_ref.shape[1]
    fin = g - 2

    @pl.when(g == 0)
    def _():
        win_ref[...] = win_in_ref[...].astype(BF16)
        wout_ref[...] = wout_in_ref[...].astype(BF16)
        for ref in (dec_s, raw_s, gate_s, qe_s, kd_s, q8_s, k8_s, v_s) + tuple(w_s):
            ref[...] = jnp.zeros_like(ref)

    @pl.when(jnp.logical_or(g < 2, lax.rem(fin, nt) == 0))
    def _():
        st_s[...] = jnp.zeros_like(st_s)

    lv = _level_matrix(HGRN_CHUNK).astype(F32).astype(BF16)
    owned = [lv == i + 1 for i in range(len(w_s) + 1)]
    gon = gon_ref[...]
    chunks = tm // HGRN_CHUNK

    def chunk_scores(c, hd):
        rows = slice(c * HGRN_CHUNK, (c + 1) * HGRN_CHUNK)
        cols = slice(hd * HEAD_DIM, (hd + 1) * HEAD_DIM)
        a = jnp.where(owned[0], _dot_nt(q8_s[rows, cols], k8_s[rows, cols]).astype(BF16), 0.0)
        for i, w_ref in enumerate(w_s):
            wv = w_ref[rows, cols]
            a = jnp.where(owned[i + 1], _dot_nt(wv, wv).astype(BF16), a)
        return a

    def chunk_output(c, hd, a):
        rows = slice(c * HGRN_CHUNK, (c + 1) * HGRN_CHUNK)
        cols = slice(hd * HEAD_DIM, (hd + 1) * HEAD_DIM)
        vb = v_s[rows, cols]
        st = st_s[hd]
        o = _dot(a, vb) + _dot_nt(qe_s[rows, cols], st.astype(BF16))
        st3 = st.reshape(HEAD_DIM // SUBLANES, SUBLANES, HEAD_DIM) * dec_s[c, :, cols]
        st_s[hd] = st3.reshape(HEAD_DIM, HEAD_DIM) + _dot_tn(vb, kd_s[rows, cols])
        return (_rms(o, gon) * gate_s[rows, cols]).astype(BF16)

    groups = e // MXU_WIDTH
    heads_per_group = MXU_WIDTH // HEAD_DIM

    def part_cols(cg):
        return [slice(part * e + cg * MXU_WIDTH, part * e + (cg + 1) * MXU_WIDTH) for part in range(4)]

    def stage(lb, cg, c, q_raw, f_raw, v_raw, z_raw):
        gc = slice(cg * MXU_WIDTH, (cg + 1) * MXU_WIDTH)
        rows = slice(c * HGRN_CHUNK, (c + 1) * HGRN_CHUNK)
        q = _silu(q_raw) * (HEAD_DIM ** -0.5)
        lf2, k = _forget_gate(f_raw, lb[:, gc])
        v_s[rows, gc] = v_raw.astype(BF16)
        gate_s[rows, gc] = _silu(z_raw)
        qe, kd, q8, k8, w, decay = _chunk_operands(q, k, lf2)
        qe_s[rows, gc] = qe
        kd_s[rows, gc] = kd
        q8_s[rows, gc] = q8
        k8_s[rows, gc] = k8
        for w_ref, w_val in zip(w_s, w):
            w_ref[rows, gc] = w_val
        dec_s[c, :, gc] = jnp.broadcast_to(decay, (SUBLANES, MXU_WIDTH))

    def step(do_project, do_stage, do_finish):
        out_blocks = [[None] * heads for _ in range(chunks)]
        projections, raws, scores = [], {}, {}
        if do_project:
            h = _rms(xp_ref[0], gpre_ref[...]).astype(BF16)
            projections = [cols for cg in range(groups) for cols in part_cols(cg)]
        if do_stage:
            lb = _lower_bound(lbl_ref[...], j)
            for cg in range(groups):
                for c in range(chunks):
                    rows = slice(c * HGRN_CHUNK, (c + 1) * HGRN_CHUNK)
                    raws[cg, c] = [raw_s[rows, cols] for cols in part_cols(cg)]
        units = [(cg, c, hd) for cg in range(groups) for c in range(chunks)
                 for hd in range(cg * heads_per_group, (cg + 1) * heads_per_group)]
        for n in range(len(units) + SCORE_LOOKAHEAD):
            if do_finish and n < len(units):
                _, c, hd = units[n]
                scores[c, hd] = chunk_scores(c, hd)
            if projections:
                cols = projections.pop(0)
                raw_s[:, cols] = _dot(h, win_ref[:, cols])
            if n >= SCORE_LOOKAHEAD:
                done = n - SCORE_LOOKAHEAD
                cg, c, hd = units[done]
                if do_finish:
                    out_blocks[c][hd] = chunk_output(c, hd, scores.pop((c, hd)))
                if do_stage and (done + 1) % heads_per_group == 0:
                    stage(lb, cg, c, *raws.pop((cg, c)))
        for cols in projections:
            raw_s[:, cols] = _dot(h, win_ref[:, cols])
        if do_finish:
            gated = jnp.concatenate([jnp.concatenate(blocks, axis=1) for blocks in out_blocks], axis=0)
            y = _dot(gated, wout_ref[...])
            y_ref[0] = xr_ref[0] + _rms(y, gpost_ref[...])

    last = pl.num_programs(0) - 1
    pl.when(g == 0)(functools.partial(step, True, False, False))
    pl.when(jnp.logical_and(g > 0, g < last))(functools.partial(step, True, True, True))
    pl.when(g == last)(functools.partial(step, False, False, True))

    @pl.when(jnp.logical_and(g >= 2, lax.rem(fin, nt) == nt - 1))
    def _():
        for hd in range(heads):
            sout_ref[0, hd] = st_s[hd].T


def _hgrn_sample_kernel(j, seq, x_ref, gpre_ref, wq_ref, wf_ref, wv_ref, wz_ref, lbl_ref, gon_ref,
                        wout_ref, gpost_ref, s0_ref, *rest):
    y_ref, s1_ref, h_s, qe_s, kd_s, decp_s, v_s, o_s, gate_s = rest[-9:]
    hd, blk = pl.program_id(0), pl.program_id(1)
    n = x_ref.shape[0]
    nseq = s0_ref.shape[0]
    heads = o_s.shape[0]

    @pl.when(jnp.logical_and(hd == 0, blk == 0))
    def _():
        h_s[...] = _rms(x_ref[...], gpre_ref[...]).astype(BF16)

    @pl.when(blk == 0)
    def _():
        h = h_s[...]
        lb = _lower_bound(lbl_ref[...], j)
        q = _silu(_dot(h, wq_ref[...].astype(BF16))) * (HEAD_DIM ** -0.5)
        lf2, k = _forget_gate(_dot(h, wf_ref[...].astype(BF16)), lb)
        v = _dot(h, wv_ref[...].astype(BF16))
        gate_s[hd] = _silu(_dot(h, wz_ref[...].astype(BF16)))
        t = lax.broadcasted_iota(jnp.int32, (n, n), 0)
        s = lax.broadcasted_iota(jnp.int32, (n, n), 1)
        sh = seq.bit_length() - 1
        same_seq_causal = ((t >> sh) == (s >> sh)) & (t >= s)
        row = lax.broadcasted_iota(jnp.int32, (n, HEAD_DIM), 0)
        pos = row & (seq - 1)
        b = _cumsum_rows(lf2, seq)
        b_last = _group_allsum(lf2, seq)
        x_mid = b - _group_allsum(jnp.where(pos < seq // 2, lf2, 0.0), seq)
        a = _dot_nt((q * jnp.exp2(x_mid)).astype(BF16), (k * jnp.exp2(-x_mid)).astype(BF16))
        a = jnp.where(same_seq_causal, a, 0.0)
        o_s[hd] = _dot(a.astype(BF16), v.astype(BF16))
        qe_s[...] = q * jnp.exp2(b)
        kd_s[...] = k * jnp.exp2(b_last - b)
        v_s[...] = v
        dec = jnp.exp2(b_last)
        dsw = jnp.where((row & (SUBLANES - 1)) < seq,
                        pltpu.roll(dec, n - seq, axis=0), pltpu.roll(dec, seq, axis=0))
        hi = dsw.astype(BF16).astype(F32)
        mid = (dsw - hi).astype(BF16).astype(F32)
        lo = (dsw - hi - mid).astype(BF16).astype(F32)
        decp_s[...] = jnp.where(pos == 0, hi, jnp.where(pos == 1, mid, jnp.where(pos == 2, lo, 0.0)))

    row = lax.broadcasted_iota(jnp.int32, (SUBLANES, HEAD_DIM), 0)
    pad = jnp.zeros((HEAD_DIM - SUBLANES, HEAD_DIM), F32)
    pad2 = jnp.zeros((HEAD_DIM - SUBLANES, 2 * HEAD_DIM), F32)
    base = blk * (nseq * seq)

    def pairs_body(it, carry):
        for u in range(SAMPLE_PAIRS_PER_ITER):
            p = it * SAMPLE_PAIRS_PER_ITER + u
            rows = pl.ds(pl.multiple_of(base + p * SUBLANES, SUBLANES), SUBLANES)
            qe8 = qe_s[rows, :].astype(BF16)
            kd8 = kd_s[rows, :]
            dp8 = decp_s[rows, :]
            v8 = v_s[rows, :]
            o8 = o_s[hd, rows, :]
            for half in range(2):
                mine = (row >= seq) if half else (row < seq)
                s0 = s0_ref[2 * p + half]
                o8 = o8 + jnp.where(mine, _dot(qe8, s0.astype(BF16)), 0.0)
                lhs = jnp.concatenate([jnp.where(mine, kd8, dp8), pad], axis=0).astype(BF16)
                ones = jnp.where(mine, 0.0, 1.0)
                rhs = jnp.concatenate(
                    [jnp.concatenate([jnp.where(mine, v8, 0.0), ones], axis=1), pad2], axis=0).astype(BF16)
                upd = _dot_tn(lhs, rhs)
                s1_ref[2 * p + half] = upd[:, HEAD_DIM:] * s0 + upd[:, :HEAD_DIM]
            o_s[hd, rows, :] = o8
        return carry

    lax.fori_loop(0, nseq // (2 * SAMPLE_PAIRS_PER_ITER), pairs_body, 0)

    @pl.when(jnp.logical_and(hd == heads - 1, blk == pl.num_programs(1) - 1))
    def _():
        gon = gon_ref[...]
        gated = jnp.concatenate(
            [(_rms(o_s[i], gon) * gate_s[i]).astype(BF16) for i in range(heads)], axis=1)
        y = _dot(gated, wout_ref[...].astype(BF16))
        y_ref[...] = x_ref[...] + _rms(y, gpost_ref[...])


def _const_spec(shape):
    nd = len(shape)
    return pl.BlockSpec(shape, lambda *_: (0,) * nd)


def _layer_spec(arr, layer):
    nd = arr.ndim - 1
    return pl.BlockSpec((None,) + arr.shape[1:], lambda *_: (layer,) + (0,) * nd)


def _params(sem):
    return pltpu.CompilerParams(dimension_semantics=sem, vmem_limit_bytes=VMEM_LIMIT_BYTES)


def _conv_layer(i, j, x, xf, seq, state, gpre, win, cw, wout, gpost):
    bsz, t, d = x.shape
    n = xf.shape[0]
    nb = n // seq
    e = wout.shape[1]
    tm = CONV_PROMPT_TILE
    assert t % tm == 0 and (tm // CONV_ROW_SPLITS) % SUBLANES == 0
    assert CONV_W == 3 and seq & (seq - 1) == 0 and seq >= CONV_W - 1
    nt = t // tm
    n_tiles = bsz * nt

    def tile(g):
        g = jnp.minimum(g, n_tiles - 1)
        return g // nt, g % nt

    y, buf, ys, new_sbuf = pl.pallas_call(
        functools.partial(_conv_kernel, nt, seq),
        grid=(n_tiles + 1,),
        in_specs=[pl.BlockSpec((1, tm, d), lambda g: tile(g) + (0,)),
                  _const_spec(xf.shape), _layer_spec(state, j),
                  _layer_spec(gpre, i), _layer_spec(win, i), _layer_spec(cw, j),
                  _layer_spec(wout, i), _layer_spec(gpost, i)],
        out_specs=[pl.BlockSpec((1, tm, d), lambda g: tile(g) + (0,)),
                   pl.BlockSpec((1, CONV_W - 1, e), lambda g: (tile(g)[0], 0, 0)),
                   _const_spec((n, d)), _const_spec((nb, (CONV_W - 1) * e))],
        out_shape=[jax.ShapeDtypeStruct(x.shape, F32),
                   jax.ShapeDtypeStruct((bsz, CONV_W - 1, e), F32),
                   jax.ShapeDtypeStruct((n, d), F32),
                   jax.ShapeDtypeStruct((nb, (CONV_W - 1) * e), F32)],
        scratch_shapes=[pltpu.VMEM((SUBLANES, e), F32),
                        pltpu.VMEM(win.shape[1:], BF16), pltpu.VMEM(wout.shape[1:], BF16),
                        pltpu.VMEM((e // LANES, n, LANES), F32),
                        pltpu.VMEM((e // LANES, n, LANES), F32)],
        compiler_params=pltpu.CompilerParams(dimension_semantics=("arbitrary",),
                                             vmem_limit_bytes=CONV_VMEM_LIMIT_BYTES),
        name="conv_layer",
    )(x, xf, state, gpre, win, cw, wout, gpost)
    return y, buf, ys, new_sbuf.reshape(nb, CONV_W - 1, e)


def _hgrn_prompt(i, j, x, gpre, win, lb_logits, gon, wout, gpost):
    bsz, t, d = x.shape
    e = wout.shape[1]
    heads = e // HEAD_DIM
    tm = PROMPT_TILE
    assert t % tm == 0 and tm % HGRN_CHUNK == 0
    nt = t // tm
    n_tiles = bsz * nt
    n_bf16 = 5 + len(_level_sizes(HGRN_CHUNK))
    scratch = ([pltpu.VMEM(win.shape[1:], BF16), pltpu.VMEM(wout.shape[1:], BF16),
                pltpu.VMEM((heads, HEAD_DIM, HEAD_DIM), F32),
                pltpu.VMEM((tm // HGRN_CHUNK, SUBLANES, e), F32),
                pltpu.VMEM((tm, 4 * e), F32),
                pltpu.VMEM((tm, e), F32)]
               + [pltpu.VMEM((tm, e), BF16) for _ in range(n_bf16)])

    def projected(g):
        tile = jnp.minimum(g, n_tiles - 1)
        return tile // nt, tile % nt

    def finished(g):
        tile = jnp.maximum(g - 2, 0)
        return tile // nt, tile % nt

    return pl.pallas_call(
        functools.partial(_hgrn_prompt_kernel, j, nt),
        grid=(n_tiles + 2,),
        in_specs=[pl.BlockSpec((1, tm, d), lambda g: projected(g) + (0,)),
                  pl.BlockSpec((1, tm, d), lambda g: finished(g) + (0,)),
                  _layer_spec(gpre, i), _layer_spec(win, i), _const_spec(lb_logits.shape),
                  _layer_spec(gon, j), _layer_spec(wout, i), _layer_spec(gpost, i)],
        out_specs=[pl.BlockSpec((1, tm, d), lambda g: finished(g) + (0,)),
                   pl.BlockSpec((1, heads, HEAD_DIM, HEAD_DIM), lambda g: (finished(g)[0], 0, 0, 0))],
        out_shape=[jax.ShapeDtypeStruct(x.shape, F32),
                   jax.ShapeDtypeStruct((bsz, heads, HEAD_DIM, HEAD_DIM), F32)],
        scratch_shapes=scratch,
        compiler_params=_params(("arbitrary",)),
        name="hgrn_prompt",
    )(x, x, gpre, win, lb_logits, gon, wout, gpost)


def _hgrn_sample(i, j, xf, seq, state, states_out, gpre, win, lb_logits, gon, wout, gpost):
    n, d = xf.shape
    nb = n // seq
    e = wout.shape[1]
    heads = e // HEAD_DIM
    g = SAMPLE_SEQS_PER_STEP
    assert 2 * seq == SUBLANES and nb % g == 0
    assert g % (2 * SAMPLE_PAIRS_PER_ITER) == 0
    assert seq // 2 * -LOG_F_FLOOR * LOG2_E < F32_MAX_EXP2

    def head_cols(part):
        return pl.BlockSpec((None, d, HEAD_DIM), lambda hd, blk: (i, 0, part * heads + hd))

    st_spec = pl.BlockSpec((None, g, None, HEAD_DIM, HEAD_DIM), lambda hd, blk: (j, blk, hd, 0, 0))
    args = [xf, gpre, win, win, win, win, lb_logits, gon, wout, gpost, state]
    in_specs = [_const_spec(xf.shape), _layer_spec(gpre, i)] + [head_cols(part) for part in range(4)] + [
        pl.BlockSpec((lb_logits.shape[0], HEAD_DIM), lambda hd, blk: (0, hd)),
        _layer_spec(gon, j), _layer_spec(wout, i), _layer_spec(gpost, i), st_spec]
    aliases = {}
    if states_out is not None:
        args.append(states_out)
        in_specs.append(pl.BlockSpec(memory_space=pl.ANY))
        aliases = {len(args) - 1: 1}
    y, states_out = pl.pallas_call(
        functools.partial(_hgrn_sample_kernel, j, seq),
        grid=(heads, nb // g),
        in_specs=in_specs,
        out_specs=[_const_spec((n, d)), st_spec],
        out_shape=[jax.ShapeDtypeStruct((n, d), F32), jax.ShapeDtypeStruct(state.shape, F32)],
        scratch_shapes=[pltpu.VMEM((n, d), BF16)]
                       + [pltpu.VMEM((n, HEAD_DIM), F32) for _ in range(4)]
                       + [pltpu.VMEM((heads, n, HEAD_DIM), F32) for _ in range(2)],
        input_output_aliases=aliases,
        compiler_params=_params(("arbitrary", "arbitrary")),
        name="hgrn_sample",
    )(*args)
    return y, states_out


def kernel(x_prompt, x_sample, state_conv, state_hgrn, norm_pre, w_in, conv_w, hgrn_lb_logits,
           hgrn_onorm, w_out, norm_post):
    depth = w_in.shape[0]
    w_in32, w_out32, lb_logits = w_in, w_out, hgrn_lb_logits
    gpre, gpost, gon = norm_pre[:, None, :], norm_post[:, None, :], hgrn_onorm[:, None, :]
    conv_rows = state_conv.reshape(state_conv.shape[0], state_conv.shape[1], -1)
    nb, seq, d = x_sample.shape
    xp, xs = x_prompt, x_sample.reshape(nb * seq, d)
    conv_p, conv_s, hgrn_p, hgrn_s = [], [], [], None
    for i in range(depth):
        j = i // 2
        if i % 2 == 0:
            xp, bp, xs, bs = _conv_layer(i, j, xp, xs, seq, conv_rows, gpre, w_in32, conv_w, w_out32, gpost)
            conv_p.append(bp)
            conv_s.append(bs)
        else:
            xp, sp = _hgrn_prompt(i, j, xp, gpre, w_in32, lb_logits, gon, w_out32, gpost)
            xs, hgrn_s = _hgrn_sample(i, j, xs, seq, state_hgrn, hgrn_s, gpre, w_in32, lb_logits, gon,
                                      w_out32, gpost)
            hgrn_p.append(sp)
    return (xp, xs.reshape(nb, seq, d), jnp.stack(conv_p), jnp.stack(conv_s), jnp.stack(hgrn_p), hgrn_s)
```

```python
import functools
import math

import jax
import jax.numpy as jnp
from jax import lax
from jax.experimental import pallas as pl
from jax.experimental.pallas import tpu as pltpu

F32 = jnp.float32
BF16 = jnp.bfloat16

RMS_EPS = 1e-6
LOG_F_FLOOR = -20.0
LOG2_E = math.log2(math.e)
HEAD_DIM = 128
LANES = 128
SUBLANES = 8
MXU_WIDTH = 256
F32_MAX_EXP2 = 127
CONV_W = 3
HGRN_CHUNK = 128
BASE_BLOCK = SUBLANES
assert BASE_BLOCK // 2 * -LOG_F_FLOOR * LOG2_E < F32_MAX_EXP2
SCORE_LOOKAHEAD = 2
PROMPT_TILE = 256
CONV_PROMPT_TILE = 512
CONV_ROW_SPLITS = 2
SAMPLE_SEQS_PER_STEP = 128
SAMPLE_PAIRS_PER_ITER = 8
VMEM_LIMIT_BYTES = 56 * 1024 * 1024
CONV_VMEM_LIMIT_BYTES = 60 * 1024 * 1024


def _rms(x, g):
    ms = jnp.mean(x * x, axis=-1, keepdims=True)
    return x * lax.rsqrt(ms + RMS_EPS) * g


def _silu(x):
    return x * (1.0 / (1.0 + jnp.exp2(x * (-LOG2_E))))


def _dot(a, b):
    return jnp.dot(a, b, preferred_element_type=F32)


def _dot_nt(a, b):
    return lax.dot_general(a, b, (((1,), (1,)), ((), ())), preferred_element_type=F32)


def _dot_tn(a, b):
    return lax.dot_general(a, b, (((0,), (0,)), ((), ())), preferred_element_type=F32)


def _lower_bound(logits, j):
    m = jnp.max(logits, axis=0, keepdims=True)
    e = jnp.exp(logits - m)
    p = e / jnp.sum(e, axis=0, keepdims=True)
    if j == 0:
        return jnp.zeros_like(p[0:1])
    return jnp.sum(p[1:j + 1], axis=0, keepdims=True)


def _forget_gate(fpre, lb):
    e = jnp.exp2(jnp.abs(fpre) * (-LOG2_E))
    r = 1.0 / (1.0 + e)
    er = e * r
    pos = fpre >= 0
    sig = jnp.where(pos, r, er)
    nsig = jnp.where(pos, er, r)
    f = lb + (1.0 - lb) * sig
    log2_f = jnp.maximum(jnp.log(f) * LOG2_E, LOG_F_FLOOR * LOG2_E)
    k = jnp.minimum((1.0 - lb) * nsig, 1.0 - math.exp(LOG_F_FLOOR))
    return log2_f, k


def _cumsum_rows(x, group):
    n, width = x.shape
    assert group <= SUBLANES and n % SUBLANES == 0
    pos = lax.broadcasted_iota(jnp.int32, (SUBLANES, width), 0) & (group - 1)
    s = 1
    while s < group:
        keep = jnp.where(pos >= s, 1.0, 0.0)
        shifted = pltpu.roll(x, s, axis=0).reshape(n // SUBLANES, SUBLANES, width) * keep
        x = x + shifted.reshape(n, width)
        s *= 2
    return x


def _group_allsum(x, group):
    n = x.shape[0]
    pos = lax.broadcasted_iota(jnp.int32, x.shape, 0) & (group - 1)
    s = 1
    while s < group:
        partner = jnp.where((pos & s) == 0, pltpu.roll(x, n - s, axis=0), pltpu.roll(x, s, axis=0))
        x = x + partner
        s *= 2
    return x


def _level_matrix(n):
    t = lax.broadcasted_iota(jnp.int32, (n, n), 0)
    s = lax.broadcasted_iota(jnp.int32, (n, n), 1)
    sh = BASE_BLOCK.bit_length() - 1
    lv = jnp.where(((t >> sh) == (s >> sh)) & (t >= s), 1, 0)
    c, i = 2 * BASE_BLOCK, 2
    while c <= n:
        half, sh = c // 2, sh + 1
        own = ((t >> sh) == (s >> sh)) & ((t & half) != 0) & ((s & half) == 0)
        lv = jnp.where(own, i, lv)
        c, i = 2 * c, i + 1
    return lv


def _level_sizes(chunk):
    sizes, c = [], 2 * BASE_BLOCK
    while c <= chunk:
        sizes.append(c)
        c *= 2
    return sizes


def _chunk_operands(q, k, lf2):
    c = q.shape[0]
    local = _cumsum_rows(lf2, BASE_BLOCK)
    b_parts, x_parts, carry = [], [], None
    for g in range(c // BASE_BLOCK):
        blk = local[g * BASE_BLOCK:(g + 1) * BASE_BLOCK]
        x_parts.append(blk - blk[BASE_BLOCK // 2 - 1:BASE_BLOCK // 2])
        if carry is not None:
            blk = blk + carry
        b_parts.append(blk)
        carry = blk[BASE_BLOCK - 1:BASE_BLOCK]
    b = jnp.concatenate(b_parts, axis=0)
    x = jnp.concatenate(x_parts, axis=0)
    b_last = carry
    qe = (q * jnp.exp2(b)).astype(BF16)
    kd = (k * jnp.exp2(b_last - b)).astype(BF16)
    q8 = (q * jnp.exp2(x)).astype(BF16)
    k8 = (k * jnp.exp2(-x)).astype(BF16)
    w = []
    for size in _level_sizes(c):
        half = size // 2
        parts = []
        for s0 in range(0, c, size):
            r = b[s0 + half - 1:s0 + half]
            lo, hi = slice(s0, s0 + half), slice(s0 + half, s0 + size)
            parts.append(k[lo] * jnp.exp2(r - b[lo]))
            parts.append(q[hi] * jnp.exp2(b[hi] - r))
        w.append(jnp.concatenate(parts, axis=0).astype(BF16))
    return qe, kd, q8, k8, w, jnp.exp2(b_last)


def _conv_project(h, win_ref):
    e = win_ref.shape[1] // 4
    return tuple(_dot(h, win_ref[:, part * e:(part + 1) * e]) for part in range(4))


def _conv_gate(proj, cw_ref, prev1, prev2, pos):
    v, bg, cg, z = proj
    u = cg * v
    u1 = jnp.where(pos >= 1, pltpu.roll(u, 1, axis=0), prev1)
    u2 = jnp.where(pos >= 2, pltpu.roll(u, 2, axis=0), prev2)
    conv = cw_ref[0:1, :] * u2 + cw_ref[1:2, :] * u1 + cw_ref[2:3, :] * u
    return bg * conv * _silu(z), u


def _conv_kernel(nt, seq, x_ref, xs_ref, sbuf_ref, gpre_ref, win_in_ref, cw_ref, wout_in_ref, gpost_ref,
                 y_ref, buf_ref, ys_ref, new_sbuf_ref, tail_ref, win_ref, wout_ref, p1_s, p2_s):
    g = pl.program_id(0)
    n_tiles = pl.num_programs(0) - 1
    e = win_ref.shape[1] // 4

    @pl.when(g == 0)
    def _():
        win_ref[...] = win_in_ref[...].astype(BF16)
        wout_ref[...] = wout_in_ref[...].astype(BF16)

    @pl.when(lax.rem(g, nt) == 0)
    def _():
        tail_ref[...] = jnp.zeros_like(tail_ref)

    @pl.when(g < n_tiles)
    def _():
        tm = x_ref.shape[1]
        rows = tm // CONV_ROW_SPLITS
        pos = lax.broadcasted_iota(jnp.int32, (rows, e), 0)
        tail = tail_ref[...]
        xs = [x_ref[0, k * rows:(k + 1) * rows, :] for k in range(CONV_ROW_SPLITS)]
        proj = _conv_project(_rms(xs[0], gpre_ref[...]).astype(BF16), win_ref)
        for k in range(CONV_ROW_SPLITS):
            if k + 1 < CONV_ROW_SPLITS:
                next_proj = _conv_project(_rms(xs[k + 1], gpre_ref[...]).astype(BF16), win_ref)
            prev1 = tail[SUBLANES - 1:SUBLANES, :]
            prev2 = jnp.where(pos == 0, tail[SUBLANES - 2:SUBLANES - 1, :], prev1)
            gated, u = _conv_gate(proj, cw_ref, prev1, prev2, pos)
            y = _dot(gated.astype(BF16), wout_ref[...])
            y_ref[0, k * rows:(k + 1) * rows, :] = xs[k] + _rms(y, gpost_ref[...])
            tail = u[rows - SUBLANES:rows, :]
            if k + 1 < CONV_ROW_SPLITS:
                proj = next_proj
        tail_ref[...] = tail
        buf_ref[0] = tail[SUBLANES - (CONV_W - 1):SUBLANES, :]

    @pl.when(g == n_tiles)
    def _():
        x = xs_ref[...]
        nb = sbuf_ref.shape[0]
        lane_tiles, n, lanes = p1_s.shape
        h = _rms(x, gpre_ref[...]).astype(BF16)
        pos = lax.broadcasted_iota(jnp.int32, (n, e), 0) & (seq - 1)
        p1_s[...] = jnp.zeros_like(p1_s)
        p2_s[...] = jnp.zeros_like(p2_s)
        for c in range(lane_tiles):
            buf0 = sbuf_ref[:, c * lanes:(c + 1) * lanes]
            buf1 = sbuf_ref[:, e + c * lanes:e + (c + 1) * lanes]
            p1_s[c, pl.ds(0, nb, stride=seq), :] = buf1
            p2_s[c, pl.ds(0, nb, stride=seq), :] = buf0
            p2_s[c, pl.ds(1, nb, stride=seq), :] = buf1
        prev1 = jnp.concatenate([p1_s[c] for c in range(lane_tiles)], axis=1)
        prev2 = jnp.concatenate([p2_s[c] for c in range(lane_tiles)], axis=1)
        gated, u = _conv_gate(_conv_project(h, win_ref), cw_ref, prev1, prev2, pos)
        y = _dot(gated.astype(BF16), wout_ref[...])
        ys_ref[...] = x + _rms(y, gpost_ref[...])
        for c in range(lane_tiles):
            p1_s[c] = u[:, c * lanes:(c + 1) * lanes]
            for r in range(CONV_W - 1):
                new_sbuf_ref[:, r * e + c * lanes:r * e + (c + 1) * lanes] = (
                    p1_s[c, pl.ds(seq - (CONV_W - 1) + r, nb, stride=seq), :])


def _hgrn_prompt_kernel(j, nt, xp_ref, xr_ref, gpre_ref, win_in_ref, lbl_ref, gon_ref, wout_in_ref, gpost_ref,
                        y_ref, sout_ref, win_ref, wout_ref, st_s, dec_s, raw_s, gate_s,
                        qe_s, kd_s, q8_s, k8_s, v_s, *w_s):
    g = pl.program_id(0)
    e = win_ref.shape[1] // 4
    heads = e // HEAD_DIM
    tm = xp_ref.shape[1]
    fin = g - 2

    @pl.when(g == 0)
    def _():
        win_ref[...] = win_in_ref[...].astype(BF16)
        wout_ref[...] = wout_in_ref[...].astype(BF16)
        for ref in (dec_s, raw_s, gate_s, qe_s, kd_s, q8_s, k8_s, v_s) + tuple(w_s):
            ref[...] = jnp.zeros_like(ref)

    @pl.when(jnp.logical_or(g < 2, lax.rem(fin, nt) == 0))
    def _():
        st_s[...] = jnp.zeros_like(st_s)

    lv = _level_matrix(HGRN_CHUNK).astype(F32).astype(BF16)
    owned = [lv == i + 1 for i in range(len(w_s) + 1)]
    gon = gon_ref[...]
    chunks = tm // HGRN_CHUNK

    def chunk_scores(c, hd):
        rows = slice(c * HGRN_CHUNK, (c + 1) * HGRN_CHUNK)
        cols = slice(hd * HEAD_DIM, (hd + 1) * HEAD_DIM)
        a = jnp.where(owned[0], _dot_nt(q8_s[rows, cols], k8_s[rows, cols]).astype(BF16), 0.0)
        for i, w_ref in enumerate(w_s):
            wv = w_ref[rows, cols]
            a = jnp.where(owned[i + 1], _dot_nt(wv, wv).astype(BF16), a)
        return a

    def chunk_output(c, hd, a):
        rows = slice(c * HGRN_CHUNK, (c + 1) * HGRN_CHUNK)
        cols = slice(hd * HEAD_DIM, (hd + 1) * HEAD_DIM)
        vb = v_s[rows, cols]
        st = st_s[hd]
        o = _dot(a, vb) + _dot_nt(qe_s[rows, cols], st.astype(BF16))
        st3 = st.reshape(HEAD_DIM // SUBLANES, SUBLANES, HEAD_DIM) * dec_s[c, :, cols]
        st_s[hd] = st3.reshape(HEAD_DIM, HEAD_DIM) + _dot_tn(vb, kd_s[rows, cols])
        return (_rms(o, gon) * gate_s[rows, cols]).astype(BF16)

    groups = e // MXU_WIDTH
    heads_per_group = MXU_WIDTH // HEAD_DIM

    def part_cols(cg):
        return [slice(part * e + cg * MXU_WIDTH, part * e + (cg + 1) * MXU_WIDTH) for part in range(4)]

    def stage(lb, cg, c, q_raw, f_raw, v_raw, z_raw):
        gc = slice(cg * MXU_WIDTH, (cg + 1) * MXU_WIDTH)
        rows = slice(c * HGRN_CHUNK, (c + 1) * HGRN_CHUNK)
        q = _silu(q_raw) * (HEAD_DIM ** -0.5)
        lf2, k = _forget_gate(f_raw, lb[:, gc])
        v_s[rows, gc] = v_raw.astype(BF16)
        gate_s[rows, gc] = _silu(z_raw)
        qe, kd, q8, k8, w, decay = _chunk_operands(q, k, lf2)
        qe_s[rows, gc] = qe
        kd_s[rows, gc] = kd
        q8_s[rows, gc] = q8
        k8_s[rows, gc] = k8
        for w_ref, w_val in zip(w_s, w):
            w_ref[rows, gc] = w_val
        dec_s[c, :, gc] = jnp.broadcast_to(decay, (SUBLANES, MXU_WIDTH))

    def step(do_project, do_stage, do_finish):
        out_blocks = [[None] * heads for _ in range(chunks)]
        projections, raws, scores = [], {}, {}
        if do_project:
            h = _rms(xp_ref[0], gpre_ref[...]).astype(BF16)
            projections = [cols for cg in range(groups) for cols in part_cols(cg)]
        if do_stage:
            lb = _lower_bound(lbl_ref[...], j)
            for cg in range(groups):
                for c in range(chunks):
                    rows = slice(c * HGRN_CHUNK, (c + 1) * HGRN_CHUNK)
                    raws[cg, c] = [raw_s[rows, cols] for cols in part_cols(cg)]
        units = [(cg, c, hd) for cg in range(groups) for c in range(chunks)
                 for hd in range(cg * heads_per_group, (cg + 1) * heads_per_group)]
        for n in range(len(units) + SCORE_LOOKAHEAD):
            if do_finish and n < len(units):
                _, c, hd = units[n]
                scores[c, hd] = chunk_scores(c, hd)
            if projections:
                cols = projections.pop(0)
                raw_s[:, cols] = _dot(h, win_ref[:, cols])
            if n >= SCORE_LOOKAHEAD:
                done = n - SCORE_LOOKAHEAD
                cg, c, hd = units[done]
                if do_finish:
                    out_blocks[c][hd] = chunk_output(c, hd, scores.pop((c, hd)))
                if do_stage and (done + 1) % heads_per_group == 0:
                    stage(lb, cg, c, *raws.pop((cg, c)))
        for cols in projections:
            raw_s[:, cols] = _dot(h, win_ref[:, cols])
        if do_finish:
            gated = jnp.concatenate([jnp.concatenate(blocks, axis=1) for blocks in out_blocks], axis=0)
            y = _dot(gated, wout_ref[...])
            y_ref[0] = xr_ref[0] + _rms(y, gpost_ref[...])

    last = pl.num_programs(0) - 1
    pl.when(g == 0)(functools.partial(step, True, False, False))
    pl.when(jnp.logical_and(g > 0, g < last))(functools.partial(step, True, True, True))
    pl.when(g == last)(functools.partial(step, False, False, True))

    @pl.when(jnp.logical_and(g >= 2, lax.rem(fin, nt) == nt - 1))
    def _():
        for hd in range(heads):
            sout_ref[0, hd] = st_s[hd].T


def _hgrn_sample_kernel(j, seq, x_ref, gpre_ref, wq_ref, wf_ref, wv_ref, wz_ref, lbl_ref, gon_ref,
                        wout_ref, gpost_ref, s0_ref, *rest):
    y_ref, s1_ref, h_s, qe_s, kd_s, decp_s, v_s, o_s, gate_s = rest[-9:]
    hd, blk = pl.program_id(0), pl.program_id(1)
    n = x_ref.shape[0]
    nseq = s0_ref.shape[0]
    heads = o_s.shape[0]

    @pl.when(jnp.logical_and(hd == 0, blk == 0))
    def _():
        h_s[...] = _rms(x_ref[...], gpre_ref[...]).astype(BF16)

    @pl.when(blk == 0)
    def _():
        h = h_s[...]
        lb = _lower_bound(lbl_ref[...], j)
        q = _silu(_dot(h, wq_ref[...].astype(BF16))) * (HEAD_DIM ** -0.5)
        lf2, k = _forget_gate(_dot(h, wf_ref[...].astype(BF16)), lb)
        v = _dot(h, wv_ref[...].astype(BF16))
        gate_s[hd] = _silu(_dot(h, wz_ref[...].astype(BF16)))
        t = lax.broadcasted_iota(jnp.int32, (n, n), 0)
        s = lax.broadcasted_iota(jnp.int32, (n, n), 1)
        sh = seq.bit_length() - 1
        same_seq_causal = ((t >> sh) == (s >> sh)) & (t >= s)
        row = lax.broadcasted_iota(jnp.int32, (n, HEAD_DIM), 0)
        pos = row & (seq - 1)
        b = _cumsum_rows(lf2, seq)
        b_last = _group_allsum(lf2, seq)
        x_mid = b - _group_allsum(jnp.where(pos < seq // 2, lf2, 0.0), seq)
        a = _dot_nt((q * jnp.exp2(x_mid)).astype(BF16), (k * jnp.exp2(-x_mid)).astype(BF16))
        a = jnp.where(same_seq_causal, a, 0.0)
        o_s[hd] = _dot(a.astype(BF16), v.astype(BF16))
        qe_s[...] = q * jnp.exp2(b)
        kd_s[...] = k * jnp.exp2(b_last - b)
        v_s[...] = v
        dec = jnp.exp2(b_last)
        dsw = jnp.where((row & (SUBLANES - 1)) < seq,
                        pltpu.roll(dec, n - seq, axis=0), pltpu.roll(dec, seq, axis=0))
        hi = dsw.astype(BF16).astype(F32)
        mid = (dsw - hi).astype(BF16).astype(F32)
        lo = (dsw - hi - mid).astype(BF16).astype(F32)
        decp_s[...] = jnp.where(pos == 0, hi, jnp.where(pos == 1, mid, jnp.where(pos == 2, lo, 0.0)))

    row = lax.broadcasted_iota(jnp.int32, (SUBLANES, HEAD_DIM), 0)
    pad = jnp.zeros((HEAD_DIM - SUBLANES, HEAD_DIM), F32)
    pad2 = jnp.zeros((HEAD_DIM - SUBLANES, 2 * HEAD_DIM), F32)
    base = blk * (nseq * seq)

    def pairs_body(it, carry):
        for u in range(SAMPLE_PAIRS_PER_ITER):
            p = it * SAMPLE_PAIRS_PER_ITER + u
            rows = pl.ds(pl.multiple_of(base + p * SUBLANES, SUBLANES), SUBLANES)
            qe8 = qe_s[rows, :].astype(BF16)
            kd8 = kd_s[rows, :]
            dp8 = decp_s[rows, :]
            v8 = v_s[rows, :]
            o8 = o_s[hd, rows, :]
            for half in range(2):
                mine = (row >= seq) if half else (row < seq)
                s0 = s0_ref[2 * p + half]
                o8 = o8 + jnp.where(mine, _dot(qe8, s0.astype(BF16)), 0.0)
                lhs = jnp.concatenate([jnp.where(mine, kd8, dp8), pad], axis=0).astype(BF16)
                ones = jnp.where(mine, 0.0, 1.0)
                rhs = jnp.concatenate(
                    [jnp.concatenate([jnp.where(mine, v8, 0.0), ones], axis=1), pad2], axis=0).astype(BF16)
                upd = _dot_tn(lhs, rhs)
                s1_ref[2 * p + half] = upd[:, HEAD_DIM:] * s0 + upd[:, :HEAD_DIM]
            o_s[hd, rows, :] = o8
        return carry

    lax.fori_loop(0, nseq // (2 * SAMPLE_PAIRS_PER_ITER), pairs_body, 0)

    @pl.when(jnp.logical_and(hd == heads - 1, blk == pl.num_programs(1) - 1))
    def _():
        gon = gon_ref[...]
        gated = jnp.concatenate(
            [(_rms(o_s[i], gon) * gate_s[i]).astype(BF16) for i in range(heads)], axis=1)
        y = _dot(gated, wout_ref[...].astype(BF16))
        y_ref[...] = x_ref[...] + _rms(y, gpost_ref[...])


def _const_spec(shape):
    nd = len(shape)
    return pl.BlockSpec(shape, lambda *_: (0,) * nd)


def _layer_spec(arr, layer):
    nd = arr.ndim - 1
    return pl.BlockSpec((None,) + arr.shape[1:], lambda *_: (layer,) + (0,) * nd)


def _params(sem):
    return pltpu.CompilerParams(dimension_semantics=sem, vmem_limit_bytes=VMEM_LIMIT_BYTES)


def _conv_layer(i, j, x, xf, seq, state, gpre, win, cw, wout, gpost):
    bsz, t, d = x.shape
    n = xf.shape[0]
    nb = n // seq
    e = wout.shape[1]
    tm = CONV_PROMPT_TILE
    assert t % tm == 0 and (tm // CONV_ROW_SPLITS) % SUBLANES == 0
    assert CONV_W == 3 and seq & (seq - 1) == 0 and seq >= CONV_W - 1
    nt = t // tm
    n_tiles = bsz * nt

    def tile(g):
        g = jnp.minimum(g, n_tiles - 1)
        return g // nt, g % nt

    y, buf, ys, new_sbuf = pl.pallas_call(
        functools.partial(_conv_kernel, nt, seq),
        grid=(n_tiles + 1,),
        in_specs=[pl.BlockSpec((1, tm, d), lambda g: tile(g) + (0,)),
                  _const_spec(xf.shape), _layer_spec(state, j),
                  _layer_spec(gpre, i), _layer_spec(win, i), _layer_spec(cw, j),
                  _layer_spec(wout, i), _layer_spec(gpost, i)],
        out_specs=[pl.BlockSpec((1, tm, d), lambda g: tile(g) + (0,)),
                   pl.BlockSpec((1, CONV_W - 1, e), lambda g: (tile(g)[0], 0, 0)),
                   _const_spec((n, d)), _const_spec((nb, (CONV_W - 1) * e))],
        out_shape=[jax.ShapeDtypeStruct(x.shape, F32),
                   jax.ShapeDtypeStruct((bsz, CONV_W - 1, e), F32),
                   jax.ShapeDtypeStruct((n, d), F32),
                   jax.ShapeDtypeStruct((nb, (CONV_W - 1) * e), F32)],
        scratch_shapes=[pltpu.VMEM((SUBLANES, e), F32),
                        pltpu.VMEM(win.shape[1:], BF16), pltpu.VMEM(wout.shape[1:], BF16),
                        pltpu.VMEM((e // LANES, n, LANES), F32),
                        pltpu.VMEM((e // LANES, n, LANES), F32)],
        compiler_params=pltpu.CompilerParams(dimension_semantics=("arbitrary",),
                                             vmem_limit_bytes=CONV_VMEM_LIMIT_BYTES),
        name="conv_layer",
    )(x, xf, state, gpre, win, cw, wout, gpost)
    return y, buf, ys, new_sbuf.reshape(nb, CONV_W - 1, e)


def _hgrn_prompt(i, j, x, gpre, win, lb_logits, gon, wout, gpost):
    bsz, t, d = x.shape
    e = wout.shape[1]
    heads = e // HEAD_DIM
    tm = PROMPT_TILE
    assert t % tm == 0 and tm % HGRN_CHUNK == 0
    nt = t // tm
    n_tiles = bsz * nt
    n_bf16 = 5 + len(_level_sizes(HGRN_CHUNK))
    scratch = ([pltpu.VMEM(win.shape[1:], BF16), pltpu.VMEM(wout.shape[1:], BF16),
                pltpu.VMEM((heads, HEAD_DIM, HEAD_DIM), F32),
                pltpu.VMEM((tm // HGRN_CHUNK, SUBLANES, e), F32),
                pltpu.VMEM((tm, 4 * e), F32),
                pltpu.VMEM((tm, e), F32)]
               + [pltpu.VMEM((tm, e), BF16) for _ in range(n_bf16)])

    def projected(g):
        tile = jnp.minimum(g, n_tiles - 1)
        return tile // nt, tile % nt

    def finished(g):
        tile = jnp.maximum(g - 2, 0)
        return tile // nt, tile % nt

    return pl.pallas_call(
        functools.partial(_hgrn_prompt_kernel, j, nt),
        grid=(n_tiles + 2,),
        in_specs=[pl.BlockSpec((1, tm, d), lambda g: projected(g) + (0,)),
                  pl.BlockSpec((1, tm, d), lambda g: finished(g) + (0,)),
                  _layer_spec(gpre, i), _layer_spec(win, i), _const_spec(lb_logits.shape),
                  _layer_spec(gon, j), _layer_spec(wout, i), _layer_spec(gpost, i)],
        out_specs=[pl.BlockSpec((1, tm, d), lambda g: finished(g) + (0,)),
                   pl.BlockSpec((1, heads, HEAD_DIM, HEAD_DIM), lambda g: (finished(g)[0], 0, 0, 0))],
        out_shape=[jax.ShapeDtypeStruct(x.shape, F32),
                   jax.ShapeDtypeStruct((bsz, heads, HEAD_DIM, HEAD_DIM), F32)],
        scratch_shapes=scratch,
        compiler_params=_params(("arbitrary",)),
        name="hgrn_prompt",
    )(x, x, gpre, win, lb_logits, gon, wout, gpost)


def _hgrn_sample(i, j, xf, seq, state, states_out, gpre, win, lb_logits, gon, wout, gpost):
    n, d = xf.shape
    nb = n // seq
    e = wout.shape[1]
    heads = e // HEAD_DIM
    g = SAMPLE_SEQS_PER_STEP
    assert 2 * seq == SUBLANES and nb % g == 0
    assert g % (2 * SAMPLE_PAIRS_PER_ITER) == 0
    assert seq // 2 * -LOG_F_FLOOR * LOG2_E < F32_MAX_EXP2

    def head_cols(part):
        return pl.BlockSpec((None, d, HEAD_DIM), lambda hd, blk: (i, 0, part * heads + hd))

    st_spec = pl.BlockSpec((None, g, None, HEAD_DIM, HEAD_DIM), lambda hd, blk: (j, blk, hd, 0, 0))
    args = [xf, gpre, win, win, win, win, lb_logits, gon, wout, gpost, state]
    in_specs = [_const_spec(xf.shape), _layer_spec(gpre, i)] + [head_cols(part) for part in range(4)] + [
        pl.BlockSpec((lb_logits.shape[0], HEAD_DIM), lambda hd, blk: (0, hd)),
        _layer_spec(gon, j), _layer_spec(wout, i), _layer_spec(gpost, i), st_spec]
    aliases = {}
    if states_out is not None:
        args.append(states_out)
        in_specs.append(pl.BlockSpec(memory_space=pl.ANY))
        aliases = {len(args) - 1: 1}
    y, states_out = pl.pallas_call(
        functools.partial(_hgrn_sample_kernel, j, seq),
        grid=(heads, nb // g),
        in_specs=in_specs,
        out_specs=[_const_spec((n, d)), st_spec],
        out_shape=[jax.ShapeDtypeStruct((n, d), F32), jax.ShapeDtypeStruct(state.shape, F32)],
        scratch_shapes=[pltpu.VMEM((n, d), BF16)]
                       + [pltpu.VMEM((n, HEAD_DIM), F32) for _ in range(4)]
                       + [pltpu.VMEM((heads, n, HEAD_DIM), F32) for _ in range(2)],
        input_output_aliases=aliases,
        compiler_params=_params(("arbitrary", "arbitrary")),
        name="hgrn_sample",
    )(*args)
    return y, states_out


def kernel(x_prompt, x_sample, state_conv, state_hgrn, norm_pre, w_in, conv_w, hgrn_lb_logits,
           hgrn_onorm, w_out, norm_post):
    depth = w_in.shape[0]
    w_in32, w_out32, lb_logits = w_in, w_out, hgrn_lb_logits
    gpre, gpost, gon = norm_pre[:, None, :], norm_post[:, None, :], hgrn_onorm[:, None, :]
    conv_rows = state_conv.reshape(state_conv.shape[0], state_conv.shape[1], -1)
    nb, seq, d = x_sample.shape
    xp, xs = x_prompt, x_sample.reshape(nb * seq, d)
    conv_p, conv_s, hgrn_p, hgrn_s = [], [], [], None
    for i in range(depth):
        j = i // 2
        if i % 2 == 0:
            xp, bp, xs, bs = _conv_layer(i, j, xp, xs, seq, conv_rows, gpre, w_in32, conv_w, w_out32, gpost)
            conv_p.append(bp)
            conv_s.append(bs)
        else:
            xp, sp = _hgrn_prompt(i, j, xp, gpre, w_in32, lb_logits, gon, w_out32, gpost)
            xs, hgrn_s = _hgrn_sample(i, j, xs, seq, state_hgrn, hgrn_s, gpre, w_in32, lb_logits, gon,
                                      w_out32, gpost)
            hgrn_p.append(sp)
    return (xp, xs.reshape(nb, seq, d), jnp.stack(conv_p), jnp.stack(conv_s), jnp.stack(hgrn_p), hgrn_s)
```

```python
import functools
import math

import jax
import jax.numpy as jnp
from jax import lax
from jax.experimental import pallas as pl
from jax.experimental.pallas import tpu as pltpu

F32 = jnp.float32
BF16 = jnp.bfloat16

RMS_EPS = 1e-6
LOG_F_FLOOR = -20.0
LOG2_E = math.log2(math.e)
HEAD_DIM = 128
LANES = 128
SUBLANES = 8
MXU_WIDTH = 256
F32_MAX_EXP2 = 127
CONV_W = 3
HGRN_CHUNK = 128
BASE_BLOCK = SUBLANES
assert BASE_BLOCK // 2 * -LOG_F_FLOOR * LOG2_E < F32_MAX_EXP2
SCORE_LOOKAHEAD = 3
PROMPT_TILE = 256
CONV_PROMPT_TILE = 512
CONV_ROW_SPLITS = 2
SAMPLE_SEQS_PER_STEP = 128
SAMPLE_PAIRS_PER_ITER = 8
VMEM_LIMIT_BYTES = 56 * 1024 * 1024
CONV_VMEM_LIMIT_BYTES = 60 * 1024 * 1024


def _rms(x, g):
    ms = jnp.mean(x * x, axis=-1, keepdims=True)
    return x * lax.rsqrt(ms + RMS_EPS) * g


def _silu(x):
    return x * (1.0 / (1.0 + jnp.exp2(x * (-LOG2_E))))


def _dot(a, b):
    return jnp.dot(a, b, preferred_element_type=F32)


def _dot_nt(a, b):
    return lax.dot_general(a, b, (((1,), (1,)), ((), ())), preferred_element_type=F32)


def _dot_tn(a, b):
    return lax.dot_general(a, b, (((0,), (0,)), ((), ())), preferred_element_type=F32)


def _lower_bound(logits, j):
    m = jnp.max(logits, axis=0, keepdims=True)
    e = jnp.exp(logits - m)
    p = e / jnp.sum(e, axis=0, keepdims=True)
    if j == 0:
        return jnp.zeros_like(p[0:1])
    return jnp.sum(p[1:j + 1], axis=0, keepdims=True)


def _forget_gate(fpre, lb):
    e = jnp.exp2(jnp.abs(fpre) * (-LOG2_E))
    r = 1.0 / (1.0 + e)
    er = e * r
    pos = fpre >= 0
    sig = jnp.where(pos, r, er)
    nsig = jnp.where(pos, er, r)
    f = lb + (1.0 - lb) * sig
    log2_f = jnp.maximum(jnp.log(f) * LOG2_E, LOG_F_FLOOR * LOG2_E)
    k = jnp.minimum((1.0 - lb) * nsig, 1.0 - math.exp(LOG_F_FLOOR))
    return log2_f, k


def _cumsum_rows(x, group):
    pos = lax.broadcasted_iota(jnp.int32, x.shape, 0) & (group - 1)
    s = 1
    while s < group:
        x = x + jnp.where(pos >= s, pltpu.roll(x, s, axis=0), 0.0)
        s *= 2
    return x


def _group_allsum(x, group):
    n = x.shape[0]
    pos = lax.broadcasted_iota(jnp.int32, x.shape, 0) & (group - 1)
    s = 1
    while s < group:
        partner = jnp.where((pos & s) == 0, pltpu.roll(x, n - s, axis=0), pltpu.roll(x, s, axis=0))
        x = x + partner
        s *= 2
    return x


def _level_matrix(n):
    t = lax.broadcasted_iota(jnp.int32, (n, n), 0)
    s = lax.broadcasted_iota(jnp.int32, (n, n), 1)
    sh = BASE_BLOCK.bit_length() - 1
    lv = jnp.where(((t >> sh) == (s >> sh)) & (t >= s), 1, 0)
    c, i = 2 * BASE_BLOCK, 2
    while c <= n:
        half, sh = c // 2, sh + 1
        own = ((t >> sh) == (s >> sh)) & ((t & half) != 0) & ((s & half) == 0)
        lv = jnp.where(own, i, lv)
        c, i = 2 * c, i + 1
    return lv


def _level_sizes(chunk):
    sizes, c = [], 2 * BASE_BLOCK
    while c <= chunk:
        sizes.append(c)
        c *= 2
    return sizes


def _chunk_operands(q, k, lf2):
    c = q.shape[0]
    local = _cumsum_rows(lf2, BASE_BLOCK)
    b_parts, x_parts, carry = [], [], None
    for g in range(c // BASE_BLOCK):
        blk = local[g * BASE_BLOCK:(g + 1) * BASE_BLOCK]
        x_parts.append(blk - blk[BASE_BLOCK // 2 - 1:BASE_BLOCK // 2])
        if carry is not None:
            blk = blk + carry
        b_parts.append(blk)
        carry = blk[BASE_BLOCK - 1:BASE_BLOCK]
    b = jnp.concatenate(b_parts, axis=0)
    x = jnp.concatenate(x_parts, axis=0)
    b_last = carry
    qe = (q * jnp.exp2(b)).astype(BF16)
    kd = (k * jnp.exp2(b_last - b)).astype(BF16)
    q8 = (q * jnp.exp2(x)).astype(BF16)
    k8 = (k * jnp.exp2(-x)).astype(BF16)
    w = []
    for size in _level_sizes(c):
        half = size // 2
        parts = []
        for s0 in range(0, c, size):
            r = b[s0 + half - 1:s0 + half]
            lo, hi = slice(s0, s0 + half), slice(s0 + half, s0 + size)
            parts.append(k[lo] * jnp.exp2(r - b[lo]))
            parts.append(q[hi] * jnp.exp2(b[hi] - r))
        w.append(jnp.concatenate(parts, axis=0).astype(BF16))
    return qe, kd, q8, k8, w, jnp.exp2(b_last)


def _conv_project(h, win_ref):
    e = win_ref.shape[1] // 4
    return tuple(_dot(h, win_ref[:, part * e:(part + 1) * e]) for part in range(4))


def _conv_gate(proj, cw_ref, prev1, prev2, pos):
    v, bg, cg, z = proj
    u = cg * v
    u1 = jnp.where(pos >= 1, pltpu.roll(u, 1, axis=0), prev1)
    u2 = jnp.where(pos >= 2, pltpu.roll(u, 2, axis=0), prev2)
    conv = cw_ref[0:1, :] * u2 + cw_ref[1:2, :] * u1 + cw_ref[2:3, :] * u
    return bg * conv * _silu(z), u


def _conv_kernel(nt, seq, x_ref, xs_ref, sbuf_ref, gpre_ref, win_in_ref, cw_ref, wout_in_ref, gpost_ref,
                 y_ref, buf_ref, ys_ref, new_sbuf_ref, tail_ref, win_ref, wout_ref, p1_s, p2_s):
    g = pl.program_id(0)
    n_tiles = pl.num_programs(0) - 1
    e = win_ref.shape[1] // 4

    @pl.when(g == 0)
    def _():
        win_ref[...] = win_in_ref[...].astype(BF16)
        wout_ref[...] = wout_in_ref[...].astype(BF16)

    @pl.when(lax.rem(g, nt) == 0)
    def _():
        tail_ref[...] = jnp.zeros_like(tail_ref)

    @pl.when(g < n_tiles)
    def _():
        tm = x_ref.shape[1]
        rows = tm // CONV_ROW_SPLITS
        pos = lax.broadcasted_iota(jnp.int32, (rows, e), 0)
        tail = tail_ref[...]
        xs = [x_ref[0, k * rows:(k + 1) * rows, :] for k in range(CONV_ROW_SPLITS)]
        proj = _conv_project(_rms(xs[0], gpre_ref[...]).astype(BF16), win_ref)
        for k in range(CONV_ROW_SPLITS):
            if k + 1 < CONV_ROW_SPLITS:
                next_proj = _conv_project(_rms(xs[k + 1], gpre_ref[...]).astype(BF16), win_ref)
            prev1 = tail[SUBLANES - 1:SUBLANES, :]
            prev2 = jnp.where(pos == 0, tail[SUBLANES - 2:SUBLANES - 1, :], prev1)
            gated, u = _conv_gate(proj, cw_ref, prev1, prev2, pos)
            y = _dot(gated.astype(BF16), wout_ref[...])
            y_ref[0, k * rows:(k + 1) * rows, :] = xs[k] + _rms(y, gpost_ref[...])
            tail = u[rows - SUBLANES:rows, :]
            if k + 1 < CONV_ROW_SPLITS:
                proj = next_proj
        tail_ref[...] = tail
        buf_ref[0] = tail[SUBLANES - (CONV_W - 1):SUBLANES, :]

    @pl.when(g == n_tiles)
    def _():
        x = xs_ref[...]
        nb = sbuf_ref.shape[0]
        lane_tiles, n, lanes = p1_s.shape
        h = _rms(x, gpre_ref[...]).astype(BF16)
        pos = lax.broadcasted_iota(jnp.int32, (n, e), 0) & (seq - 1)
        p1_s[...] = jnp.zeros_like(p1_s)
        p2_s[...] = jnp.zeros_like(p2_s)
        for c in range(lane_tiles):
            buf0 = sbuf_ref[:, c * lanes:(c + 1) * lanes]
            buf1 = sbuf_ref[:, e + c * lanes:e + (c + 1) * lanes]
            p1_s[c, pl.ds(0, nb, stride=seq), :] = buf1
            p2_s[c, pl.ds(0, nb, stride=seq), :] = buf0
            p2_s[c, pl.ds(1, nb, stride=seq), :] = buf1
        prev1 = jnp.concatenate([p1_s[c] for c in range(lane_tiles)], axis=1)
        prev2 = jnp.concatenate([p2_s[c] for c in range(lane_tiles)], axis=1)
        gated, u = _conv_gate(_conv_project(h, win_ref), cw_ref, prev1, prev2, pos)
        y = _dot(gated.astype(BF16), wout_ref[...])
        ys_ref[...] = x + _rms(y, gpost_ref[...])
        for c in range(lane_tiles):
            p1_s[c] = u[:, c * lanes:(c + 1) * lanes]
            for r in range(CONV_W - 1):
                new_sbuf_ref[:, r * e + c * lanes:r * e + (c + 1) * lanes] = (
                    p1_s[c, pl.ds(seq - (CONV_W - 1) + r, nb, stride=seq), :])


def _hgrn_prompt_kernel(j, nt, xp_ref, xr_ref, gpre_ref, win_in_ref, lbl_ref, gon_ref, wout_in_ref, gpost_ref,
                        y_ref, sout_ref, win_ref, wout_ref, st_s, dec_s, raw_s, gate_s,
                        qe_s, kd_s, q8_s, k8_s, v_s, *w_s):
    g = pl.program_id(0)
    e = win_ref.shape[1] // 4
    heads = e // HEAD_DIM
    tm = xp_ref.shape[1]
    fin = g - 2

    @pl.when(g == 0)
    def _():
        win_ref[...] = win_in_ref[...].astype(BF16)
        wout_ref[...] = wout_in_ref[...].astype(BF16)
        for ref in (dec_s, raw_s, gate_s, qe_s, kd_s, q8_s, k8_s, v_s) + tuple(w_s):
            ref[...] = jnp.zeros_like(ref)

    @pl.when(jnp.logical_or(g < 2, lax.rem(fin, nt) == 0))
    def _():
        st_s[...] = jnp.zeros_like(st_s)

    lv = _level_matrix(HGRN_CHUNK).astype(F32).astype(BF16)
    owned = [lv == i + 1 for i in range(len(w_s) + 1)]
    gon = gon_ref[...]
    chunks = tm // HGRN_CHUNK

    def chunk_scores(c, hd):
        rows = slice(c * HGRN_CHUNK, (c + 1) * HGRN_CHUNK)
        cols = slice(hd * HEAD_DIM, (hd + 1) * HEAD_DIM)
        a = jnp.where(owned[0], _dot_nt(q8_s[rows, cols], k8_s[rows, cols]).astype(BF16), 0.0)
        for i, w_ref in enumerate(w_s):
            wv = w_ref[rows, cols]
            a = jnp.where(owned[i + 1], _dot_nt(wv, wv).astype(BF16), a)
        return a

    def chunk_output(c, hd, a):
        rows = slice(c * HGRN_CHUNK, (c + 1) * HGRN_CHUNK)
        cols = slice(hd * HEAD_DIM, (hd + 1) * HEAD_DIM)
        vb = v_s[rows, cols]
        st = st_s[hd]
        o = _dot(a, vb) + _dot_nt(qe_s[rows, cols], st.astype(BF16))
        st3 = st.reshape(HEAD_DIM // SUBLANES, SUBLANES, HEAD_DIM) * dec_s[c, :, cols]
        st_s[hd] = st3.reshape(HEAD_DIM, HEAD_DIM) + _dot_tn(vb, kd_s[rows, cols])
        return (_rms(o, gon) * gate_s[rows, cols]).astype(BF16)

    groups = e // MXU_WIDTH
    heads_per_group = MXU_WIDTH // HEAD_DIM

    def part_cols(cg):
        return [slice(part * e + cg * MXU_WIDTH, part * e + (cg + 1) * MXU_WIDTH) for part in range(4)]

    def stage(lb, cg, c, q_raw, f_raw, v_raw, z_raw):
        gc = slice(cg * MXU_WIDTH, (cg + 1) * MXU_WIDTH)
        rows = slice(c * HGRN_CHUNK, (c + 1) * HGRN_CHUNK)
        q = _silu(q_raw) * (HEAD_DIM ** -0.5)
        lf2, k = _forget_gate(f_raw, lb[:, gc])
        v_s[rows, gc] = v_raw.astype(BF16)
        gate_s[rows, gc] = _silu(z_raw)
        qe, kd, q8, k8, w, decay = _chunk_operands(q, k, lf2)
        qe_s[rows, gc] = qe
        kd_s[rows, gc] = kd
        q8_s[rows, gc] = q8
        k8_s[rows, gc] = k8
        for w_ref, w_val in zip(w_s, w):
            w_ref[rows, gc] = w_val
        dec_s[c, :, gc] = jnp.broadcast_to(decay, (SUBLANES, MXU_WIDTH))

    def step(do_project, do_stage, do_finish):
        out_blocks = [[None] * heads for _ in range(chunks)]
        projections, raws, scores = [], {}, {}
        if do_project:
            h = _rms(xp_ref[0], gpre_ref[...]).astype(BF16)
            projections = [cols for cg in range(groups) for cols in part_cols(cg)]
        if do_stage:
            lb = _lower_bound(lbl_ref[...], j)
            for cg in range(groups):
                for c in range(chunks):
                    rows = slice(c * HGRN_CHUNK, (c + 1) * HGRN_CHUNK)
                    raws[cg, c] = [raw_s[rows, cols] for cols in part_cols(cg)]
        units = [(cg, c, hd) for cg in range(groups) for c in range(chunks)
                 for hd in range(cg * heads_per_group, (cg + 1) * heads_per_group)]
        for n in range(len(units) + SCORE_LOOKAHEAD):
            if do_finish and n < len(units):
                _, c, hd = units[n]
                scores[c, hd] = chunk_scores(c, hd)
            if projections:
                cols = projections.pop(0)
                raw_s[:, cols] = _dot(h, win_ref[:, cols])
            if n >= SCORE_LOOKAHEAD:
                done = n - SCORE_LOOKAHEAD
                cg, c, hd = units[done]
                if do_finish:
                    out_blocks[c][hd] = chunk_output(c, hd, scores.pop((c, hd)))
                if do_stage and (done + 1) % heads_per_group == 0:
                    stage(lb, cg, c, *raws.pop((cg, c)))
        for cols in projections:
            raw_s[:, cols] = _dot(h, win_ref[:, cols])
        if do_finish:
            gated = jnp.concatenate([jnp.concatenate(blocks, axis=1) for blocks in out_blocks], axis=0)
            y = _dot(gated, wout_ref[...])
            y_ref[0] = xr_ref[0] + _rms(y, gpost_ref[...])

    last = pl.num_programs(0) - 1
    pl.when(g == 0)(functools.partial(step, True, False, False))
    pl.when(jnp.logical_and(g > 0, g < last))(functools.partial(step, True, True, True))
    pl.when(g == last)(functools.partial(step, False, False, True))

    @pl.when(jnp.logical_and(g >= 2, lax.rem(fin, nt) == nt - 1))
    def _():
        for hd in range(heads):
            sout_ref[0, hd] = st_s[hd].T


def _hgrn_sample_kernel(j, seq, x_ref, gpre_ref, wq_ref, wf_ref, wv_ref, wz_ref, lbl_ref, gon_ref,
                        wout_ref, gpost_ref, s0_ref, *rest):
    y_ref, s1_ref, h_s, qe_s, kd_s, decp_s, v_s, o_s, gate_s = rest[-9:]
    hd, blk = pl.program_id(0), pl.program_id(1)
    n = x_ref.shape[0]
    nseq = s0_ref.shape[0]
    heads = o_s.shape[0]

    @pl.when(jnp.logical_and(hd == 0, blk == 0))
    def _():
        h_s[...] = _rms(x_ref[...], gpre_ref[...]).astype(BF16)

    @pl.when(blk == 0)
    def _():
        h = h_s[...]
        lb = _lower_bound(lbl_ref[...], j)
        q = _silu(_dot(h, wq_ref[...].astype(BF16))) * (HEAD_DIM ** -0.5)
        lf2, k = _forget_gate(_dot(h, wf_ref[...].astype(BF16)), lb)
        v = _dot(h, wv_ref[...].astype(BF16))
        gate_s[hd] = _silu(_dot(h, wz_ref[...].astype(BF16)))
        t = lax.broadcasted_iota(jnp.int32, (n, n), 0)
        s = lax.broadcasted_iota(jnp.int32, (n, n), 1)
        sh = seq.bit_length() - 1
        same_seq_causal = ((t >> sh) == (s >> sh)) & (t >= s)
        row = lax.broadcasted_iota(jnp.int32, (n, HEAD_DIM), 0)
        pos = row & (seq - 1)
        b = _cumsum_rows(lf2, seq)
        b_last = _group_allsum(lf2, seq)
        x_mid = b - _group_allsum(jnp.where(pos < seq // 2, lf2, 0.0), seq)
        a = _dot_nt((q * jnp.exp2(x_mid)).astype(BF16), (k * jnp.exp2(-x_mid)).astype(BF16))
        a = jnp.where(same_seq_causal, a, 0.0)
        o_s[hd] = _dot(a.astype(BF16), v.astype(BF16))
        qe_s[...] = q * jnp.exp2(b)
        kd_s[...] = k * jnp.exp2(b_last - b)
        v_s[...] = v
        dec = jnp.exp2(b_last)
        dsw = jnp.where((row & (SUBLANES - 1)) < seq,
                        pltpu.roll(dec, n - seq, axis=0), pltpu.roll(dec, seq, axis=0))
        hi = dsw.astype(BF16).astype(F32)
        mid = (dsw - hi).astype(BF16).astype(F32)
        lo = (dsw - hi - mid).astype(BF16).astype(F32)
        decp_s[...] = jnp.where(pos == 0, hi, jnp.where(pos == 1, mid, jnp.where(pos == 2, lo, 0.0)))

    row = lax.broadcasted_iota(jnp.int32, (SUBLANES, HEAD_DIM), 0)
    pad = jnp.zeros((HEAD_DIM - SUBLANES, HEAD_DIM), F32)
    pad2 = jnp.zeros((HEAD_DIM - SUBLANES, 2 * HEAD_DIM), F32)
    base = blk * (nseq * seq)

    def pairs_body(it, carry):
        for u in range(SAMPLE_PAIRS_PER_ITER):
            p = it * SAMPLE_PAIRS_PER_ITER + u
            rows = pl.ds(pl.multiple_of(base + p * SUBLANES, SUBLANES), SUBLANES)
            qe8 = qe_s[rows, :].astype(BF16)
            kd8 = kd_s[rows, :]
            dp8 = decp_s[rows, :]
            v8 = v_s[rows, :]
            o8 = o_s[hd, rows, :]
            for half in range(2):
                mine = (row >= seq) if half else (row < seq)
                s0 = s0_ref[2 * p + half]
                o8 = o8 + jnp.where(mine, _dot(qe8, s0.astype(BF16)), 0.0)
                lhs = jnp.concatenate([jnp.where(mine, kd8, dp8), pad], axis=0).astype(BF16)
                ones = jnp.where(mine, 0.0, 1.0)
                rhs = jnp.concatenate(
                    [jnp.concatenate([jnp.where(mine, v8, 0.0), ones], axis=1), pad2], axis=0).astype(BF16)
                upd = _dot_tn(lhs, rhs)
                s1_ref[2 * p + half] = upd[:, HEAD_DIM:] * s0 + upd[:, :HEAD_DIM]
            o_s[hd, rows, :] = o8
        return carry

    lax.fori_loop(0, nseq // (2 * SAMPLE_PAIRS_PER_ITER), pairs_body, 0)

    @pl.when(jnp.logical_and(hd == heads - 1, blk == pl.num_programs(1) - 1))
    def _():
        gon = gon_ref[...]
        gated = jnp.concatenate(
            [(_rms(o_s[i], gon) * gate_s[i]).astype(BF16) for i in range(heads)], axis=1)
        y = _dot(gated, wout_ref[...].astype(BF16))
        y_ref[...] = x_ref[...] + _rms(y, gpost_ref[...])


def _const_spec(shape):
    nd = len(shape)
    return pl.BlockSpec(shape, lambda *_: (0,) * nd)


def _layer_spec(arr, layer):
    nd = arr.ndim - 1
    return pl.BlockSpec((None,) + arr.shape[1:], lambda *_: (layer,) + (0,) * nd)


def _params(sem):
    return pltpu.CompilerParams(dimension_semantics=sem, vmem_limit_bytes=VMEM_LIMIT_BYTES)


def _conv_layer(i, j, x, xf, seq, state, gpre, win, cw, wout, gpost):
    bsz, t, d = x.shape
    n = xf.shape[0]
    nb = n // seq
    e = wout.shape[1]
    tm = CONV_PROMPT_TILE
    assert t % tm == 0 and (tm // CONV_ROW_SPLITS) % SUBLANES == 0
    assert CONV_W == 3 and seq & (seq - 1) == 0 and seq >= CONV_W - 1
    nt = t // tm
    n_tiles = bsz * nt

    def tile(g):
        g = jnp.minimum(g, n_tiles - 1)
        return g // nt, g % nt

    y, buf, ys, new_sbuf = pl.pallas_call(
        functools.partial(_conv_kernel, nt, seq),
        grid=(n_tiles + 1,),
        in_specs=[pl.BlockSpec((1, tm, d), lambda g: tile(g) + (0,)),
                  _const_spec(xf.shape), _layer_spec(state, j),
                  _layer_spec(gpre, i), _layer_spec(win, i), _layer_spec(cw, j),
                  _layer_spec(wout, i), _layer_spec(gpost, i)],
        out_specs=[pl.BlockSpec((1, tm, d), lambda g: tile(g) + (0,)),
                   pl.BlockSpec((1, CONV_W - 1, e), lambda g: (tile(g)[0], 0, 0)),
                   _const_spec((n, d)), _const_spec((nb, (CONV_W - 1) * e))],
        out_shape=[jax.ShapeDtypeStruct(x.shape, F32),
                   jax.ShapeDtypeStruct((bsz, CONV_W - 1, e), F32),
                   jax.ShapeDtypeStruct((n, d), F32),
                   jax.ShapeDtypeStruct((nb, (CONV_W - 1) * e), F32)],
        scratch_shapes=[pltpu.VMEM((SUBLANES, e), F32),
                        pltpu.VMEM(win.shape[1:], BF16), pltpu.VMEM(wout.shape[1:], BF16),
                        pltpu.VMEM((e // LANES, n, LANES), F32),
                        pltpu.VMEM((e // LANES, n, LANES), F32)],
        compiler_params=pltpu.CompilerParams(dimension_semantics=("arbitrary",),
                                             vmem_limit_bytes=CONV_VMEM_LIMIT_BYTES),
        name="conv_layer",
    )(x, xf, state, gpre, win, cw, wout, gpost)
    return y, buf, ys, new_sbuf.reshape(nb, CONV_W - 1, e)


def _hgrn_prompt(i, j, x, gpre, win, lb_logits, gon, wout, gpost):
    bsz, t, d = x.shape
    e = wout.shape[1]
    heads = e // HEAD_DIM
    tm = PROMPT_TILE
    assert t % tm == 0 and tm % HGRN_CHUNK == 0
    nt = t // tm
    n_tiles = bsz * nt
    n_bf16 = 5 + len(_level_sizes(HGRN_CHUNK))
    scratch = ([pltpu.VMEM(win.shape[1:], BF16), pltpu.VMEM(wout.shape[1:], BF16),
                pltpu.VMEM((heads, HEAD_DIM, HEAD_DIM), F32),
                pltpu.VMEM((tm // HGRN_CHUNK, SUBLANES, e), F32),
                pltpu.VMEM((tm, 4 * e), F32),
                pltpu.VMEM((tm, e), F32)]
               + [pltpu.VMEM((tm, e), BF16) for _ in range(n_bf16)])

    def projected(g):
        tile = jnp.minimum(g, n_tiles - 1)
        return tile // nt, tile % nt

    def finished(g):
        tile = jnp.maximum(g - 2, 0)
        return tile // nt, tile % nt

    return pl.pallas_call(
        functools.partial(_hgrn_prompt_kernel, j, nt),
        grid=(n_tiles + 2,),
        in_specs=[pl.BlockSpec((1, tm, d), lambda g: projected(g) + (0,)),
                  pl.BlockSpec((1, tm, d), lambda g: finished(g) + (0,)),
                  _layer_spec(gpre, i), _layer_spec(win, i), _const_spec(lb_logits.shape),
                  _layer_spec(gon, j), _layer_spec(wout, i), _layer_spec(gpost, i)],
        out_specs=[pl.BlockSpec((1, tm, d), lambda g: finished(g) + (0,)),
                   pl.BlockSpec((1, heads, HEAD_DIM, HEAD_DIM), lambda g: (finished(g)[0], 0, 0, 0))],
        out_shape=[jax.ShapeDtypeStruct(x.shape, F32),
                   jax.ShapeDtypeStruct((bsz, heads, HEAD_DIM, HEAD_DIM), F32)],
        scratch_shapes=scratch,
        compiler_params=_params(("arbitrary",)),
        name="hgrn_prompt",
    )(x, x, gpre, win, lb_logits, gon, wout, gpost)


def _hgrn_sample(i, j, xf, seq, state, states_out, gpre, win, lb_logits, gon, wout, gpost):
    n, d = xf.shape
    nb = n // seq
    e = wout.shape[1]
    heads = e // HEAD_DIM
    g = SAMPLE_SEQS_PER_STEP
    assert 2 * seq == SUBLANES and nb % g == 0
    assert g % (2 * SAMPLE_PAIRS_PER_ITER) == 0
    assert seq // 2 * -LOG_F_FLOOR * LOG2_E < F32_MAX_EXP2

    def head_cols(part):
        return pl.BlockSpec((None, d, HEAD_DIM), lambda hd, blk: (i, 0, part * heads + hd))

    st_spec = pl.BlockSpec((None, g, None, HEAD_DIM, HEAD_DIM), lambda hd, blk: (j, blk, hd, 0, 0))
    args = [xf, gpre, win, win, win, win, lb_logits, gon, wout, gpost, state]
    in_specs = [_const_spec(xf.shape), _layer_spec(gpre, i)] + [head_cols(part) for part in range(4)] + [
        pl.BlockSpec((lb_logits.shape[0], HEAD_DIM), lambda hd, blk: (0, hd)),
        _layer_spec(gon, j), _layer_spec(wout, i), _layer_spec(gpost, i), st_spec]
    aliases = {}
    if states_out is not None:
        args.append(states_out)
        in_specs.append(pl.BlockSpec(memory_space=pl.ANY))
        aliases = {len(args) - 1: 1}
    y, states_out = pl.pallas_call(
        functools.partial(_hgrn_sample_kernel, j, seq),
        grid=(heads, nb // g),
        in_specs=in_specs,
        out_specs=[_const_spec((n, d)), st_spec],
        out_shape=[jax.ShapeDtypeStruct((n, d), F32), jax.ShapeDtypeStruct(state.shape, F32)],
        scratch_shapes=[pltpu.VMEM((n, d), BF16)]
                       + [pltpu.VMEM((n, HEAD_DIM), F32) for _ in range(4)]
                       + [pltpu.VMEM((heads, n, HEAD_DIM), F32) for _ in range(2)],
        input_output_aliases=aliases,
        compiler_params=_params(("arbitrary", "arbitrary")),
        name="hgrn_sample",
    )(*args)
    return y, states_out


def kernel(x_prompt, x_sample, state_conv, state_hgrn, norm_pre, w_in, conv_w, hgrn_lb_logits,
           hgrn_onorm, w_out, norm_post):
    depth = w_in.shape[0]
    w_in32, w_out32, lb_logits = w_in, w_out, hgrn_lb_logits
    gpre, gpost, gon = norm_pre[:, None, :], norm_post[:, None, :], hgrn_onorm[:, None, :]
    conv_rows = state_conv.reshape(state_conv.shape[0], state_conv.shape[1], -1)
    nb, seq, d = x_sample.shape
    xp, xs = x_prompt, x_sample.reshape(nb * seq, d)
    conv_p, conv_s, hgrn_p, hgrn_s = [], [], [], None
    for i in range(depth):
        j = i // 2
        if i % 2 == 0:
            xp, bp, xs, bs = _conv_layer(i, j, xp, xs, seq, conv_rows, gpre, w_in32, conv_w, w_out32, gpost)
            conv_p.append(bp)
            conv_s.append(bs)
        else:
            xp, sp = _hgrn_prompt(i, j, xp, gpre, w_in32, lb_logits, gon, w_out32, gpost)
            xs, hgrn_s = _hgrn_sample(i, j, xs, seq, state_hgrn, hgrn_s, gpre, w_in32, lb_logits, gon,
                                      w_out32, gpost)
            hgrn_p.append(sp)
    return (xp, xs.reshape(nb, seq, d), jnp.stack(conv_p), jnp.stack(conv_s), jnp.stack(hgrn_p), hgrn_s)
```

```python
import functools
import math

import jax
import jax.numpy as jnp
from jax import lax
from jax.experimental import pallas as pl
from jax.experimental.pallas import tpu as pltpu

F32 = jnp.float32
BF16 = jnp.bfloat16

RMS_EPS = 1e-6
LOG_F_FLOOR = -20.0
LOG2_E = math.log2(math.e)
HEAD_DIM = 128
LANES = 128
SUBLANES = 8
MXU_WIDTH = 256
F32_MAX_EXP2 = 127
CONV_W = 3
HGRN_CHUNK = 128
BASE_BLOCK = SUBLANES
assert BASE_BLOCK // 2 * -LOG_F_FLOOR * LOG2_E < F32_MAX_EXP2
SCORE_LOOKAHEAD = 4
PROMPT_TILE = 256
CONV_PROMPT_TILE = 512
CONV_ROW_SPLITS = 2
SAMPLE_SEQS_PER_STEP = 128
SAMPLE_PAIRS_PER_ITER = 8
VMEM_LIMIT_BYTES = 56 * 1024 * 1024
CONV_VMEM_LIMIT_BYTES = 60 * 1024 * 1024


def _rms(x, g):
    ms = jnp.mean(x * x, axis=-1, keepdims=True)
    return x * lax.rsqrt(ms + RMS_EPS) * g


def _silu(x):
    return x * (1.0 / (1.0 + jnp.exp2(x * (-LOG2_E))))


def _dot(a, b):
    return jnp.dot(a, b, preferred_element_type=F32)


def _dot_nt(a, b):
    return lax.dot_general(a, b, (((1,), (1,)), ((), ())), preferred_element_type=F32)


def _dot_tn(a, b):
    return lax.dot_general(a, b, (((0,), (0,)), ((), ())), preferred_element_type=F32)


def _lower_bound(logits, j):
    m = jnp.max(logits, axis=0, keepdims=True)
    e = jnp.exp(logits - m)
    p = e / jnp.sum(e, axis=0, keepdims=True)
    if j == 0:
        return jnp.zeros_like(p[0:1])
    return jnp.sum(p[1:j + 1], axis=0, keepdims=True)


def _forget_gate(fpre, lb):
    e = jnp.exp2(jnp.abs(fpre) * (-LOG2_E))
    r = 1.0 / (1.0 + e)
    er = e * r
    pos = fpre >= 0
    sig = jnp.where(pos, r, er)
    nsig = jnp.where(pos, er, r)
    f = lb + (1.0 - lb) * sig
    log2_f = jnp.maximum(jnp.log(f) * LOG2_E, LOG_F_FLOOR * LOG2_E)
    k = jnp.minimum((1.0 - lb) * nsig, 1.0 - math.exp(LOG_F_FLOOR))
    return log2_f, k


def _cumsum_rows(x, group):
    pos = lax.broadcasted_iota(jnp.int32, x.shape, 0) & (group - 1)
    s = 1
    while s < group:
        x = x + jnp.where(pos >= s, pltpu.roll(x, s, axis=0), 0.0)
        s *= 2
    return x


def _group_allsum(x, group):
    n = x.shape[0]
    pos = lax.broadcasted_iota(jnp.int32, x.shape, 0) & (group - 1)
    s = 1
    while s < group:
        partner = jnp.where((pos & s) == 0, pltpu.roll(x, n - s, axis=0), pltpu.roll(x, s, axis=0))
        x = x + partner
        s *= 2
    return x


def _level_matrix(n):
    t = lax.broadcasted_iota(jnp.int32, (n, n), 0)
    s = lax.broadcasted_iota(jnp.int32, (n, n), 1)
    sh = BASE_BLOCK.bit_length() - 1
    lv = jnp.where(((t >> sh) == (s >> sh)) & (t >= s), 1, 0)
    c, i = 2 * BASE_BLOCK, 2
    while c <= n:
        half, sh = c // 2, sh + 1
        own = ((t >> sh) == (s >> sh)) & ((t & half) != 0) & ((s & half) == 0)
        lv = jnp.where(own, i, lv)
        c, i = 2 * c, i + 1
    return lv


def _level_sizes(chunk):
    sizes, c = [], 2 * BASE_BLOCK
    while c <= chunk:
        sizes.append(c)
        c *= 2
    return sizes


def _chunk_operands(q, k, lf2):
    c = q.shape[0]
    local = _cumsum_rows(lf2, BASE_BLOCK)
    b_parts, x_parts, carry = [], [], None
    for g in range(c // BASE_BLOCK):
        blk = local[g * BASE_BLOCK:(g + 1) * BASE_BLOCK]
        x_parts.append(blk - blk[BASE_BLOCK // 2 - 1:BASE_BLOCK // 2])
        if carry is not None:
            blk = blk + carry
        b_parts.append(blk)
        carry = blk[BASE_BLOCK - 1:BASE_BLOCK]
    b = jnp.concatenate(b_parts, axis=0)
    x = jnp.concatenate(x_parts, axis=0)
    b_last = carry
    qe = (q * jnp.exp2(b)).astype(BF16)
    kd = (k * jnp.exp2(b_last - b)).astype(BF16)
    q8 = (q * jnp.exp2(x)).astype(BF16)
    k8 = (k * jnp.exp2(-x)).astype(BF16)
    w = []
    for size in _level_sizes(c):
        half = size // 2
        parts = []
        for s0 in range(0, c, size):
            r = b[s0 + half - 1:s0 + half]
            lo, hi = slice(s0, s0 + half), slice(s0 + half, s0 + size)
            parts.append(k[lo] * jnp.exp2(r - b[lo]))
            parts.append(q[hi] * jnp.exp2(b[hi] - r))
        w.append(jnp.concatenate(parts, axis=0).astype(BF16))
    return qe, kd, q8, k8, w, jnp.exp2(b_last)


def _conv_project(h, win_ref):
    e = win_ref.shape[1] // 4
    return tuple(_dot(h, win_ref[:, part * e:(part + 1) * e]) for part in range(4))


def _conv_gate(proj, cw_ref, prev1, prev2, pos):
    v, bg, cg, z = proj
    u = cg * v
    u1 = jnp.where(pos >= 1, pltpu.roll(u, 1, axis=0), prev1)
    u2 = jnp.where(pos >= 2, pltpu.roll(u, 2, axis=0), prev2)
    conv = cw_ref[0:1, :] * u2 + cw_ref[1:2, :] * u1 + cw_ref[2:3, :] * u
    return bg * conv * _silu(z), u


def _conv_kernel(nt, seq, x_ref, xs_ref, sbuf_ref, gpre_ref, win_in_ref, cw_ref, wout_in_ref, gpost_ref,
                 y_ref, buf_ref, ys_ref, new_sbuf_ref, tail_ref, win_ref, wout_ref, p1_s, p2_s):
    g = pl.program_id(0)
    n_tiles = pl.num_programs(0) - 1
    e = win_ref.shape[1] // 4

    @pl.when(g == 0)
    def _():
        win_ref[...] = win_in_ref[...].astype(BF16)
        wout_ref[...] = wout_in_ref[...].astype(BF16)

    @pl.when(lax.rem(g, nt) == 0)
    def _():
        tail_ref[...] = jnp.zeros_like(tail_ref)

    @pl.when(g < n_tiles)
    def _():
        tm = x_ref.shape[1]
        rows = tm // CONV_ROW_SPLITS
        pos = lax.broadcasted_iota(jnp.int32, (rows, e), 0)
        tail = tail_ref[...]
        xs = [x_ref[0, k * rows:(k + 1) * rows, :] for k in range(CONV_ROW_SPLITS)]
        proj = _conv_project(_rms(xs[0], gpre_ref[...]).astype(BF16), win_ref)
        for k in range(CONV_ROW_SPLITS):
            if k + 1 < CONV_ROW_SPLITS:
                next_proj = _conv_project(_rms(xs[k + 1], gpre_ref[...]).astype(BF16), win_ref)
            prev1 = tail[SUBLANES - 1:SUBLANES, :]
            prev2 = jnp.where(pos == 0, tail[SUBLANES - 2:SUBLANES - 1, :], prev1)
            gated, u = _conv_gate(proj, cw_ref, prev1, prev2, pos)
            y = _dot(gated.astype(BF16), wout_ref[...])
            y_ref[0, k * rows:(k + 1) * rows, :] = xs[k] + _rms(y, gpost_ref[...])
            tail = u[rows - SUBLANES:rows, :]
            if k + 1 < CONV_ROW_SPLITS:
                proj = next_proj
        tail_ref[...] = tail
        buf_ref[0] = tail[SUBLANES - (CONV_W - 1):SUBLANES, :]

    @pl.when(g == n_tiles)
    def _():
        x = xs_ref[...]
        nb = sbuf_ref.shape[0]
        lane_tiles, n, lanes = p1_s.shape
        h = _rms(x, gpre_ref[...]).astype(BF16)
        pos = lax.broadcasted_iota(jnp.int32, (n, e), 0) & (seq - 1)
        p1_s[...] = jnp.zeros_like(p1_s)
        p2_s[...] = jnp.zeros_like(p2_s)
        for c in range(lane_tiles):
            buf0 = sbuf_ref[:, c * lanes:(c + 1) * lanes]
            buf1 = sbuf_ref[:, e + c * lanes:e + (c + 1) * lanes]
            p1_s[c, pl.ds(0, nb, stride=seq), :] = buf1
            p2_s[c, pl.ds(0, nb, stride=seq), :] = buf0
            p2_s[c, pl.ds(1, nb, stride=seq), :] = buf1
        prev1 = jnp.concatenate([p1_s[c] for c in range(lane_tiles)], axis=1)
        prev2 = jnp.concatenate([p2_s[c] for c in range(lane_tiles)], axis=1)
        gated, u = _conv_gate(_conv_project(h, win_ref), cw_ref, prev1, prev2, pos)
        y = _dot(gated.astype(BF16), wout_ref[...])
        ys_ref[...] = x + _rms(y, gpost_ref[...])
        for c in range(lane_tiles):
            p1_s[c] = u[:, c * lanes:(c + 1) * lanes]
            for r in range(CONV_W - 1):
                new_sbuf_ref[:, r * e + c * lanes:r * e + (c + 1) * lanes] = (
                    p1_s[c, pl.ds(seq - (CONV_W - 1) + r, nb, stride=seq), :])


def _hgrn_prompt_kernel(j, nt, xp_ref, xr_ref, gpre_ref, win_in_ref, lbl_ref, gon_ref, wout_in_ref, gpost_ref,
                        y_ref, sout_ref, win_ref, wout_ref, st_s, dec_s, raw_s, gate_s,
                        qe_s, kd_s, q8_s, k8_s, v_s, *w_s):
    g = pl.program_id(0)
    e = win_ref.shape[1] // 4
    heads = e // HEAD_DIM
    tm = xp_ref.shape[1]
    fin = g - 2

    @pl.when(g == 0)
    def _():
        win_ref[...] = win_in_ref[...].astype(BF16)
        wout_ref[...] = wout_in_ref[...].astype(BF16)
        for ref in (dec_s, raw_s, gate_s, qe_s, kd_s, q8_s, k8_s, v_s) + tuple(w_s):
            ref[...] = jnp.zeros_like(ref)

    @pl.when(jnp.logical_or(g < 2, lax.rem(fin, nt) == 0))
    def _():
        st_s[...] = jnp.zeros_like(st_s)

    lv = _level_matrix(HGRN_CHUNK).astype(F32).astype(BF16)
    owned = [lv == i + 1 for i in range(len(w_s) + 1)]
    gon = gon_ref[...]
    chunks = tm // HGRN_CHUNK

    def chunk_scores(c, hd):
        rows = slice(c * HGRN_CHUNK, (c + 1) * HGRN_CHUNK)
        cols = slice(hd * HEAD_DIM, (hd + 1) * HEAD_DIM)
        a = jnp.where(owned[0], _dot_nt(q8_s[rows, cols], k8_s[rows, cols]).astype(BF16), 0.0)
        for i, w_ref in enumerate(w_s):
            wv = w_ref[rows, cols]
            a = jnp.where(owned[i + 1], _dot_nt(wv, wv).astype(BF16), a)
        return a

    def chunk_output(c, hd, a):
        rows = slice(c * HGRN_CHUNK, (c + 1) * HGRN_CHUNK)
        cols = slice(hd * HEAD_DIM, (hd + 1) * HEAD_DIM)
        vb = v_s[rows, cols]
        st = st_s[hd]
        o = _dot(a, vb) + _dot_nt(qe_s[rows, cols], st.astype(BF16))
        st3 = st.reshape(HEAD_DIM // SUBLANES, SUBLANES, HEAD_DIM) * dec_s[c, :, cols]
        st_s[hd] = st3.reshape(HEAD_DIM, HEAD_DIM) + _dot_tn(vb, kd_s[rows, cols])
        return (_rms(o, gon) * gate_s[rows, cols]).astype(BF16)

    groups = e // MXU_WIDTH
    heads_per_group = MXU_WIDTH // HEAD_DIM

    def part_cols(cg):
        return [slice(part * e + cg * MXU_WIDTH, part * e + (cg + 1) * MXU_WIDTH) for part in range(4)]

    def stage(lb, cg, c, q_raw, f_raw, v_raw, z_raw):
        gc = slice(cg * MXU_WIDTH, (cg + 1) * MXU_WIDTH)
        rows = slice(c * HGRN_CHUNK, (c + 1) * HGRN_CHUNK)
        q = _silu(q_raw) * (HEAD_DIM ** -0.5)
        lf2, k = _forget_gate(f_raw, lb[:, gc])
        v_s[rows, gc] = v_raw.astype(BF16)
        gate_s[rows, gc] = _silu(z_raw)
        qe, kd, q8, k8, w, decay = _chunk_operands(q, k, lf2)
        qe_s[rows, gc] = qe
        kd_s[rows, gc] = kd
        q8_s[rows, gc] = q8
        k8_s[rows, gc] = k8
        for w_ref, w_val in zip(w_s, w):
            w_ref[rows, gc] = w_val
        dec_s[c, :, gc] = jnp.broadcast_to(decay, (SUBLANES, MXU_WIDTH))

    def step(do_project, do_stage, do_finish):
        out_blocks = [[None] * heads for _ in range(chunks)]
        projections, raws, scores = [], {}, {}
        if do_project:
            h = _rms(xp_ref[0], gpre_ref[...]).astype(BF16)
            projections = [cols for cg in range(groups) for cols in part_cols(cg)]
        if do_stage:
            lb = _lower_bound(lbl_ref[...], j)
            for cg in range(groups):
                for c in range(chunks):
                    rows = slice(c * HGRN_CHUNK, (c + 1) * HGRN_CHUNK)
                    raws[cg, c] = [raw_s[rows, cols] for cols in part_cols(cg)]
        units = [(cg, c, hd) for cg in range(groups) for c in range(chunks)
                 for hd in range(cg * heads_per_group, (cg + 1) * heads_per_group)]
        for n in range(len(units) + SCORE_LOOKAHEAD):
            if do_finish and n < len(units):
                _, c, hd = units[n]
                scores[c, hd] = chunk_scores(c, hd)
            if projections:
                cols = projections.pop(0)
                raw_s[:, cols] = _dot(h, win_ref[:, cols])
            if n >= SCORE_LOOKAHEAD:
                done = n - SCORE_LOOKAHEAD
                cg, c, hd = units[done]
                if do_finish:
                    out_blocks[c][hd] = chunk_output(c, hd, scores.pop((c, hd)))
                if do_stage and (done + 1) % heads_per_group == 0:
                    stage(lb, cg, c, *raws.pop((cg, c)))
        for cols in projections:
            raw_s[:, cols] = _dot(h, win_ref[:, cols])
        if do_finish:
            gated = jnp.concatenate([jnp.concatenate(blocks, axis=1) for blocks in out_blocks], axis=0)
            y = _dot(gated, wout_ref[...])
            y_ref[0] = xr_ref[0] + _rms(y, gpost_ref[...])

    last = pl.num_programs(0) - 1
    pl.when(g == 0)(functools.partial(step, True, False, False))
    pl.when(jnp.logical_and(g > 0, g < last))(functools.partial(step, True, True, True))
    pl.when(g == last)(functools.partial(step, False, False, True))

    @pl.when(jnp.logical_and(g >= 2, lax.rem(fin, nt) == nt - 1))
    def _():
        for hd in range(heads):
            sout_ref[0, hd] = st_s[hd].T


def _hgrn_sample_kernel(j, seq, x_ref, gpre_ref, wq_ref, wf_ref, wv_ref, wz_ref, lbl_ref, gon_ref,
                        wout_ref, gpost_ref, s0_ref, *rest):
    y_ref, s1_ref, h_s, qe_s, kd_s, decp_s, v_s, o_s, gate_s = rest[-9:]
    hd, blk = pl.program_id(0), pl.program_id(1)
    n = x_ref.shape[0]
    nseq = s0_ref.shape[0]
    heads = o_s.shape[0]

    @pl.when(jnp.logical_and(hd == 0, blk == 0))
    def _():
        h_s[...] = _rms(x_ref[...], gpre_ref[...]).astype(BF16)

    @pl.when(blk == 0)
    def _():
        h = h_s[...]
        lb = _lower_bound(lbl_ref[...], j)
        q = _silu(_dot(h, wq_ref[...].astype(BF16))) * (HEAD_DIM ** -0.5)
        lf2, k = _forget_gate(_dot(h, wf_ref[...].astype(BF16)), lb)
        v = _dot(h, wv_ref[...].astype(BF16))
        gate_s[hd] = _silu(_dot(h, wz_ref[...].astype(BF16)))
        t = lax.broadcasted_iota(jnp.int32, (n, n), 0)
        s = lax.broadcasted_iota(jnp.int32, (n, n), 1)
        sh = seq.bit_length() - 1
        same_seq_causal = ((t >> sh) == (s >> sh)) & (t >= s)
        row = lax.broadcasted_iota(jnp.int32, (n, HEAD_DIM), 0)
        pos = row & (seq - 1)
        b = _cumsum_rows(lf2, seq)
        b_last = _group_allsum(lf2, seq)
        x_mid = b - _group_allsum(jnp.where(pos < seq // 2, lf2, 0.0), seq)
        a = _dot_nt((q * jnp.exp2(x_mid)).astype(BF16), (k * jnp.exp2(-x_mid)).astype(BF16))
        a = jnp.where(same_seq_causal, a, 0.0)
        o_s[hd] = _dot(a.astype(BF16), v.astype(BF16))
        qe_s[...] = q * jnp.exp2(b)
        kd_s[...] = k * jnp.exp2(b_last - b)
        v_s[...] = v
        dec = jnp.exp2(b_last)
        dsw = jnp.where((row & (SUBLANES - 1)) < seq,
                        pltpu.roll(dec, n - seq, axis=0), pltpu.roll(dec, seq, axis=0))
        hi = dsw.astype(BF16).astype(F32)
        mid = (dsw - hi).astype(BF16).astype(F32)
        lo = (dsw - hi - mid).astype(BF16).astype(F32)
        decp_s[...] = jnp.where(pos == 0, hi, jnp.where(pos == 1, mid, jnp.where(pos == 2, lo, 0.0)))

    row = lax.broadcasted_iota(jnp.int32, (SUBLANES, HEAD_DIM), 0)
    pad = jnp.zeros((HEAD_DIM - SUBLANES, HEAD_DIM), F32)
    pad2 = jnp.zeros((HEAD_DIM - SUBLANES, 2 * HEAD_DIM), F32)
    base = blk * (nseq * seq)

    def pairs_body(it, carry):
        for u in range(SAMPLE_PAIRS_PER_ITER):
            p = it * SAMPLE_PAIRS_PER_ITER + u
            rows = pl.ds(pl.multiple_of(base + p * SUBLANES, SUBLANES), SUBLANES)
            qe8 = qe_s[rows, :].astype(BF16)
            kd8 = kd_s[rows, :]
            dp8 = decp_s[rows, :]
            v8 = v_s[rows, :]
            o8 = o_s[hd, rows, :]
            for half in range(2):
                mine = (row >= seq) if half else (row < seq)
                s0 = s0_ref[2 * p + half]
                o8 = o8 + jnp.where(mine, _dot(qe8, s0.astype(BF16)), 0.0)
                lhs = jnp.concatenate([jnp.where(mine, kd8, dp8), pad], axis=0).astype(BF16)
                ones = jnp.where(mine, 0.0, 1.0)
                rhs = jnp.concatenate(
                    [jnp.concatenate([jnp.where(mine, v8, 0.0), ones], axis=1), pad2], axis=0).astype(BF16)
                upd = _dot_tn(lhs, rhs)
                s1_ref[2 * p + half] = upd[:, HEAD_DIM:] * s0 + upd[:, :HEAD_DIM]
            o_s[hd, rows, :] = o8
        return carry

    lax.fori_loop(0, nseq // (2 * SAMPLE_PAIRS_PER_ITER), pairs_body, 0)

    @pl.when(jnp.logical_and(hd == heads - 1, blk == pl.num_programs(1) - 1))
    def _():
        gon = gon_ref[...]
        gated = jnp.concatenate(
            [(_rms(o_s[i], gon) * gate_s[i]).astype(BF16) for i in range(heads)], axis=1)
        y = _dot(gated, wout_ref[...].astype(BF16))
        y_ref[...] = x_ref[...] + _rms(y, gpost_ref[...])


def _const_spec(shape):
    nd = len(shape)
    return pl.BlockSpec(shape, lambda *_: (0,) * nd)


def _layer_spec(arr, layer):
    nd = arr.ndim - 1
    return pl.BlockSpec((None,) + arr.shape[1:], lambda *_: (layer,) + (0,) * nd)


def _params(sem):
    return pltpu.CompilerParams(dimension_semantics=sem, vmem_limit_bytes=VMEM_LIMIT_BYTES)


def _conv_layer(i, j, x, xf, seq, state, gpre, win, cw, wout, gpost):
    bsz, t, d = x.shape
    n = xf.shape[0]
    nb = n // seq
    e = wout.shape[1]
    tm = CONV_PROMPT_TILE
    assert t % tm == 0 and (tm // CONV_ROW_SPLITS) % SUBLANES == 0
    assert CONV_W == 3 and seq & (seq - 1) == 0 and seq >= CONV_W - 1
    nt = t // tm
    n_tiles = bsz * nt

    def tile(g):
        g = jnp.minimum(g, n_tiles - 1)
        return g // nt, g % nt

    y, buf, ys, new_sbuf = pl.pallas_call(
        functools.partial(_conv_kernel, nt, seq),
        grid=(n_tiles + 1,),
        in_specs=[pl.BlockSpec((1, tm, d), lambda g: tile(g) + (0,)),
                  _const_spec(xf.shape), _layer_spec(state, j),
                  _layer_spec(gpre, i), _layer_spec(win, i), _layer_spec(cw, j),
                  _layer_spec(wout, i), _layer_spec(gpost, i)],
        out_specs=[pl.BlockSpec((1, tm, d), lambda g: tile(g) + (0,)),
                   pl.BlockSpec((1, CONV_W - 1, e), lambda g: (tile(g)[0], 0, 0)),
                   _const_spec((n, d)), _const_spec((nb, (CONV_W - 1) * e))],
        out_shape=[jax.ShapeDtypeStruct(x.shape, F32),
                   jax.ShapeDtypeStruct((bsz, CONV_W - 1, e), F32),
                   jax.ShapeDtypeStruct((n, d), F32),
                   jax.ShapeDtypeStruct((nb, (CONV_W - 1) * e), F32)],
        scratch_shapes=[pltpu.VMEM((SUBLANES, e), F32),
                        pltpu.VMEM(win.shape[1:], BF16), pltpu.VMEM(wout.shape[1:], BF16),
                        pltpu.VMEM((e // LANES, n, LANES), F32),
                        pltpu.VMEM((e // LANES, n, LANES), F32)],
        compiler_params=pltpu.CompilerParams(dimension_semantics=("arbitrary",),
                                             vmem_limit_bytes=CONV_VMEM_LIMIT_BYTES),
        name="conv_layer",
    )(x, xf, state, gpre, win, cw, wout, gpost)
    return y, buf, ys, new_sbuf.reshape(nb, CONV_W - 1, e)


def _hgrn_prompt(i, j, x, gpre, win, lb_logits, gon, wout, gpost):
    bsz, t, d = x.shape
    e = wout.shape[1]
    heads = e // HEAD_DIM
    tm = PROMPT_TILE
    assert t % tm == 0 and tm % HGRN_CHUNK == 0
    nt = t // tm
    n_tiles = bsz * nt
    n_bf16 = 5 + len(_level_sizes(HGRN_CHUNK))
    scratch = ([pltpu.VMEM(win.shape[1:], BF16), pltpu.VMEM(wout.shape[1:], BF16),
                pltpu.VMEM((heads, HEAD_DIM, HEAD_DIM), F32),
                pltpu.VMEM((tm // HGRN_CHUNK, SUBLANES, e), F32),
                pltpu.VMEM((tm, 4 * e), F32),
                pltpu.VMEM((tm, e), F32)]
               + [pltpu.VMEM((tm, e), BF16) for _ in range(n_bf16)])

    def projected(g):
        tile = jnp.minimum(g, n_tiles - 1)
        return tile // nt, tile % nt

    def finished(g):
        tile = jnp.maximum(g - 2, 0)
        return tile // nt, tile % nt

    return pl.pallas_call(
        functools.partial(_hgrn_prompt_kernel, j, nt),
        grid=(n_tiles + 2,),
        in_specs=[pl.BlockSpec((1, tm, d), lambda g: projected(g) + (0,)),
                  pl.BlockSpec((1, tm, d), lambda g: finished(g) + (0,)),
                  _layer_spec(gpre, i), _layer_spec(win, i), _const_spec(lb_logits.shape),
                  _layer_spec(gon, j), _layer_spec(wout, i), _layer_spec(gpost, i)],
        out_specs=[pl.BlockSpec((1, tm, d), lambda g: finished(g) + (0,)),
                   pl.BlockSpec((1, heads, HEAD_DIM, HEAD_DIM), lambda g: (finished(g)[0], 0, 0, 0))],
        out_shape=[jax.ShapeDtypeStruct(x.shape, F32),
                   jax.ShapeDtypeStruct((bsz, heads, HEAD_DIM, HEAD_DIM), F32)],
        scratch_shapes=scratch,
        compiler_params=_params(("arbitrary",)),
        name="hgrn_prompt",
    )(x, x, gpre, win, lb_logits, gon, wout, gpost)


def _hgrn_sample(i, j, xf, seq, state, states_out, gpre, win, lb_logits, gon, wout, gpost):
    n, d = xf.shape
    nb = n // seq
    e = wout.shape[1]
    heads = e // HEAD_DIM
    g = SAMPLE_SEQS_PER_STEP
    assert 2 * seq == SUBLANES and nb % g == 0
    assert g % (2 * SAMPLE_PAIRS_PER_ITER) == 0
    assert seq // 2 * -LOG_F_FLOOR * LOG2_E < F32_MAX_EXP2

    def head_cols(part):
        return pl.BlockSpec((None, d, HEAD_DIM), lambda hd, blk: (i, 0, part * heads + hd))

    st_spec = pl.BlockSpec((None, g, None, HEAD_DIM, HEAD_DIM), lambda hd, blk: (j, blk, hd, 0, 0))
    args = [xf, gpre, win, win, win, win, lb_logits, gon, wout, gpost, state]
    in_specs = [_const_spec(xf.shape), _layer_spec(gpre, i)] + [head_cols(part) for part in range(4)] + [
        pl.BlockSpec((lb_logits.shape[0], HEAD_DIM), lambda hd, blk: (0, hd)),
        _layer_spec(gon, j), _layer_spec(wout, i), _layer_spec(gpost, i), st_spec]
    aliases = {}
    if states_out is not None:
        args.append(states_out)
        in_specs.append(pl.BlockSpec(memory_space=pl.ANY))
        aliases = {len(args) - 1: 1}
    y, states_out = pl.pallas_call(
        functools.partial(_hgrn_sample_kernel, j, seq),
        grid=(heads, nb // g),
        in_specs=in_specs,
        out_specs=[_const_spec((n, d)), st_spec],
        out_shape=[jax.ShapeDtypeStruct((n, d), F32), jax.ShapeDtypeStruct(state.shape, F32)],
        scratch_shapes=[pltpu.VMEM((n, d), BF16)]
                       + [pltpu.VMEM((n, HEAD_DIM), F32) for _ in range(4)]
                       + [pltpu.VMEM((heads, n, HEAD_DIM), F32) for _ in range(2)],
        input_output_aliases=aliases,
        compiler_params=_params(("arbitrary", "arbitrary")),
        name="hgrn_sample",
    )(*args)
    return y, states_out


def kernel(x_prompt, x_sample, state_conv, state_hgrn, norm_pre, w_in, conv_w, hgrn_lb_logits,
           hgrn_onorm, w_out, norm_post):
    depth = w_in.shape[0]
    w_in32, w_out32, lb_logits = w_in, w_out, hgrn_lb_logits
    gpre, gpost, gon = norm_pre[:, None, :], norm_post[:, None, :], hgrn_onorm[:, None, :]
    conv_rows = state_conv.reshape(state_conv.shape[0], state_conv.shape[1], -1)
    nb, seq, d = x_sample.shape
    xp, xs = x_prompt, x_sample.reshape(nb * seq, d)
    conv_p, conv_s, hgrn_p, hgrn_s = [], [], [], None
    for i in range(depth):
        j = i // 2
        if i % 2 == 0:
            xp, bp, xs, bs = _conv_layer(i, j, xp, xs, seq, conv_rows, gpre, w_in32, conv_w, w_out32, gpost)
            conv_p.append(bp)
            conv_s.append(bs)
        else:
            xp, sp = _hgrn_prompt(i, j, xp, gpre, w_in32, lb_logits, gon, w_out32, gpost)
            xs, hgrn_s = _hgrn_sample(i, j, xs, seq, state_hgrn, hgrn_s, gpre, w_in32, lb_logits, gon,
                                      w_out32, gpost)
            hgrn_p.append(sp)
    return (xp, xs.reshape(nb, seq, d), jnp.stack(conv_p), jnp.stack(conv_s), jnp.stack(hgrn_p), hgrn_s)
```
